```python
import math
import jax, jax.numpy as jnp
from jax import lax
import numpy as np

D_MODEL = 2048
BATCH = 4
SEQ = 4096
DEPTH = 2

N_MIXERS = 2
N_A_LAYERS = (DEPTH + 1) // 2
N_B_LAYERS = DEPTH // 2
EPS = 1e-6
D_RNN = D_MODEL
RG_HEADS = 16
RG_BLOCK = D_RNN // RG_HEADS
CONV_W = 4
RG_C = 8.0
MLA_HEADS = 16
Q_LORA = 512
KV_LORA = 512
QK_NOPE = 128
QK_ROPE = 64
V_DIM = 128
QK_DIM = QK_NOPE + QK_ROPE
ROPE_THETA = 10000.0
Q_BLOCK = 128
D_FF = 7 * D_MODEL // 2
N_EXPERTS = 8
TOP_K = 2
MOE_BLOCK = 256

kernel_name = "hybrid_rglru_mla_moe_trunk"


def rms_norm(x, g):
    xf = x.astype(jnp.float32)
    y = xf * lax.rsqrt(jnp.mean(xf * xf, axis=-1, keepdims=True) + EPS)
    return (y * g.astype(jnp.float32)).astype(x.dtype)


def swiglu(h, w_gate, w_up, w_down):
    return (jax.nn.silu(h @ w_gate) * (h @ w_up)) @ w_down


def causal_dwconv(x, w, b):
    S = x.shape[1]
    xp = jnp.pad(x, ((0, 0), (CONV_W - 1, 0), (0, 0)))
    out = b
    for k in range(CONV_W):
        out = out + xp[:, k:k + S] * w[k]
    return out


def rglru_mixer(h, w_in, conv_w, conv_b, w_a, b_a, w_x, b_x, lam, w_out):
    B, S, _ = h.shape
    u = h @ w_in
    gate_branch, rec = jnp.split(u, 2, axis=-1)
    xc = causal_dwconv(rec, conv_w, conv_b)
    xh = xc.reshape(B, S, RG_HEADS, RG_BLOCK)
    r = jax.nn.sigmoid((jnp.einsum('bshi,hij->bshj', xh, w_a).reshape(B, S, D_RNN) + b_a).astype(jnp.float32))
    i = jax.nn.sigmoid((jnp.einsum('bshi,hij->bshj', xh, w_x).reshape(B, S, D_RNN) + b_x).astype(jnp.float32))
    log_a = -RG_C * r * jax.nn.softplus(-lam.astype(jnp.float32))
    a = jnp.exp(log_a)
    mult = jnp.sqrt(-jnp.expm1(2.0 * log_a))
    b_in = mult * i * xc.astype(jnp.float32)

    def combine(left, right):
        a1, b1 = left
        a2, b2 = right
        return a1 * a2, a2 * b1 + b2

    _, hs = lax.associative_scan(combine, (a, b_in), axis=1)
    y = hs.astype(h.dtype) * jax.nn.gelu(gate_branch)
    return y @ w_out


def rope(x, positions):
    R = x.shape[-1]
    inv_freq = 1.0 / (ROPE_THETA ** (jnp.arange(0, R, 2, dtype=jnp.float32) / R))
    ang = positions.astype(jnp.float32)[..., None] * inv_freq
    cos = jnp.cos(ang)[:, :, None, :].astype(x.dtype)
    sin = jnp.sin(ang)[:, :, None, :].astype(x.dtype)
    x1, x2 = jnp.split(x, 2, axis=-1)
    return jnp.concatenate([x1 * cos - x2 * sin, x2 * cos + x1 * sin], axis=-1)


def mla_mixer(h, positions, w_in, q_norm, w_q_up, kv_norm, w_kv_up, w_out):
    B, S, _ = h.shape
    u = h @ w_in
    c_q = u[..., :Q_LORA]
    c_kv = u[..., Q_LORA:Q_LORA + KV_LORA]
    k_pe = u[..., Q_LORA + KV_LORA:]
    q = (rms_norm(c_q, q_norm) @ w_q_up).reshape(B, S, MLA_HEADS, QK_DIM)
    q = jnp.concatenate([q[..., :QK_NOPE], rope(q[..., QK_NOPE:], positions)], axis=-1)
    k_pe = rope(k_pe[:, :, None, :], positions)
    kv = (rms_norm(c_kv, kv_norm) @ w_kv_up).reshape(B, S, MLA_HEADS, QK_NOPE + V_DIM)
    k_nope, v = kv[..., :QK_NOPE], kv[..., QK_NOPE:]
    k = jnp.concatenate([k_nope, jnp.broadcast_to(k_pe, (B, S, MLA_HEADS, QK_ROPE))], axis=-1)
    scale = QK_DIM ** -0.5
    outs = []
    for s0 in range(0, S, Q_BLOCK):
        s1 = s0 + Q_BLOCK
        sc = jnp.einsum('bqhd,bkhd->bhqk', q[:, s0:s1], k[:, :s1],
                        preferred_element_type=jnp.float32) * scale
        q_idx = s0 + jnp.arange(Q_BLOCK)[:, None]
        k_idx = jnp.arange(s1)[None, :]
        sc = jnp.where(k_idx <= q_idx, sc, -jnp.inf)
        p = jax.nn.softmax(sc, axis=-1).astype(v.dtype)
        outs.append(jnp.einsum('bhqk,bkhv->bqhv', p, v[:, :s1]))
    o = jnp.concatenate(outs, axis=1).reshape(B, S, MLA_HEADS * V_DIM)
    return o @ w_out


def moe_swiglu(h, w_router, we_gate, we_up, we_down):
    B, S, D = h.shape
    N = B * S
    t = h.reshape(N, D)
    logits = jnp.dot(t, w_router, preferred_element_type=jnp.float32)
    top_v, top_e = lax.top_k(logits, TOP_K)
    gates = jax.nn.softmax(top_v, axis=-1)
    flat_e = top_e.reshape(-1)
    order = jnp.argsort(flat_e)
    e_sorted = flat_e[order]
    tok_sorted = order // TOP_K
    g_sorted = gates.reshape(-1)[order]
    counts = jnp.bincount(flat_e, length=N_EXPERTS)
    padded = ((counts + MOE_BLOCK - 1) // MOE_BLOCK) * MOE_BLOCK
    pad_end = jnp.cumsum(padded)
    pad_start = pad_end - padded
    start = jnp.cumsum(counts) - counts
    j = jnp.arange(N * TOP_K)
    dest = pad_start[e_sorted] + (j - start[e_sorted])
    P = ((N * TOP_K + MOE_BLOCK - 1) // MOE_BLOCK) * MOE_BLOCK + N_EXPERTS * MOE_BLOCK
    n_blocks = P // MOE_BLOCK
    buf = jnp.zeros((P, D), h.dtype).at[dest].set(t[tok_sorted])
    block_e = jnp.minimum(jnp.searchsorted(pad_end, jnp.arange(n_blocks) * MOE_BLOCK, side='right'),
                          N_EXPERTS - 1)

    def expert_block(args):
        xb, e = args
        return swiglu(xb, we_gate[e], we_up[e], we_down[e])

    y_buf = lax.map(expert_block, (buf.reshape(n_blocks, MOE_BLOCK, D), block_e)).reshape(P, D)
    y = y_buf[dest] * g_sorted[:, None].astype(h.dtype)
    out = jax.ops.segment_sum(y, tok_sorted, num_segments=N)
    return out.reshape(B, S, D)


def setup_inputs(seed: int = 0) -> dict:
    key = jax.random.key(seed)
    ks = iter(jax.random.split(key, 40))
    f32 = jnp.float32

    def w(shape, fan_in):
        return jax.random.normal(next(ks), shape, f32) * (fan_in ** -0.5)

    def gain(shape):
        return 1.0 + 0.02 * jax.random.normal(next(ks), shape, f32)

    def bias(shape):
        return 0.02 * jax.random.normal(next(ks), shape, f32)

    nA, nB = N_A_LAYERS, N_B_LAYERS
    x = jax.random.normal(next(ks), (BATCH, SEQ, D_MODEL), f32)
    positions = jnp.broadcast_to(jnp.arange(SEQ, dtype=jnp.int32), (BATCH, SEQ))
    a_base = jax.random.uniform(next(ks), (nA, D_RNN), f32, minval=0.9, maxval=0.999)
    rg_lambda = jnp.log(a_base) - jnp.log1p(-a_base)
    return {
        "x": x,
        "positions": positions,
        "a_norm_mix": gain((nA, D_MODEL)),
        "rg_w_in": w((nA, D_MODEL, 2 * D_RNN), D_MODEL),
        "rg_conv_w": w((nA, CONV_W, D_RNN), CONV_W),
        "rg_conv_b": bias((nA, D_RNN)),
        "rg_w_a": w((nA, RG_HEADS, RG_BLOCK, RG_BLOCK), RG_BLOCK),
        "rg_b_a": bias((nA, D_RNN)),
        "rg_w_x": w((nA, RG_HEADS, RG_BLOCK, RG_BLOCK), RG_BLOCK),
        "rg_b_x": bias((nA, D_RNN)),
        "rg_lambda": rg_lambda,
        "rg_w_out": w((nA, D_RNN, D_MODEL), D_RNN),
        "a_norm_ffn": gain((nA, D_MODEL)),
        "ff_w_gate": w((nA, D_MODEL, D_FF), D_MODEL),
        "ff_w_up": w((nA, D_MODEL, D_FF), D_MODEL),
        "ff_w_down": w((nA, D_FF, D_MODEL), D_FF),
        "b_norm_mix": gain((nB, D_MODEL)),
        "mla_w_in": w((nB, D_MODEL, Q_LORA + KV_LORA + QK_ROPE), D_MODEL),
        "mla_q_norm": gain((nB, Q_LORA)),
        "mla_w_q_up": w((nB, Q_LORA, MLA_HEADS * QK_DIM), Q_LORA),
        "mla_kv_norm": gain((nB, KV_LORA)),
        "mla_w_kv_up": w((nB, KV_LORA, MLA_HEADS * (QK_NOPE + V_DIM)), KV_LORA),
        "mla_w_out": w((nB, MLA_HEADS * V_DIM, D_MODEL), MLA_HEADS * V_DIM),
        "b_norm_ffn": gain((nB, D_MODEL)),
        "moe_w_router": w((nB, D_MODEL, N_EXPERTS), D_MODEL),
        "moe_w_gate": w((nB, N_EXPERTS, D_MODEL, D_FF), D_MODEL),
        "moe_w_up": w((nB, N_EXPERTS, D_MODEL, D_FF), D_MODEL),
        "moe_w_down": w((nB, N_EXPERTS, D_FF, D_MODEL), D_FF),
        "final_norm": gain((D_MODEL,)),
    }


def reference(x, positions, a_norm_mix, rg_w_in, rg_conv_w, rg_conv_b, rg_w_a, rg_b_a, rg_w_x, rg_b_x,
              rg_lambda, rg_w_out, a_norm_ffn, ff_w_gate, ff_w_up, ff_w_down, b_norm_mix, mla_w_in,
              mla_q_norm, mla_w_q_up, mla_kv_norm, mla_w_kv_up, mla_w_out, b_norm_ffn, moe_w_router,
              moe_w_gate, moe_w_up, moe_w_down, final_norm):
    for i in range(DEPTH):
        li = i // N_MIXERS
        if i % N_MIXERS == 0:
            x = x + rglru_mixer(rms_norm(x, a_norm_mix[li]), rg_w_in[li], rg_conv_w[li], rg_conv_b[li],
                                rg_w_a[li], rg_b_a[li], rg_w_x[li], rg_b_x[li], rg_lambda[li], rg_w_out[li])
            x = x + swiglu(rms_norm(x, a_norm_ffn[li]), ff_w_gate[li], ff_w_up[li], ff_w_down[li])
        else:
            x = x + mla_mixer(rms_norm(x, b_norm_mix[li]), positions, mla_w_in[li], mla_q_norm[li],
                              mla_w_q_up[li], mla_kv_norm[li], mla_w_kv_up[li], mla_w_out[li])
            x = x + moe_swiglu(rms_norm(x, b_norm_ffn[li]), moe_w_router[li], moe_w_gate[li],
                               moe_w_up[li], moe_w_down[li])
    return rms_norm(x, final_norm)
```

```python
import functools
import math

import jax
import jax.numpy as jnp
from jax import lax
from jax.experimental import pallas as pl
from jax.experimental.pallas import tpu as pltpu

EPS = 1e-6
RG_HEADS = 16
RG_BLOCK = 128
CONV_W = 4
RG_C = 8.0
MLA_HEADS = 16
Q_LORA = 512
KV_LORA = 512
QK_NOPE = 128
QK_ROPE = 64
V_DIM = 128
QK_DIM = QK_NOPE + QK_ROPE
ROPE_THETA = 10000.0
N_EXPERTS = 8
LANES = 128
SUBLANES = 8
VMEM_LIMIT_BYTES = 56 * 1024 * 1024

BF16 = jnp.bfloat16
F32 = jnp.float32


def _params(*sem):
    return pltpu.CompilerParams(dimension_semantics=sem, vmem_limit_bytes=VMEM_LIMIT_BYTES)


def _rms(x, g):
    return x * lax.rsqrt(jnp.mean(x * x, axis=-1, keepdims=True) + EPS) * g


def _dot(a, b):
    return jnp.dot(a, b, preferred_element_type=F32)


def _sigmoid(x):
    return 1.0 / (1.0 + jnp.exp(-x))


def _norm_mm_kernel(x_ref, g_ref, w_ref, o_ref, hn_ref):
    @pl.when(pl.program_id(1) == 0)
    def _():
        hn_ref[...] = _rms(x_ref[...], g_ref[...]).astype(BF16)

    o_ref[...] = _dot(hn_ref[...], w_ref[...]).astype(o_ref.dtype)


def norm_mm(x, g, w, *, tm, tn, out_dtype):
    n, d = x.shape
    f = w.shape[1]
    return pl.pallas_call(
        _norm_mm_kernel,
        grid=(n // tm, f // tn),
        in_specs=[pl.BlockSpec((tm, d), lambda i, j: (i, 0)),
                  pl.BlockSpec((1, d), lambda i, j: (0, 0)),
                  pl.BlockSpec((d, tn), lambda i, j: (0, j))],
        out_specs=pl.BlockSpec((tm, tn), lambda i, j: (i, j)),
        out_shape=jax.ShapeDtypeStruct((n, f), out_dtype),
        scratch_shapes=[pltpu.VMEM((tm, d), BF16)],
        compiler_params=_params("parallel", "arbitrary"),
        name="norm_mm",
    )(x, g, w)


def _mm_res_kernel(y_ref, w_ref, r_ref, o_ref):
    o_ref[...] = r_ref[...] + _dot(y_ref[...], w_ref[...])


def mm_res(y, w, res, *, tm, tn):
    n, k = y.shape
    f = w.shape[1]
    return pl.pallas_call(
        _mm_res_kernel,
        grid=(n // tm, f // tn),
        in_specs=[pl.BlockSpec((tm, k), lambda i, j: (i, 0)),
                  pl.BlockSpec((k, tn), lambda i, j: (0, j)),
                  pl.BlockSpec((tm, tn), lambda i, j: (i, j))],
        out_specs=pl.BlockSpec((tm, tn), lambda i, j: (i, j)),
        out_shape=jax.ShapeDtypeStruct((n, f), F32),
        compiler_params=_params("parallel", "parallel"),
        name="mm_res",
    )(y, w, res)


def _gelu_tanh(x):
    c = math.sqrt(2.0 / math.pi)
    return 0.5 * x * (1.0 + jnp.tanh(c * (x + 0.044715 * (x * x * x))))


def _softplus(z):
    return jnp.maximum(z, 0.0) + jnp.log1p(jnp.exp(-jnp.abs(z)))


def _rglru_kernel(gate_ref, rec_ref, cw_ref, cb_ref, wax_ref, ba_ref, bx_ref, lam_ref,
                  y_ref, xbuf, a_s, b_s, h_s):
    s = pl.program_id(1)
    t = rec_ref.shape[0]
    d = rec_ref.shape[1]

    @pl.when(s == 0)
    def _():
        xbuf[0:SUBLANES, :] = jnp.zeros((SUBLANES, d), F32)
        h_s[...] = jnp.zeros_like(h_s)

    @pl.when(s > 0)
    def _():
        xbuf[0:SUBLANES, :] = xbuf[t:t + SUBLANES, :]

    xbuf[SUBLANES:t + SUBLANES, :] = rec_ref[...]

    row = lax.broadcasted_iota(jnp.int32, (t, RG_BLOCK), 0) % SUBLANES
    for h in range(RG_HEADS):
        c0, c1 = h * RG_BLOCK, (h + 1) * RG_BLOCK
        xc = cb_ref[:, c0:c1] + jnp.zeros((t, RG_BLOCK), F32)
        for k in range(CONV_W):
            sh = CONV_W - 1 - k
            xc = xc + xbuf[SUBLANES - sh:SUBLANES - sh + t, c0:c1] * cw_ref[k:k + 1, c0:c1]
        gg = _dot(xc.astype(BF16), wax_ref[h])
        r = _sigmoid(gg[:, :RG_BLOCK] + ba_ref[:, c0:c1])
        i = _sigmoid(gg[:, RG_BLOCK:] + bx_ref[:, c0:c1])
        log_a = -RG_C * r * _softplus(-lam_ref[:, c0:c1])
        a = jnp.exp(log_a)
        b = jnp.sqrt(1.0 - a * a) * i * xc
        for sft in (1, 2, 4):
            a_sh = pltpu.roll(a, sft, axis=0)
            b_sh = pltpu.roll(b, sft, axis=0)
            m = row >= sft
            b = jnp.where(m, a * b_sh + b, b)
            a = jnp.where(m, a * a_sh, a)
        a_s[:, c0:c1] = a
        b_s[:, c0:c1] = b

    def group(gi, hc):
        r0 = pl.multiple_of(gi * SUBLANES, SUBLANES)
        rows = b_s[pl.ds(r0, SUBLANES), :] + a_s[pl.ds(r0, SUBLANES), :] * hc
        b_s[pl.ds(r0, SUBLANES), :] = rows
        return rows[SUBLANES - 1:SUBLANES, :]

    h_s[...] = lax.fori_loop(0, t // SUBLANES, group, h_s[...])
    y_ref[...] = (b_s[...] * _gelu_tanh(gate_ref[...])).astype(y_ref.dtype)


def rglru_core(u, conv_w, conv_b, wax, b_a, b_x, lam, *, batch, seq, t):
    n = u.shape[0]
    d = u.shape[1] // 2
    ns = seq // t
    vec = pl.BlockSpec((1, d), lambda b, s: (0, 0))
    return pl.pallas_call(
        _rglru_kernel,
        grid=(batch, ns),
        in_specs=[pl.BlockSpec((t, d), lambda b, s: (b * ns + s, 0)),
                  pl.BlockSpec((t, d), lambda b, s: (b * ns + s, 1)),
                  pl.BlockSpec((CONV_W, d), lambda b, s: (0, 0)),
                  vec,
                  pl.BlockSpec((RG_HEADS, RG_BLOCK, 2 * RG_BLOCK), lambda b, s: (0, 0, 0)),
                  vec, vec, vec],
        out_specs=pl.BlockSpec((t, d), lambda b, s: (b * ns + s, 0)),
        out_shape=jax.ShapeDtypeStruct((n, d), BF16),
        scratch_shapes=[pltpu.VMEM((t + SUBLANES, d), F32),
                        pltpu.VMEM((t, d), F32),
                        pltpu.VMEM((t, d), F32),
                        pltpu.VMEM((1, d), F32)],
        compiler_params=_params("arbitrary", "arbitrary"),
        name="rglru_core",
    )(u, u, conv_w, conv_b, wax, b_a, b_x, lam)


def _swiglu_acc(h, wg_ref, wu_ref, wd_ref):
    g = _dot(h, wg_ref[...])
    u = _dot(h, wu_ref[...])
    a = (g * _sigmoid(g) * u).astype(BF16)
    return _dot(a, wd_ref[...])


def _ffn_kernel(x_ref, g_ref, wg_ref, wu_ref, wd_ref, o_ref, hn_ref):
    @pl.when(pl.program_id(1) == 0)
    def _():
        x = x_ref[...]
        hn_ref[...] = _rms(x, g_ref[...]).astype(BF16)
        o_ref[...] = x

    o_ref[...] += _swiglu_acc(hn_ref[...], wg_ref, wu_ref, wd_ref)


def ffn_dense(x, g, wg, wu, wd, *, tm, tc):
    n, d = x.shape
    f = wg.shape[1]
    return pl.pallas_call(
        _ffn_kernel,
        grid=(n // tm, f // tc),
        in_specs=[pl.BlockSpec((tm, d), lambda i, j: (i, 0)),
                  pl.BlockSpec((1, d), lambda i, j: (0, 0)),
                  pl.BlockSpec((d, tc), lambda i, j: (0, j)),
                  pl.BlockSpec((d, tc), lambda i, j: (0, j)),
                  pl.BlockSpec((tc, d), lambda i, j: (j, 0))],
        out_specs=pl.BlockSpec((tm, d), lambda i, j: (i, 0)),
        out_shape=jax.ShapeDtypeStruct((n, d), F32),
        scratch_shapes=[pltpu.VMEM((tm, d), BF16)],
        compiler_params=_params("parallel", "arbitrary"),
        name="ffn_dense",
    )(x, g, wg, wu, wd)


def _mla_proj_kernel(x_ref, pos_ref, g_ref, win_ref, qn_ref, wq_ref, kvn_ref, wkv_ref,
                     freq_ref, sign_ref, qnope_ref, qpe_ref, knope_ref, kpe_ref, v_ref):
    d = x_ref.shape[1]
    hn = _rms(x_ref[...], g_ref[...]).astype(BF16)
    u = _dot(hn, win_ref[...])
    ang = pos_ref[...].astype(F32) * freq_ref[...]
    cos = jnp.cos(ang)
    sin = jnp.sin(ang) * sign_ref[...]

    def rope(xs):
        return xs * cos + pltpu.roll(xs, LANES // 2, axis=1) * sin

    kpe_ref[...] = rope(u[:, Q_LORA + KV_LORA:]).astype(BF16)
    q = _dot(_rms(u[:, :Q_LORA], qn_ref[...]).astype(BF16), wq_ref[...])
    qnope_ref[...] = q[:, :d].astype(BF16)
    for h in range(MLA_HEADS):
        c0, c1 = h * LANES, (h + 1) * LANES
        qpe_ref[:, c0:c1] = rope(q[:, d + c0:d + c1]).astype(BF16)
    kv = _dot(_rms(u[:, Q_LORA:Q_LORA + KV_LORA], kvn_ref[...]).astype(BF16), wkv_ref[...])
    knope_ref[...] = kv[:, :d].astype(BF16)
    v_ref[...] = kv[:, d:].astype(BF16)


def mla_proj(x, pos, g, w_in, q_norm, w_q, kv_norm, w_kv, freq, sign, *, tm):
    n, d = x.shape
    row = lambda w: pl.BlockSpec((tm, w), lambda i: (i, 0))
    full = lambda a: pl.BlockSpec(a.shape, lambda i: (0, 0))
    big = jax.ShapeDtypeStruct((n, d), BF16)
    return pl.pallas_call(
        _mla_proj_kernel,
        grid=(n // tm,),
        in_specs=[row(d), row(1), full(g), full(w_in), full(q_norm), full(w_q), full(kv_norm),
                  full(w_kv), full(freq), full(sign)],
        out_specs=[row(d), row(d), row(d), row(LANES), row(d)],
        out_shape=[big, big, big, jax.ShapeDtypeStruct((n, LANES), BF16), big],
        compiler_params=_params("parallel"),
        name="mla_proj",
    )(x, pos, g, w_in, q_norm, w_q, kv_norm, w_kv, freq, sign)


def _attn_kernel(qn_ref, qp_ref, kn_ref, kp_ref, v_ref, o_ref, m_s, l_s, acc_s):
    qi = pl.program_id(2)
    tq = qn_ref.shape[0]
    scale = QK_DIM ** -0.5
    q = jnp.concatenate([qn_ref[...], qp_ref[...]], axis=1)
    m_s[...] = jnp.full_like(m_s, -jnp.inf)
    l_s[...] = jnp.zeros_like(l_s)
    acc_s[...] = jnp.zeros_like(acc_s)

    def chunk(kj, masked):
        k0 = pl.multiple_of(kj * tq, tq)
        k = jnp.concatenate([kn_ref[pl.ds(k0, tq), :], kp_ref[pl.ds(k0, tq), :]], axis=1)
        s = lax.dot_general(q, k, (((1,), (1,)), ((), ())), preferred_element_type=F32) * scale
        if masked:
            q_idx = lax.broadcasted_iota(jnp.int32, (tq, tq), 0)
            k_idx = lax.broadcasted_iota(jnp.int32, (tq, tq), 1)
            s = jnp.where(k_idx <= q_idx, s, -jnp.inf)
        m_prev = m_s[...]
        m_new = jnp.maximum(m_prev, jnp.max(s, axis=1, keepdims=True))
        p = jnp.exp(s - m_new)
        alpha = jnp.exp(m_prev - m_new)
        l_s[...] = alpha * l_s[...] + jnp.sum(p, axis=1, keepdims=True)
        acc_s[...] = alpha * acc_s[...] + _dot(p.astype(BF16), v_ref[pl.ds(k0, tq), :])
        m_s[...] = m_new

    def body(kj, c):
        chunk(kj, False)
        return c

    lax.fori_loop(0, qi, body, 0)
    chunk(qi, True)
    o_ref[...] = (acc_s[...] / l_s[...]).astype(o_ref.dtype)


def attention(q_nope, q_pe, k_nope, k_pe, v, *, batch, seq, tq):
    n, d = q_nope.shape
    nq = seq // tq
    qspec = pl.BlockSpec((tq, LANES), lambda b, h, i: (b * nq + i, h))
    kspec = pl.BlockSpec((seq, LANES), lambda b, h, i: (b, h))
    return pl.pallas_call(
        _attn_kernel,
        grid=(batch, MLA_HEADS, nq),
        in_specs=[qspec, qspec, kspec,
                  pl.BlockSpec((seq, LANES), lambda b, h, i: (b, 0)),
                  kspec],
        out_specs=qspec,
        out_shape=jax.ShapeDtypeStruct((n, d), BF16),
        scratch_shapes=[pltpu.VMEM((tq, 1), F32), pltpu.VMEM((tq, 1), F32),
                        pltpu.VMEM((tq, LANES), F32)],
        compiler_params=_params("parallel", "parallel", "arbitrary"),
        name="attention",
    )(q_nope, q_pe, k_nope, k_pe, v)


def _out_router_kernel(o_ref, w_ref, r_ref, g_ref, wr_ref, x_ref, hn_ref, route_ref, cnt_ref, carry):
    i = pl.program_id(0)
    tm = o_ref.shape[0]

    @pl.when(i == 0)
    def _():
        carry[...] = jnp.zeros_like(carry)

    x = r_ref[...] + _dot(o_ref[...], w_ref[...])
    x_ref[...] = x
    hn = _rms(x, g_ref[...])
    hn_ref[...] = hn
    logits = jnp.dot(hn, wr_ref[...], preferred_element_type=F32, precision=lax.Precision.HIGHEST)
    lane = lax.broadcasted_iota(jnp.int32, (tm, LANES), 1)
    lane_f = lane.astype(F32)
    neg = -jnp.inf
    logits = jnp.where(lane < N_EXPERTS, logits, neg)
    m1 = jnp.max(logits, axis=1, keepdims=True)
    e1 = jnp.min(jnp.where(logits == m1, lane_f, float(LANES)), axis=1, keepdims=True)
    rest = jnp.where(lane_f == e1, neg, logits)
    m2 = jnp.max(rest, axis=1, keepdims=True)
    e2 = jnp.min(jnp.where(rest == m2, lane_f, float(LANES)), axis=1, keepdims=True)
    z = jnp.exp(m2 - m1)
    g1 = 1.0 / (1.0 + z)
    g2 = z / (1.0 + z)
    sel1 = lane_f == e1
    sel2 = lane_f == e2
    sel = jnp.where(sel1 | sel2, 1.0, 0.0)
    rr = lax.broadcasted_iota(jnp.int32, (tm, tm), 0)
    cc = lax.broadcasted_iota(jnp.int32, (tm, tm), 1)
    tril = jnp.where(cc < rr, 1.0, 0.0).astype(BF16)
    before = _dot(tril, sel.astype(BF16)) + carry[...]
    rank1 = jnp.sum(jnp.where(sel1, before, 0.0), axis=1, keepdims=True)
    rank2 = jnp.sum(jnp.where(sel2, before, 0.0), axis=1, keepdims=True)
    total = carry[...] + jnp.sum(sel, axis=0, keepdims=True)
    carry[...] = total
    cnt_ref[...] = jnp.broadcast_to(total, cnt_ref.shape)
    route = jnp.zeros((tm, LANES), F32)
    for col, val in enumerate((e1, e2, g1, g2, rank1, rank2)):
        route = jnp.where(lane == col, val, route)
    route_ref[...] = route


def out_router(o, w_out, res, g, w_router, *, tm):
    n, d = res.shape
    row = lambda w: pl.BlockSpec((tm, w), lambda i: (i, 0))
    full = lambda a: pl.BlockSpec(a.shape, lambda i: (0, 0))
    return pl.pallas_call(
        _out_router_kernel,
        grid=(n // tm,),
        in_specs=[row(d), full(w_out), row(d), full(g), full(w_router)],
        out_specs=[row(d), row(d), row(LANES), pl.BlockSpec((SUBLANES, LANES), lambda i: (0, 0))],
        out_shape=[jax.ShapeDtypeStruct((n, d), F32), jax.ShapeDtypeStruct((n, d), F32),
                   jax.ShapeDtypeStruct((n, LANES), F32),
                   jax.ShapeDtypeStruct((SUBLANES, LANES), F32)],
        scratch_shapes=[pltpu.VMEM((1, LANES), F32)],
        compiler_params=_params("arbitrary"),
        name="out_router",
    )(o, w_out, res, g, w_router)


def _row_copy(src_hbm, row, dst, r, sem):
    return pltpu.make_async_copy(src_hbm.at[pl.ds(row, 1), :], dst.at[pl.ds(r, 1), :], sem)


def _dispatch_kernel(src_ref, hn_hbm, o_ref, gbuf, sem):
    tm = o_ref.shape[0]
    base = pl.program_id(0) * tm

    def issue(r, c):
        _row_copy(hn_hbm, src_ref[base + r], gbuf, r, sem).start()
        return c

    lax.fori_loop(0, tm, issue, 0)

    def drain(r, c):
        _row_copy(hn_hbm, src_ref[base + r], gbuf, r, sem).wait()
        return c

    lax.fori_loop(0, tm, drain, 0)
    o_ref[...] = gbuf[...].astype(o_ref.dtype)


def dispatch(src_tok, hn, *, p, tm):
    d = hn.shape[1]
    return pl.pallas_call(
        _dispatch_kernel,
        grid_spec=pltpu.PrefetchScalarGridSpec(
            num_scalar_prefetch=1,
            grid=(p // tm,),
            in_specs=[pl.BlockSpec(memory_space=pl.ANY)],
            out_specs=pl.BlockSpec((tm, d), lambda i, src: (i, 0)),
            scratch_shapes=[pltpu.VMEM((tm, d), F32), pltpu.SemaphoreType.DMA(())]),
        out_shape=jax.ShapeDtypeStruct((p, d), BF16),
        compiler_params=_params("arbitrary"),
        name="moe_dispatch",
    )(src_tok, hn)


def _moe_ffn_kernel(be_ref, nu_ref, x_ref, wg_ref, wu_ref, wd_ref, o_ref):
    i = pl.program_id(0)
    j = pl.program_id(1)

    @pl.when(j == 0)
    def _():
        o_ref[...] = jnp.zeros_like(o_ref)

    @pl.when(i < nu_ref[0])
    def _():
        o_ref[...] += _swiglu_acc(x_ref[...], wg_ref, wu_ref, wd_ref)


def moe_ffn(block_e, n_used, buf, wg, wu, wd, *, tm, tc):
    p, d = buf.shape
    f = wg.shape[2]
    nj = f // tc

    def blk(i, nu):
        return jnp.minimum(i, nu[0] - 1)

    def col(i, j, nu):
        return jnp.where(i < nu[0], j, nj - 1)

    return pl.pallas_call(
        _moe_ffn_kernel,
        grid_spec=pltpu.PrefetchScalarGridSpec(
            num_scalar_prefetch=2,
            grid=(p // tm, nj),
            in_specs=[pl.BlockSpec((tm, d), lambda i, j, be, nu: (blk(i, nu), 0)),
                      pl.BlockSpec((None, d, tc), lambda i, j, be, nu: (be[blk(i, nu)], 0, col(i, j, nu))),
                      pl.BlockSpec((None, d, tc), lambda i, j, be, nu: (be[blk(i, nu)], 0, col(i, j, nu))),
                      pl.BlockSpec((None, tc, d), lambda i, j, be, nu: (be[blk(i, nu)], col(i, j, nu), 0))],
            out_specs=pl.BlockSpec((tm, d), lambda i, j, be, nu: (i, 0))),
        out_shape=jax.ShapeDtypeStruct((p, d), F32),
        compiler_params=_params("parallel", "arbitrary"),
        name="moe_ffn",
    )(block_e, n_used, buf, wg, wu, wd)


def _combine_kernel(dest_ref, x_ref, route_ref, g_ref, y_hbm, o_ref, ya, yb, sems):
    tm = o_ref.shape[0]
    base = pl.program_id(0) * tm

    def copies(r):
        t2 = 2 * (base + r)
        return (_row_copy(y_hbm, dest_ref[t2], ya, r, sems.at[0]),
                _row_copy(y_hbm, dest_ref[t2 + 1], yb, r, sems.at[1]))

    def issue(r, c):
        ca, cb = copies(r)
        ca.start()
        cb.start()
        return c

    lax.fori_loop(0, tm, issue, 0)

    def drain(r, c):
        ca, cb = copies(r)
        ca.wait()
        cb.wait()
        return c

    lax.fori_loop(0, tm, drain, 0)
    route = route_ref[...]
    x = x_ref[...] + (ya[...] * route[:, 2:3] + yb[...] * route[:, 3:4])
    o_ref[...] = _rms(x, g_ref[...])


def combine(dest, x, route, g, y_buf, *, tm):
    n, d = x.shape
    return pl.pallas_call(
        _combine_kernel,
        grid_spec=pltpu.PrefetchScalarGridSpec(
            num_scalar_prefetch=1,
            grid=(n // tm,),
            in_specs=[pl.BlockSpec((tm, d), lambda i, dst: (i, 0)),
                      pl.BlockSpec((tm, LANES), lambda i, dst: (i, 0)),
                      pl.BlockSpec((1, d), lambda i, dst: (0, 0)),
                      pl.BlockSpec(memory_space=pl.ANY)],
            out_specs=pl.BlockSpec((tm, d), lambda i, dst: (i, 0)),
            scratch_shapes=[pltpu.VMEM((tm, d), F32), pltpu.VMEM((tm, d), F32),
                            pltpu.SemaphoreType.DMA((2,))]),
        out_shape=jax.ShapeDtypeStruct((n, d), F32),
        compiler_params=_params("arbitrary"),
        name="moe_combine",
    )(dest, x, route, g, y_buf)


def _spread_rope(w):
    half = QK_ROPE // 2
    z = jnp.zeros(w.shape[:-1] + (half,), w.dtype)
    return jnp.concatenate([w[..., :half], z, w[..., half:], z], axis=-1)


def _tile(n, want):
    t = min(n, want)
    assert n % t == 0, (n, t)
    return t


def kernel(x, positions, a_norm_mix, rg_w_in, rg_conv_w, rg_conv_b, rg_w_a, rg_b_a, rg_w_x, rg_b_x, rg_lambda, rg_w_out, a_norm_ffn, ff_w_gate, ff_w_up, ff_w_down, b_norm_mix, mla_w_in, mla_q_norm, mla_w_q_up, mla_kv_norm, mla_w_kv_up, mla_w_out, b_norm_ffn, moe_w_router, moe_w_gate, moe_w_up, moe_w_down, final_norm):
    batch, seq, d = x.shape
    n = batch * seq
    assert a_norm_mix.shape[0] == 1 and b_norm_mix.shape[0] == 1
    xf = x.reshape(n, d)
    vec = lambda a: a.reshape(1, -1).astype(F32)

    u = norm_mm(xf, vec(a_norm_mix[0]), rg_w_in[0].astype(BF16),
                tm=_tile(n, 1024), tn=1024, out_dtype=F32)
    wax = jnp.concatenate([rg_w_a[0], rg_w_x[0]], axis=-1).astype(BF16)
    y = rglru_core(u, rg_conv_w[0], vec(rg_conv_b[0]), wax, vec(rg_b_a[0]), vec(rg_b_x[0]),
                   vec(rg_lambda[0]), batch=batch, seq=seq, t=_tile(seq, 256))
    x1 = mm_res(y, rg_w_out[0].astype(BF16), xf, tm=_tile(n, 1024), tn=1024)
    x2 = ffn_dense(x1, vec(a_norm_ffn[0]), ff_w_gate[0].astype(BF16), ff_w_up[0].astype(BF16),
                   ff_w_down[0].astype(BF16), tm=_tile(n, 512), tc=512)

    w_in = mla_w_in[0]
    w_in_p = jnp.concatenate([w_in[:, :Q_LORA + KV_LORA], _spread_rope(w_in[:, Q_LORA + KV_LORA:])],
                             axis=1).astype(BF16)
    wq = mla_w_q_up[0].reshape(Q_LORA, MLA_HEADS, QK_DIM)
    wq_p = jnp.concatenate([wq[..., :QK_NOPE].reshape(Q_LORA, -1),
                            _spread_rope(wq[..., QK_NOPE:]).reshape(Q_LORA, -1)], axis=1).astype(BF16)
    wkv = mla_w_kv_up[0].reshape(KV_LORA, MLA_HEADS, QK_NOPE + V_DIM)
    wkv_p = jnp.concatenate([wkv[..., :QK_NOPE].reshape(KV_LORA, -1),
                             wkv[..., QK_NOPE:].reshape(KV_LORA, -1)], axis=1).astype(BF16)
    inv_freq = 1.0 / (ROPE_THETA ** (jnp.arange(0, QK_ROPE, 2, dtype=F32) / QK_ROPE))
    freq = _spread_rope(jnp.concatenate([inv_freq, inv_freq])[None, :])
    ones = jnp.ones((1, QK_ROPE // 2), F32)
    sign = _spread_rope(jnp.concatenate([-ones, ones], axis=1))
    pos = positions.reshape(n, 1).astype(jnp.int32)
    q_nope, q_pe, k_nope, k_pe, v = mla_proj(
        x2, pos, vec(b_norm_mix[0]), w_in_p, vec(mla_q_norm[0]), wq_p, vec(mla_kv_norm[0]), wkv_p,
        freq, sign, tm=_tile(n, 512))
    o = attention(q_nope, q_pe, k_nope, k_pe, v, batch=batch, seq=seq, tq=_tile(seq, 512))
    w_router = jnp.pad(moe_w_router[0].astype(F32), ((0, 0), (0, LANES - N_EXPERTS)))
    x3, hn, route, cnt = out_router(o, mla_w_out[0].astype(BF16), x2, vec(b_norm_ffn[0]), w_router,
                                    tm=_tile(n, 512))

    tm = _tile(2 * n, 512)
    counts = cnt[0, :N_EXPERTS].astype(jnp.int32)
    padded = ((counts + tm - 1) // tm) * tm
    pad_end = jnp.cumsum(padded)
    pad_start = pad_end - padded
    experts = route[:, 0:2].astype(jnp.int32)
    ranks = route[:, 4:6].astype(jnp.int32)
    dest = (pad_start[experts] + ranks).reshape(-1)
    p = 2 * n + N_EXPERTS * tm
    n_blocks = p // tm
    src_tok = jnp.zeros((p,), jnp.int32).at[dest].set(jnp.arange(2 * n, dtype=jnp.int32) // 2)
    block_e = jnp.minimum(jnp.searchsorted(pad_end, jnp.arange(n_blocks, dtype=jnp.int32) * tm,
                                           side='right'), N_EXPERTS - 1).astype(jnp.int32)
    n_used = (pad_end[-1:] // tm).astype(jnp.int32)

    buf = dispatch(src_tok, hn, p=p, tm=tm)
    y_buf = moe_ffn(block_e, n_used, buf, moe_w_gate[0].astype(BF16), moe_w_up[0].astype(BF16),
                    moe_w_down[0].astype(BF16), tm=tm, tc=512)
    out = combine(dest, x3, route, vec(final_norm), y_buf, tm=_tile(n, 256))
    return out.reshape(batch, seq, d)
```

```python
import functools
import math

import jax
import jax.numpy as jnp
from jax import lax
from jax.experimental import pallas as pl
from jax.experimental.pallas import tpu as pltpu

EPS = 1e-6
RG_HEADS = 16
RG_BLOCK = 128
CONV_W = 4
RG_C = 8.0
MLA_HEADS = 16
Q_LORA = 512
KV_LORA = 512
QK_NOPE = 128
QK_ROPE = 64
V_DIM = 128
QK_DIM = QK_NOPE + QK_ROPE
ROPE_THETA = 10000.0
N_EXPERTS = 8
LANES = 128
SUBLANES = 8
VMEM_LIMIT_BYTES = 56 * 1024 * 1024

BF16 = jnp.bfloat16
F32 = jnp.float32


def _params(*sem):
    return pltpu.CompilerParams(dimension_semantics=sem, vmem_limit_bytes=VMEM_LIMIT_BYTES)


def _rms(x, g):
    return x * lax.rsqrt(jnp.mean(x * x, axis=-1, keepdims=True) + EPS) * g


def _dot(a, b):
    return jnp.dot(a, b, preferred_element_type=F32)


def _sigmoid(x):
    return 1.0 / (1.0 + jnp.exp(-x))


def _norm_mm_kernel(x_ref, g_ref, w_ref, o_ref, hn_ref):
    @pl.when(pl.program_id(1) == 0)
    def _():
        hn_ref[...] = _rms(x_ref[...], g_ref[...]).astype(BF16)

    o_ref[...] = _dot(hn_ref[...], w_ref[...]).astype(o_ref.dtype)


def norm_mm(x, g, w, *, tm, tn, out_dtype):
    n, d = x.shape
    f = w.shape[1]
    return pl.pallas_call(
        _norm_mm_kernel,
        grid=(n // tm, f // tn),
        in_specs=[pl.BlockSpec((tm, d), lambda i, j: (i, 0)),
                  pl.BlockSpec((1, d), lambda i, j: (0, 0)),
                  pl.BlockSpec((d, tn), lambda i, j: (0, j))],
        out_specs=pl.BlockSpec((tm, tn), lambda i, j: (i, j)),
        out_shape=jax.ShapeDtypeStruct((n, f), out_dtype),
        scratch_shapes=[pltpu.VMEM((tm, d), BF16)],
        compiler_params=_params("parallel", "arbitrary"),
        name="norm_mm",
    )(x, g, w)


def _mm_res_kernel(y_ref, w_ref, r_ref, o_ref):
    o_ref[...] = r_ref[...] + _dot(y_ref[...], w_ref[...])


def mm_res(y, w, res, *, tm, tn):
    n, k = y.shape
    f = w.shape[1]
    return pl.pallas_call(
        _mm_res_kernel,
        grid=(n // tm, f // tn),
        in_specs=[pl.BlockSpec((tm, k), lambda i, j: (i, 0)),
                  pl.BlockSpec((k, tn), lambda i, j: (0, j)),
                  pl.BlockSpec((tm, tn), lambda i, j: (i, j))],
        out_specs=pl.BlockSpec((tm, tn), lambda i, j: (i, j)),
        out_shape=jax.ShapeDtypeStruct((n, f), F32),
        compiler_params=_params("parallel", "parallel"),
        name="mm_res",
    )(y, w, res)


def _gelu_tanh(x):
    c = math.sqrt(2.0 / math.pi)
    return 0.5 * x * (1.0 + jnp.tanh(c * (x + 0.044715 * (x * x * x))))


def _softplus(z):
    return jnp.maximum(z, 0.0) + jnp.log1p(jnp.exp(-jnp.abs(z)))


def _rglru_kernel(gate_ref, rec_ref, cw_ref, cb_ref, wax_ref, ba_ref, bx_ref, lam_ref,
                  y_ref, xbuf, a_s, b_s, h_s):
    s = pl.program_id(1)
    t = rec_ref.shape[0]
    d = rec_ref.shape[1]

    @pl.when(s == 0)
    def _():
        xbuf[0:SUBLANES, :] = jnp.zeros((SUBLANES, d), F32)
        h_s[...] = jnp.zeros_like(h_s)

    @pl.when(s > 0)
    def _():
        xbuf[0:SUBLANES, :] = xbuf[t:t + SUBLANES, :]

    xbuf[SUBLANES:t + SUBLANES, :] = rec_ref[...]

    row = lax.broadcasted_iota(jnp.int32, (t, RG_BLOCK), 0) % SUBLANES
    for h in range(RG_HEADS):
        c0, c1 = h * RG_BLOCK, (h + 1) * RG_BLOCK
        xc = cb_ref[:, c0:c1] + jnp.zeros((t, RG_BLOCK), F32)
        for k in range(CONV_W):
            sh = CONV_W - 1 - k
            xc = xc + xbuf[SUBLANES - sh:SUBLANES - sh + t, c0:c1] * cw_ref[k:k + 1, c0:c1]
        gg = _dot(xc.astype(BF16), wax_ref[h])
        r = _sigmoid(gg[:, :RG_BLOCK] + ba_ref[:, c0:c1])
        i = _sigmoid(gg[:, RG_BLOCK:] + bx_ref[:, c0:c1])
        log_a = -RG_C * r * _softplus(-lam_ref[:, c0:c1])
        a = jnp.exp(log_a)
        b = jnp.sqrt(1.0 - a * a) * i * xc
        for sft in (1, 2, 4):
            a_sh = pltpu.roll(a, sft, axis=0)
            b_sh = pltpu.roll(b, sft, axis=0)
            m = row >= sft
            b = jnp.where(m, a * b_sh + b, b)
            a = jnp.where(m, a * a_sh, a)
        a_s[:, c0:c1] = a
        b_s[:, c0:c1] = b

    def group(gi, hc):
        r0 = pl.multiple_of(gi * SUBLANES, SUBLANES)
        rows = b_s[pl.ds(r0, SUBLANES), :] + a_s[pl.ds(r0, SUBLANES), :] * hc
        b_s[pl.ds(r0, SUBLANES), :] = rows
        return rows[SUBLANES - 1:SUBLANES, :]

    h_s[...] = lax.fori_loop(0, t // SUBLANES, group, h_s[...])
    y_ref[...] = (b_s[...] * _gelu_tanh(gate_ref[...])).astype(y_ref.dtype)


def rglru_core(u, conv_w, conv_b, wax, b_a, b_x, lam, *, batch, seq, t):
    n = u.shape[0]
    d = u.shape[1] // 2
    ns = seq // t
    vec = pl.BlockSpec((1, d), lambda b, s: (0, 0))
    return pl.pallas_call(
        _rglru_kernel,
        grid=(batch, ns),
        in_specs=[pl.BlockSpec((t, d), lambda b, s: (b * ns + s, 0)),
                  pl.BlockSpec((t, d), lambda b, s: (b * ns + s, 1)),
                  pl.BlockSpec((CONV_W, d), lambda b, s: (0, 0)),
                  vec,
                  pl.BlockSpec((RG_HEADS, RG_BLOCK, 2 * RG_BLOCK), lambda b, s: (0, 0, 0)),
                  vec, vec, vec],
        out_specs=pl.BlockSpec((t, d), lambda b, s: (b * ns + s, 0)),
        out_shape=jax.ShapeDtypeStruct((n, d), BF16),
        scratch_shapes=[pltpu.VMEM((t + SUBLANES, d), F32),
                        pltpu.VMEM((t, d), F32),
                        pltpu.VMEM((t, d), F32),
                        pltpu.VMEM((1, d), F32)],
        compiler_params=_params("arbitrary", "arbitrary"),
        name="rglru_core",
    )(u, u, conv_w, conv_b, wax, b_a, b_x, lam)


def _swiglu_acc(h, wg_ref, wu_ref, wd_ref):
    g = _dot(h, wg_ref[...])
    u = _dot(h, wu_ref[...])
    a = (g * _sigmoid(g) * u).astype(BF16)
    return _dot(a, wd_ref[...])


def _ffn_kernel(x_ref, g_ref, wg_ref, wu_ref, wd_ref, o_ref, hn_ref):
    @pl.when(pl.program_id(1) == 0)
    def _():
        x = x_ref[...]
        hn_ref[...] = _rms(x, g_ref[...]).astype(BF16)
        o_ref[...] = x

    o_ref[...] += _swiglu_acc(hn_ref[...], wg_ref, wu_ref, wd_ref)


def ffn_dense(x, g, wg, wu, wd, *, tm, tc):
    n, d = x.shape
    f = wg.shape[1]
    return pl.pallas_call(
        _ffn_kernel,
        grid=(n // tm, f // tc),
        in_specs=[pl.BlockSpec((tm, d), lambda i, j: (i, 0)),
                  pl.BlockSpec((1, d), lambda i, j: (0, 0)),
                  pl.BlockSpec((d, tc), lambda i, j: (0, j)),
                  pl.BlockSpec((d, tc), lambda i, j: (0, j)),
                  pl.BlockSpec((tc, d), lambda i, j: (j, 0))],
        out_specs=pl.BlockSpec((tm, d), lambda i, j: (i, 0)),
        out_shape=jax.ShapeDtypeStruct((n, d), F32),
        scratch_shapes=[pltpu.VMEM((tm, d), BF16)],
        compiler_params=_params("parallel", "arbitrary"),
        name="ffn_dense",
    )(x, g, wg, wu, wd)


def _mla_proj_kernel(x_ref, pos_ref, g_ref, win_ref, qn_ref, wq_ref, kvn_ref, wkv_ref,
                     freq_ref, sign_ref, qnope_ref, qpe_ref, knope_ref, kpe_ref, v_ref):
    d = x_ref.shape[1]
    hn = _rms(x_ref[...], g_ref[...]).astype(BF16)
    u = _dot(hn, win_ref[...])
    ang = pos_ref[...].astype(F32) * freq_ref[...]
    cos = jnp.cos(ang)
    sin = jnp.sin(ang) * sign_ref[...]

    def rope(xs):
        return xs * cos + pltpu.roll(xs, LANES // 2, axis=1) * sin

    kpe_ref[...] = rope(u[:, Q_LORA + KV_LORA:]).astype(BF16)
    q = _dot(_rms(u[:, :Q_LORA], qn_ref[...]).astype(BF16), wq_ref[...])
    qnope_ref[...] = q[:, :d].astype(BF16)
    for h in range(MLA_HEADS):
        c0, c1 = h * LANES, (h + 1) * LANES
        qpe_ref[:, c0:c1] = rope(q[:, d + c0:d + c1]).astype(BF16)
    kv = _dot(_rms(u[:, Q_LORA:Q_LORA + KV_LORA], kvn_ref[...]).astype(BF16), wkv_ref[...])
    knope_ref[...] = kv[:, :d].astype(BF16)
    v_ref[...] = kv[:, d:].astype(BF16)


def mla_proj(x, pos, g, w_in, q_norm, w_q, kv_norm, w_kv, freq, sign, *, tm):
    n, d = x.shape
    row = lambda w: pl.BlockSpec((tm, w), lambda i: (i, 0))
    full = lambda a: pl.BlockSpec(a.shape, lambda i: (0, 0))
    big = jax.ShapeDtypeStruct((n, d), BF16)
    return pl.pallas_call(
        _mla_proj_kernel,
        grid=(n // tm,),
        in_specs=[row(d), row(1), full(g), full(w_in), full(q_norm), full(w_q), full(kv_norm),
                  full(w_kv), full(freq), full(sign)],
        out_specs=[row(d), row(d), row(d), row(LANES), row(d)],
        out_shape=[big, big, big, jax.ShapeDtypeStruct((n, LANES), BF16), big],
        compiler_params=_params("parallel"),
        name="mla_proj",
    )(x, pos, g, w_in, q_norm, w_q, kv_norm, w_kv, freq, sign)


def _attn_kernel(qn_ref, qp_ref, kn_ref, kp_ref, v_ref, o_ref, m_s, acc_s):
    qi = pl.program_id(2)
    tq = qn_ref.shape[0]
    heads = qn_ref.shape[1] // LANES
    c = (QK_DIM ** -0.5) * math.log2(math.e)
    hs = lambda h: slice(h * LANES, (h + 1) * LANES)
    q = [jnp.concatenate([qn_ref[:, hs(h)], qp_ref[:, hs(h)]], axis=1) for h in range(heads)]
    ones = jnp.ones((tq, LANES), BF16)
    m_s[...] = jnp.full_like(m_s, -jnp.inf)
    acc_s[...] = jnp.zeros_like(acc_s)

    def chunk(kj, masked):
        k0 = pl.multiple_of(kj * tq, tq)
        kp = kp_ref[pl.ds(k0, tq), :]
        for h in range(heads):
            k = jnp.concatenate([kn_ref[pl.ds(k0, tq), hs(h)], kp], axis=1)
            s = lax.dot_general(q[h], k, (((1,), (1,)), ((), ())), preferred_element_type=F32)
            if masked:
                q_idx = lax.broadcasted_iota(jnp.int32, (tq, tq), 0)
                k_idx = lax.broadcasted_iota(jnp.int32, (tq, tq), 1)
                s = jnp.where(k_idx <= q_idx, s, -jnp.inf)
            slabs = [s[:, hs(j)] for j in range(tq // LANES)]
            part = functools.reduce(jnp.maximum, slabs)
            m_prev = m_s[h]
            m_new = jnp.maximum(m_prev, jnp.max(part, axis=1, keepdims=True))
            p = jnp.concatenate([jnp.exp2((sl - m_new) * c) for sl in slabs], axis=1)
            alpha = jnp.exp2((m_prev - m_new) * c)
            vv = jnp.concatenate([v_ref[pl.ds(k0, tq), hs(h)], ones], axis=1)
            acc = acc_s[h]
            acc = jnp.concatenate([acc[:, :LANES] * alpha, acc[:, LANES:] * alpha], axis=1)
            acc_s[h] = acc + _dot(p.astype(BF16), vv)
            m_s[h] = m_new

    def body(kj, carry):
        chunk(kj, False)
        return carry

    lax.fori_loop(0, qi, body, 0)
    chunk(qi, True)
    for h in range(heads):
        acc = acc_s[h]
        o_ref[:, hs(h)] = (acc[:, :LANES] / acc[:, LANES:]).astype(o_ref.dtype)


def attention(q_nope, q_pe, k_nope, k_pe, v, *, batch, seq, tq, heads):
    n, d = q_nope.shape
    nq = seq // tq
    w = heads * LANES
    qspec = pl.BlockSpec((tq, w), lambda b, h, i: (b * nq + i, h))
    kspec = pl.BlockSpec((seq, w), lambda b, h, i: (b, h))
    return pl.pallas_call(
        _attn_kernel,
        grid=(batch, MLA_HEADS // heads, nq),
        in_specs=[qspec, qspec, kspec,
                  pl.BlockSpec((seq, LANES), lambda b, h, i: (b, 0)),
                  kspec],
        out_specs=qspec,
        out_shape=jax.ShapeDtypeStruct((n, d), BF16),
        scratch_shapes=[pltpu.VMEM((heads, tq, LANES), F32),
                        pltpu.VMEM((heads, tq, 2 * LANES), F32)],
        compiler_params=_params("parallel", "parallel", "arbitrary"),
        name="attention",
    )(q_nope, q_pe, k_nope, k_pe, v)


def _out_router_kernel(o_ref, w_ref, r_ref, g_ref, wr_ref, x_ref, hn_ref, route_ref, cnt_ref, carry):
    i = pl.program_id(0)
    tm = o_ref.shape[0]

    @pl.when(i == 0)
    def _():
        carry[...] = jnp.zeros_like(carry)

    x = r_ref[...] + _dot(o_ref[...], w_ref[...])
    x_ref[...] = x
    hn = _rms(x, g_ref[...])
    hn_ref[...] = hn
    logits = jnp.dot(hn, wr_ref[...], preferred_element_type=F32, precision=lax.Precision.HIGHEST)
    lane = lax.broadcasted_iota(jnp.int32, (tm, LANES), 1)
    lane_f = lane.astype(F32)
    neg = -jnp.inf
    logits = jnp.where(lane < N_EXPERTS, logits, neg)
    m1 = jnp.max(logits, axis=1, keepdims=True)
    e1 = jnp.min(jnp.where(logits == m1, lane_f, float(LANES)), axis=1, keepdims=True)
    rest = jnp.where(lane_f == e1, neg, logits)
    m2 = jnp.max(rest, axis=1, keepdims=True)
    e2 = jnp.min(jnp.where(rest == m2, lane_f, float(LANES)), axis=1, keepdims=True)
    z = jnp.exp(m2 - m1)
    g1 = 1.0 / (1.0 + z)
    g2 = z / (1.0 + z)
    sel1 = lane_f == e1
    sel2 = lane_f == e2
    sel = jnp.where(sel1 | sel2, 1.0, 0.0)
    rr = lax.broadcasted_iota(jnp.int32, (tm, tm), 0)
    cc = lax.broadcasted_iota(jnp.int32, (tm, tm), 1)
    tril = jnp.where(cc < rr, 1.0, 0.0).astype(BF16)
    before = _dot(tril, sel.astype(BF16)) + carry[...]
    rank1 = jnp.sum(jnp.where(sel1, before, 0.0), axis=1, keepdims=True)
    rank2 = jnp.sum(jnp.where(sel2, before, 0.0), axis=1, keepdims=True)
    total = carry[...] + jnp.sum(sel, axis=0, keepdims=True)
    carry[...] = total
    cnt_ref[...] = jnp.broadcast_to(total, cnt_ref.shape)
    route = jnp.zeros((tm, LANES), F32)
    for col, val in enumerate((e1, e2, g1, g2, rank1, rank2)):
        route = jnp.where(lane == col, val, route)
    route_ref[...] = route


def out_router(o, w_out, res, g, w_router, *, tm):
    n, d = res.shape
    row = lambda w: pl.BlockSpec((tm, w), lambda i: (i, 0))
    full = lambda a: pl.BlockSpec(a.shape, lambda i: (0, 0))
    return pl.pallas_call(
        _out_router_kernel,
        grid=(n // tm,),
        in_specs=[row(d), full(w_out), row(d), full(g), full(w_router)],
        out_specs=[row(d), row(d), row(LANES), pl.BlockSpec((SUBLANES, LANES), lambda i: (0, 0))],
        out_shape=[jax.ShapeDtypeStruct((n, d), F32), jax.ShapeDtypeStruct((n, d), F32),
                   jax.ShapeDtypeStruct((n, LANES), F32),
                   jax.ShapeDtypeStruct((SUBLANES, LANES), F32)],
        scratch_shapes=[pltpu.VMEM((1, LANES), F32)],
        compiler_params=_params("arbitrary"),
        name="out_router",
    )(o, w_out, res, g, w_router)


GATHER_STEPS = 8


def _row_copy(src_hbm, row, dst, r, sem):
    return pltpu.make_async_copy(src_hbm.at[pl.ds(row, 1), :], dst.at[pl.ds(r, 1), :], sem)


def _moe_ffn_kernel(src_ref, be_ref, nu_ref, hn_hbm, wg_ref, wu_ref, wd_ref, o_ref, xg, xb, sems):
    i = pl.program_id(0)
    j = pl.program_id(1)
    tm = o_ref.shape[0]
    per_step = tm // GATHER_STEPS
    n_used = nu_ref[0]
    slot = i % 2

    def issue(blk, sl, r0, count):
        for u in range(count):
            r = r0 + u
            _row_copy(hn_hbm, src_ref[blk * tm + r], xg.at[sl], r, sems.at[sl]).start()

    @pl.when((i == 0) & (j == 0))
    def _():
        def first(g, c):
            issue(0, 0, g * per_step, per_step)
            return c
        lax.fori_loop(0, GATHER_STEPS, first, 0)

    @pl.when((j == 0) & (i < n_used))
    def _():
        def drain(r, c):
            _row_copy(hn_hbm, src_ref[i * tm + r], xg.at[slot], r, sems.at[slot]).wait()
            return c
        lax.fori_loop(0, tm, drain, 0)
        xb[...] = xg[slot].astype(BF16)

    @pl.when(j == 0)
    def _():
        o_ref[...] = jnp.zeros_like(o_ref)

    @pl.when((j >= 1) & (j <= GATHER_STEPS) & (i + 1 < n_used))
    def _():
        issue(i + 1, 1 - slot, (j - 1) * per_step, per_step)

    @pl.when(i < n_used)
    def _():
        o_ref[...] += _swiglu_acc(xb[...], wg_ref, wu_ref, wd_ref)


def moe_ffn(src_tok, block_e, n_used, hn, wg, wu, wd, *, tm, tc):
    d = hn.shape[1]
    p = src_tok.shape[0]
    f = wg.shape[2]
    nj = f // tc
    assert nj > GATHER_STEPS and tm % GATHER_STEPS == 0

    def blk(i, nu):
        return jnp.minimum(i, nu[0] - 1)

    def col(i, j, nu):
        return jnp.where(i < nu[0], j, nj - 1)

    return pl.pallas_call(
        _moe_ffn_kernel,
        grid_spec=pltpu.PrefetchScalarGridSpec(
            num_scalar_prefetch=3,
            grid=(p // tm, nj),
            in_specs=[pl.BlockSpec(memory_space=pl.ANY),
                      pl.BlockSpec((None, d, tc), lambda i, j, s, be, nu: (be[blk(i, nu)], 0, col(i, j, nu))),
                      pl.BlockSpec((None, d, tc), lambda i, j, s, be, nu: (be[blk(i, nu)], 0, col(i, j, nu))),
                      pl.BlockSpec((None, tc, d), lambda i, j, s, be, nu: (be[blk(i, nu)], col(i, j, nu), 0))],
            out_specs=pl.BlockSpec((tm, d), lambda i, j, s, be, nu: (i, 0)),
            scratch_shapes=[pltpu.VMEM((2, tm, d), F32), pltpu.VMEM((tm, d), BF16),
                            pltpu.SemaphoreType.DMA((2,))]),
        out_shape=jax.ShapeDtypeStruct((p, d), F32),
        compiler_params=_params("arbitrary", "arbitrary"),
        name="moe_ffn",
    )(src_tok, block_e, n_used, hn, wg, wu, wd)


def _combine_kernel(dest_ref, x_ref, route_ref, g_ref, y_hbm, o_ref, ya, yb, sems):
    i = pl.program_id(0)
    tm = o_ref.shape[0]
    slot = i % 2

    def copies(blk, sl, r):
        t2 = 2 * (blk * tm + r)
        return (_row_copy(y_hbm, dest_ref[t2], ya.at[sl], r, sems.at[0, sl]),
                _row_copy(y_hbm, dest_ref[t2 + 1], yb.at[sl], r, sems.at[1, sl]))

    def issue(blk, sl):
        def body(r, c):
            ca, cb = copies(blk, sl, r)
            ca.start()
            cb.start()
            return c
        lax.fori_loop(0, tm, body, 0, unroll=8)

    @pl.when(i == 0)
    def _():
        issue(0, 0)

    @pl.when(i + 1 < pl.num_programs(0))
    def _():
        issue(i + 1, 1 - slot)

    def drain(r, c):
        ca, cb = copies(i, slot, r)
        ca.wait()
        cb.wait()
        return c

    lax.fori_loop(0, tm, drain, 0, unroll=8)
    route = route_ref[...]
    x = x_ref[...] + (ya[slot] * route[:, 2:3] + yb[slot] * route[:, 3:4])
    o_ref[...] = _rms(x, g_ref[...])


def combine(dest, x, route, g, y_buf, *, tm):
    n, d = x.shape
    return pl.pallas_call(
        _combine_kernel,
        grid_spec=pltpu.PrefetchScalarGridSpec(
            num_scalar_prefetch=1,
            grid=(n // tm,),
            in_specs=[pl.BlockSpec((tm, d), lambda i, dst: (i, 0)),
                      pl.BlockSpec((tm, LANES), lambda i, dst: (i, 0)),
                      pl.BlockSpec((1, d), lambda i, dst: (0, 0)),
                      pl.BlockSpec(memory_space=pl.ANY)],
            out_specs=pl.BlockSpec((tm, d), lambda i, dst: (i, 0)),
            scratch_shapes=[pltpu.VMEM((2, tm, d), F32), pltpu.VMEM((2, tm, d), F32),
                            pltpu.SemaphoreType.DMA((2, 2))]),
        out_shape=jax.ShapeDtypeStruct((n, d), F32),
        compiler_params=_params("arbitrary"),
        name="moe_combine",
    )(dest, x, route, g, y_buf)


def _spread_rope(w):
    half = QK_ROPE // 2
    z = jnp.zeros(w.shape[:-1] + (half,), w.dtype)
    return jnp.concatenate([w[..., :half], z, w[..., half:], z], axis=-1)


def _tile(n, want):
    t = min(n, want)
    assert n % t == 0, (n, t)
    return t


def kernel(x, positions, a_norm_mix, rg_w_in, rg_conv_w, rg_conv_b, rg_w_a, rg_b_a, rg_w_x, rg_b_x, rg_lambda, rg_w_out, a_norm_ffn, ff_w_gate, ff_w_up, ff_w_down, b_norm_mix, mla_w_in, mla_q_norm, mla_w_q_up, mla_kv_norm, mla_w_kv_up, mla_w_out, b_norm_ffn, moe_w_router, moe_w_gate, moe_w_up, moe_w_down, final_norm):
    batch, seq, d = x.shape
    n = batch * seq
    assert a_norm_mix.shape[0] == 1 and b_norm_mix.shape[0] == 1
    xf = x.reshape(n, d)
    vec = lambda a: a.reshape(1, -1).astype(F32)

    u = norm_mm(xf, vec(a_norm_mix[0]), rg_w_in[0].astype(BF16),
                tm=_tile(n, 1024), tn=1024, out_dtype=F32)
    wax = jnp.concatenate([rg_w_a[0], rg_w_x[0]], axis=-1).astype(BF16)
    y = rglru_core(u, rg_conv_w[0], vec(rg_conv_b[0]), wax, vec(rg_b_a[0]), vec(rg_b_x[0]),
                   vec(rg_lambda[0]), batch=batch, seq=seq, t=_tile(seq, 256))
    x1 = mm_res(y, rg_w_out[0].astype(BF16), xf, tm=_tile(n, 1024), tn=1024)
    x2 = ffn_dense(x1, vec(a_norm_ffn[0]), ff_w_gate[0].astype(BF16), ff_w_up[0].astype(BF16),
                   ff_w_down[0].astype(BF16), tm=_tile(n, 512), tc=512)

    w_in = mla_w_in[0]
    w_in_p = jnp.concatenate([w_in[:, :Q_LORA + KV_LORA], _spread_rope(w_in[:, Q_LORA + KV_LORA:])],
                             axis=1).astype(BF16)
    wq = mla_w_q_up[0].reshape(Q_LORA, MLA_HEADS, QK_DIM)
    wq_p = jnp.concatenate([wq[..., :QK_NOPE].reshape(Q_LORA, -1),
                            _spread_rope(wq[..., QK_NOPE:]).reshape(Q_LORA, -1)], axis=1).astype(BF16)
    wkv = mla_w_kv_up[0].reshape(KV_LORA, MLA_HEADS, QK_NOPE + V_DIM)
    wkv_p = jnp.concatenate([wkv[..., :QK_NOPE].reshape(KV_LORA, -1),
                             wkv[..., QK_NOPE:].reshape(KV_LORA, -1)], axis=1).astype(BF16)
    inv_freq = 1.0 / (ROPE_THETA ** (jnp.arange(0, QK_ROPE, 2, dtype=F32) / QK_ROPE))
    freq = _spread_rope(jnp.concatenate([inv_freq, inv_freq])[None, :])
    ones = jnp.ones((1, QK_ROPE // 2), F32)
    sign = _spread_rope(jnp.concatenate([-ones, ones], axis=1))
    pos = positions.reshape(n, 1).astype(jnp.int32)
    q_nope, q_pe, k_nope, k_pe, v = mla_proj(
        x2, pos, vec(b_norm_mix[0]), w_in_p, vec(mla_q_norm[0]), wq_p, vec(mla_kv_norm[0]), wkv_p,
        freq, sign, tm=_tile(n, 512))
    o = attention(q_nope, q_pe, k_nope, k_pe, v, batch=batch, seq=seq, tq=_tile(seq, 512), heads=2)
    w_router = jnp.pad(moe_w_router[0].astype(F32), ((0, 0), (0, LANES - N_EXPERTS)))
    x3, hn, route, cnt = out_router(o, mla_w_out[0].astype(BF16), x2, vec(b_norm_ffn[0]), w_router,
                                    tm=_tile(n, 512))

    tm = _tile(2 * n, 512)
    counts = cnt[0, :N_EXPERTS].astype(jnp.int32)
    padded = ((counts + tm - 1) // tm) * tm
    pad_end = jnp.cumsum(padded)
    pad_start = pad_end - padded
    experts = route[:, 0:2].astype(jnp.int32)
    ranks = route[:, 4:6].astype(jnp.int32)
    dest = (pad_start[experts] + ranks).reshape(-1)
    p = 2 * n + N_EXPERTS * tm
    n_blocks = p // tm
    src_tok = jnp.zeros((p,), jnp.int32).at[dest].set(jnp.arange(2 * n, dtype=jnp.int32) // 2)
    block_start = jnp.arange(n_blocks, dtype=jnp.int32) * tm
    block_e = jnp.minimum(jnp.sum(block_start[:, None] >= pad_end[None, :], axis=1),
                          N_EXPERTS - 1).astype(jnp.int32)
    n_used = (pad_end[-1:] // tm).astype(jnp.int32)

    y_buf = moe_ffn(src_tok, block_e, n_used, hn, moe_w_gate[0].astype(BF16), moe_w_up[0].astype(BF16),
                    moe_w_down[0].astype(BF16), tm=tm, tc=512)
    out = combine(dest, x3, route, vec(final_norm), y_buf, tm=_tile(n, 256))
    return out.reshape(batch, seq, d)
```

```python
import functools
import math

import jax
import jax.numpy as jnp
from jax import lax
from jax.experimental import pallas as pl
from jax.experimental.pallas import tpu as pltpu

EPS = 1e-6
RG_HEADS = 16
RG_BLOCK = 128
CONV_W = 4
RG_C = 8.0
MLA_HEADS = 16
Q_LORA = 512
KV_LORA = 512
QK_NOPE = 128
QK_ROPE = 64
V_DIM = 128
QK_DIM = QK_NOPE + QK_ROPE
ROPE_THETA = 10000.0
N_EXPERTS = 8
LANES = 128
SUBLANES = 8
VMEM_LIMIT_BYTES = 56 * 1024 * 1024

BF16 = jnp.bfloat16
F32 = jnp.float32


def _params(*sem):
    return pltpu.CompilerParams(dimension_semantics=sem, vmem_limit_bytes=VMEM_LIMIT_BYTES)


def _rms(x, g):
    return x * lax.rsqrt(jnp.mean(x * x, axis=-1, keepdims=True) + EPS) * g


def _dot(a, b):
    return jnp.dot(a, b, preferred_element_type=F32)


def _sigmoid(x):
    return 1.0 / (1.0 + jnp.exp(-x))


def _norm_mm_kernel(x_ref, g_ref, w_ref, o_ref, hn_ref):
    @pl.when(pl.program_id(1) == 0)
    def _():
        hn_ref[...] = _rms(x_ref[...], g_ref[...]).astype(BF16)

    o_ref[...] = _dot(hn_ref[...], w_ref[...]).astype(o_ref.dtype)


def norm_mm(x, g, w, *, tm, tn, out_dtype):
    n, d = x.shape
    f = w.shape[1]
    return pl.pallas_call(
        _norm_mm_kernel,
        grid=(n // tm, f // tn),
        in_specs=[pl.BlockSpec((tm, d), lambda i, j: (i, 0)),
                  pl.BlockSpec((1, d), lambda i, j: (0, 0)),
                  pl.BlockSpec((d, tn), lambda i, j: (0, j))],
        out_specs=pl.BlockSpec((tm, tn), lambda i, j: (i, j)),
        out_shape=jax.ShapeDtypeStruct((n, f), out_dtype),
        scratch_shapes=[pltpu.VMEM((tm, d), BF16)],
        compiler_params=_params("parallel", "arbitrary"),
        name="norm_mm",
    )(x, g, w)


def _mm_res_kernel(y_ref, w_ref, r_ref, o_ref):
    o_ref[...] = r_ref[...] + _dot(y_ref[...], w_ref[...])


def mm_res(y, w, res, *, tm, tn):
    n, k = y.shape
    f = w.shape[1]
    return pl.pallas_call(
        _mm_res_kernel,
        grid=(n // tm, f // tn),
        in_specs=[pl.BlockSpec((tm, k), lambda i, j: (i, 0)),
                  pl.BlockSpec((k, tn), lambda i, j: (0, j)),
                  pl.BlockSpec((tm, tn), lambda i, j: (i, j))],
        out_specs=pl.BlockSpec((tm, tn), lambda i, j: (i, j)),
        out_shape=jax.ShapeDtypeStruct((n, f), F32),
        compiler_params=_params("parallel", "parallel"),
        name="mm_res",
    )(y, w, res)


def _gelu_tanh(x):
    c = math.sqrt(2.0 / math.pi)
    return 0.5 * x * (1.0 + jnp.tanh(c * (x + 0.044715 * (x * x * x))))


def _softplus(z):
    return jnp.maximum(z, 0.0) + jnp.log1p(jnp.exp(-jnp.abs(z)))


def _rglru_kernel(gate_ref, rec_ref, cw_ref, cb_ref, wax_ref, ba_ref, bx_ref, lam_ref,
                  y_ref, xbuf, a_s, b_s, h_s):
    s = pl.program_id(1)
    t = rec_ref.shape[0]
    d = rec_ref.shape[1]

    @pl.when(s == 0)
    def _():
        xbuf[0:SUBLANES, :] = jnp.zeros((SUBLANES, d), F32)
        h_s[...] = jnp.zeros_like(h_s)

    @pl.when(s > 0)
    def _():
        xbuf[0:SUBLANES, :] = xbuf[t:t + SUBLANES, :]

    xbuf[SUBLANES:t + SUBLANES, :] = rec_ref[...]

    row = lax.broadcasted_iota(jnp.int32, (t, RG_BLOCK), 0) % SUBLANES
    for h in range(RG_HEADS):
        c0, c1 = h * RG_BLOCK, (h + 1) * RG_BLOCK
        xc = cb_ref[:, c0:c1] + jnp.zeros((t, RG_BLOCK), F32)
        for k in range(CONV_W):
            sh = CONV_W - 1 - k
            xc = xc + xbuf[SUBLANES - sh:SUBLANES - sh + t, c0:c1] * cw_ref[k:k + 1, c0:c1]
        gg = _dot(xc.astype(BF16), wax_ref[h])
        r = _sigmoid(gg[:, :RG_BLOCK] + ba_ref[:, c0:c1])
        i = _sigmoid(gg[:, RG_BLOCK:] + bx_ref[:, c0:c1])
        log_a = -RG_C * r * _softplus(-lam_ref[:, c0:c1])
        a = jnp.exp(log_a)
        b = jnp.sqrt(1.0 - a * a) * i * xc
        for sft in (1, 2, 4):
            a_sh = pltpu.roll(a, sft, axis=0)
            b_sh = pltpu.roll(b, sft, axis=0)
            m = row >= sft
            b = jnp.where(m, a * b_sh + b, b)
            a = jnp.where(m, a * a_sh, a)
        a_s[:, c0:c1] = a
        b_s[:, c0:c1] = b

    def group(gi, hc):
        r0 = pl.multiple_of(gi * SUBLANES, SUBLANES)
        rows = b_s[pl.ds(r0, SUBLANES), :] + a_s[pl.ds(r0, SUBLANES), :] * hc
        b_s[pl.ds(r0, SUBLANES), :] = rows
        return rows[SUBLANES - 1:SUBLANES, :]

    h_s[...] = lax.fori_loop(0, t // SUBLANES, group, h_s[...])
    y_ref[...] = (b_s[...] * _gelu_tanh(gate_ref[...])).astype(y_ref.dtype)


def rglru_core(u, conv_w, conv_b, wax, b_a, b_x, lam, *, batch, seq, t):
    n = u.shape[0]
    d = u.shape[1] // 2
    ns = seq // t
    vec = pl.BlockSpec((1, d), lambda b, s: (0, 0))
    return pl.pallas_call(
        _rglru_kernel,
        grid=(batch, ns),
        in_specs=[pl.BlockSpec((t, d), lambda b, s: (b * ns + s, 0)),
                  pl.BlockSpec((t, d), lambda b, s: (b * ns + s, 1)),
                  pl.BlockSpec((CONV_W, d), lambda b, s: (0, 0)),
                  vec,
                  pl.BlockSpec((RG_HEADS, RG_BLOCK, 2 * RG_BLOCK), lambda b, s: (0, 0, 0)),
                  vec, vec, vec],
        out_specs=pl.BlockSpec((t, d), lambda b, s: (b * ns + s, 0)),
        out_shape=jax.ShapeDtypeStruct((n, d), BF16),
        scratch_shapes=[pltpu.VMEM((t + SUBLANES, d), F32),
                        pltpu.VMEM((t, d), F32),
                        pltpu.VMEM((t, d), F32),
                        pltpu.VMEM((1, d), F32)],
        compiler_params=_params("arbitrary", "arbitrary"),
        name="rglru_core",
    )(u, u, conv_w, conv_b, wax, b_a, b_x, lam)


def _swiglu_acc(h, wg_ref, wu_ref, wd_ref):
    g = _dot(h, wg_ref[...])
    u = _dot(h, wu_ref[...])
    a = (g * _sigmoid(g) * u).astype(BF16)
    return _dot(a, wd_ref[...])


def _ffn_kernel(x_ref, g_ref, wg_ref, wu_ref, wd_ref, o_ref, hn_ref):
    @pl.when(pl.program_id(1) == 0)
    def _():
        x = x_ref[...]
        hn_ref[...] = _rms(x, g_ref[...]).astype(BF16)
        o_ref[...] = x

    o_ref[...] += _swiglu_acc(hn_ref[...], wg_ref, wu_ref, wd_ref)


def ffn_dense(x, g, wg, wu, wd, *, tm, tc):
    n, d = x.shape
    f = wg.shape[1]
    return pl.pallas_call(
        _ffn_kernel,
        grid=(n // tm, f // tc),
        in_specs=[pl.BlockSpec((tm, d), lambda i, j: (i, 0)),
                  pl.BlockSpec((1, d), lambda i, j: (0, 0)),
                  pl.BlockSpec((d, tc), lambda i, j: (0, j)),
                  pl.BlockSpec((d, tc), lambda i, j: (0, j)),
                  pl.BlockSpec((tc, d), lambda i, j: (j, 0))],
        out_specs=pl.BlockSpec((tm, d), lambda i, j: (i, 0)),
        out_shape=jax.ShapeDtypeStruct((n, d), F32),
        scratch_shapes=[pltpu.VMEM((tm, d), BF16)],
        compiler_params=_params("parallel", "arbitrary"),
        name="ffn_dense",
    )(x, g, wg, wu, wd)


def _mla_proj_kernel(x_ref, pos_ref, g_ref, win_ref, qn_ref, wq_ref, kvn_ref, wkv_ref,
                     freq_ref, sign_ref, qnope_ref, qpe_ref, knope_ref, kpe_ref, v_ref):
    d = x_ref.shape[1]
    hn = _rms(x_ref[...], g_ref[...]).astype(BF16)
    u = _dot(hn, win_ref[...])
    ang = pos_ref[...].astype(F32) * freq_ref[...]
    cos = jnp.cos(ang)
    sin = jnp.sin(ang) * sign_ref[...]

    def rope(xs):
        return xs * cos + pltpu.roll(xs, LANES // 2, axis=1) * sin

    kpe_ref[...] = rope(u[:, Q_LORA + KV_LORA:]).astype(BF16)
    q = _dot(_rms(u[:, :Q_LORA], qn_ref[...]).astype(BF16), wq_ref[...])
    qnope_ref[...] = q[:, :d].astype(BF16)
    for h in range(MLA_HEADS):
        c0, c1 = h * LANES, (h + 1) * LANES
        qpe_ref[:, c0:c1] = rope(q[:, d + c0:d + c1]).astype(BF16)
    kv = _dot(_rms(u[:, Q_LORA:Q_LORA + KV_LORA], kvn_ref[...]).astype(BF16), wkv_ref[...])
    knope_ref[...] = kv[:, :d].astype(BF16)
    v_ref[...] = kv[:, d:].astype(BF16)


def mla_proj(x, pos, g, w_in, q_norm, w_q, kv_norm, w_kv, freq, sign, *, tm):
    n, d = x.shape
    row = lambda w: pl.BlockSpec((tm, w), lambda i: (i, 0))
    full = lambda a: pl.BlockSpec(a.shape, lambda i: (0, 0))
    big = jax.ShapeDtypeStruct((n, d), BF16)
    return pl.pallas_call(
        _mla_proj_kernel,
        grid=(n // tm,),
        in_specs=[row(d), row(1), full(g), full(w_in), full(q_norm), full(w_q), full(kv_norm),
                  full(w_kv), full(freq), full(sign)],
        out_specs=[row(d), row(d), row(d), row(LANES), row(d)],
        out_shape=[big, big, big, jax.ShapeDtypeStruct((n, LANES), BF16), big],
        compiler_params=_params("parallel"),
        name="mla_proj",
    )(x, pos, g, w_in, q_norm, w_q, kv_norm, w_kv, freq, sign)


def _attn_kernel(qn_ref, qp_ref, kn_ref, kp_ref, v_ref, o_ref, m_s, acc_s):
    qi = pl.program_id(2)
    tq = qn_ref.shape[0]
    heads = qn_ref.shape[1] // LANES
    c = (QK_DIM ** -0.5) * math.log2(math.e)
    hs = lambda h: slice(h * LANES, (h + 1) * LANES)
    q = [jnp.concatenate([qn_ref[:, hs(h)], qp_ref[:, hs(h)]], axis=1) for h in range(heads)]
    ones = jnp.ones((tq, LANES), BF16)
    m_s[...] = jnp.full_like(m_s, -jnp.inf)
    acc_s[...] = jnp.zeros_like(acc_s)

    def chunk(kj, masked):
        k0 = pl.multiple_of(kj * tq, tq)
        kp = kp_ref[pl.ds(k0, tq), :]
        for h in range(heads):
            k = jnp.concatenate([kn_ref[pl.ds(k0, tq), hs(h)], kp], axis=1)
            s = lax.dot_general(q[h], k, (((1,), (1,)), ((), ())), preferred_element_type=F32)
            if masked:
                q_idx = lax.broadcasted_iota(jnp.int32, (tq, tq), 0)
                k_idx = lax.broadcasted_iota(jnp.int32, (tq, tq), 1)
                s = jnp.where(k_idx <= q_idx, s, -jnp.inf)
            slabs = [s[:, hs(j)] for j in range(tq // LANES)]
            part = functools.reduce(jnp.maximum, slabs)
            m_prev = m_s[h]
            m_new = jnp.maximum(m_prev, jnp.max(part, axis=1, keepdims=True))
            p = jnp.concatenate([jnp.exp2((sl - m_new) * c) for sl in slabs], axis=1)
            alpha = jnp.exp2((m_prev - m_new) * c)
            vv = jnp.concatenate([v_ref[pl.ds(k0, tq), hs(h)], ones], axis=1)
            acc = acc_s[h]
            acc = jnp.concatenate([acc[:, :LANES] * alpha, acc[:, LANES:] * alpha], axis=1)
            acc_s[h] = acc + _dot(p.astype(BF16), vv)
            m_s[h] = m_new

    def body(kj, carry):
        chunk(kj, False)
        return carry

    lax.fori_loop(0, qi, body, 0)
    chunk(qi, True)
    for h in range(heads):
        acc = acc_s[h]
        o_ref[:, hs(h)] = (acc[:, :LANES] / acc[:, LANES:]).astype(o_ref.dtype)


def attention(q_nope, q_pe, k_nope, k_pe, v, *, batch, seq, tq, heads):
    n, d = q_nope.shape
    nq = seq // tq
    w = heads * LANES
    qspec = pl.BlockSpec((tq, w), lambda b, h, i: (b * nq + i, h))
    kspec = pl.BlockSpec((seq, w), lambda b, h, i: (b, h))
    return pl.pallas_call(
        _attn_kernel,
        grid=(batch, MLA_HEADS // heads, nq),
        in_specs=[qspec, qspec, kspec,
                  pl.BlockSpec((seq, LANES), lambda b, h, i: (b, 0)),
                  kspec],
        out_specs=qspec,
        out_shape=jax.ShapeDtypeStruct((n, d), BF16),
        scratch_shapes=[pltpu.VMEM((heads, tq, LANES), F32),
                        pltpu.VMEM((heads, tq, 2 * LANES), F32)],
        compiler_params=_params("parallel", "parallel", "arbitrary"),
        name="attention",
    )(q_nope, q_pe, k_nope, k_pe, v)


def _out_router_kernel(o_ref, w_ref, r_ref, g_ref, wr_ref, x_ref, hn_ref, route_ref, cnt_ref, carry):
    i = pl.program_id(0)
    tm = o_ref.shape[0]

    @pl.when(i == 0)
    def _():
        carry[...] = jnp.zeros_like(carry)

    x = r_ref[...] + _dot(o_ref[...], w_ref[...])
    x_ref[...] = x
    hn = _rms(x, g_ref[...])
    hn_ref[...] = hn
    hn_hi = hn.astype(BF16)
    hn_lo = (hn - hn_hi.astype(F32)).astype(BF16)
    parts = _dot(hn_hi, wr_ref[...]) + _dot(hn_lo, wr_ref[...])
    logits = parts[:, :LANES] + parts[:, LANES:]
    lane = lax.broadcasted_iota(jnp.int32, (tm, LANES), 1)
    lane_f = lane.astype(F32)
    neg = -jnp.inf
    logits = jnp.where(lane < N_EXPERTS, logits, neg)
    m1 = jnp.max(logits, axis=1, keepdims=True)
    e1 = jnp.min(jnp.where(logits == m1, lane_f, float(LANES)), axis=1, keepdims=True)
    rest = jnp.where(lane_f == e1, neg, logits)
    m2 = jnp.max(rest, axis=1, keepdims=True)
    e2 = jnp.min(jnp.where(rest == m2, lane_f, float(LANES)), axis=1, keepdims=True)
    z = jnp.exp(m2 - m1)
    g1 = 1.0 / (1.0 + z)
    g2 = z / (1.0 + z)
    sel1 = lane_f == e1
    sel2 = lane_f == e2
    sel = jnp.where(sel1 | sel2, 1.0, 0.0)
    rr = lax.broadcasted_iota(jnp.int32, (tm, tm), 0)
    cc = lax.broadcasted_iota(jnp.int32, (tm, tm), 1)
    tril = jnp.where(cc < rr, 1.0, 0.0).astype(BF16)
    before = _dot(tril, sel.astype(BF16)) + carry[...]
    rank1 = jnp.sum(jnp.where(sel1, before, 0.0), axis=1, keepdims=True)
    rank2 = jnp.sum(jnp.where(sel2, before, 0.0), axis=1, keepdims=True)
    total = carry[...] + jnp.sum(sel, axis=0, keepdims=True)
    carry[...] = total
    cnt_ref[...] = jnp.broadcast_to(total, cnt_ref.shape)
    route = jnp.zeros((tm, LANES), F32)
    for col, val in enumerate((e1, e2, g1, g2, rank1, rank2)):
        route = jnp.where(lane == col, val, route)
    route_ref[...] = route


def out_router(o, w_out, res, g, w_router, *, tm):
    n, d = res.shape
    row = lambda w: pl.BlockSpec((tm, w), lambda i: (i, 0))
    full = lambda a: pl.BlockSpec(a.shape, lambda i: (0, 0))
    return pl.pallas_call(
        _out_router_kernel,
        grid=(n // tm,),
        in_specs=[row(d), full(w_out), row(d), full(g), full(w_router)],
        out_specs=[row(d), row(d), row(LANES), pl.BlockSpec((SUBLANES, LANES), lambda i: (0, 0))],
        out_shape=[jax.ShapeDtypeStruct((n, d), F32), jax.ShapeDtypeStruct((n, d), F32),
                   jax.ShapeDtypeStruct((n, LANES), F32),
                   jax.ShapeDtypeStruct((SUBLANES, LANES), F32)],
        scratch_shapes=[pltpu.VMEM((1, LANES), F32)],
        compiler_params=_params("arbitrary"),
        name="out_router",
    )(o, w_out, res, g, w_router)


def _row_copy(src_hbm, row, dst, r, sem):
    return pltpu.make_async_copy(src_hbm.at[pl.ds(row, 1), :], dst.at[pl.ds(r, 1), :], sem)


def _moe_ffn_kernel(src_ref, be_ref, nu_ref, hn_hbm, wg_ref, wu_ref, wd_ref, o_ref, xg, xb, sems, *, per_step):
    i = pl.program_id(0)
    j = pl.program_id(1)
    tm = o_ref.shape[0]
    rows = xg.shape[1]
    n_used = nu_ref[0]
    slot = i % 2

    def start_row(blk, sl, r):
        tok = src_ref[blk * tm + jnp.minimum(r, tm - 1)]
        _row_copy(hn_hbm, tok, xg.at[sl], r, sems.at[sl]).start()

    @pl.when((i == 0) & (j == 0))
    def _():
        def first(r, c):
            start_row(0, 0, r)
            return c
        lax.fori_loop(0, rows, first, 0, unroll=8)

    @pl.when((j == 0) & (i <= n_used))
    def _():
        pltpu.make_async_copy(hn_hbm.at[pl.ds(0, rows), :], xg.at[slot], sems.at[slot]).wait()

    @pl.when((j == 0) & (i < n_used))
    def _():
        xb[...] = xg[slot, :tm, :].astype(BF16)

    @pl.when(j == 0)
    def _():
        o_ref[...] = jnp.zeros_like(o_ref)

    @pl.when(i < n_used)
    def _():
        nxt = jnp.minimum(i + 1, n_used - 1)
        for u in range(per_step):
            start_row(nxt, 1 - slot, j * per_step + u)
        o_ref[...] += _swiglu_acc(xb[...], wg_ref, wu_ref, wd_ref)


def moe_ffn(src_tok, block_e, n_used, hn, wg, wu, wd, *, tm, tc):
    d = hn.shape[1]
    p = src_tok.shape[0]
    f = wg.shape[2]
    nj = f // tc
    per_step = -(-tm // nj)
    while (per_step * nj) % SUBLANES:
        per_step += 1

    def blk(i, nu):
        return jnp.minimum(i, nu[0] - 1)

    def col(i, j, nu):
        return jnp.where(i < nu[0], j, nj - 1)

    return pl.pallas_call(
        functools.partial(_moe_ffn_kernel, per_step=per_step),
        grid_spec=pltpu.PrefetchScalarGridSpec(
            num_scalar_prefetch=3,
            grid=(p // tm, nj),
            in_specs=[pl.BlockSpec(memory_space=pl.ANY),
                      pl.BlockSpec((None, d, tc), lambda i, j, s, be, nu: (be[blk(i, nu)], 0, col(i, j, nu))),
                      pl.BlockSpec((None, d, tc), lambda i, j, s, be, nu: (be[blk(i, nu)], 0, col(i, j, nu))),
                      pl.BlockSpec((None, tc, d), lambda i, j, s, be, nu: (be[blk(i, nu)], col(i, j, nu), 0))],
            out_specs=pl.BlockSpec((tm, d), lambda i, j, s, be, nu: (i, 0)),
            scratch_shapes=[pltpu.VMEM((2, per_step * nj, d), F32), pltpu.VMEM((tm, d), BF16),
                            pltpu.SemaphoreType.DMA((2,))]),
        out_shape=jax.ShapeDtypeStruct((p, d), F32),
        compiler_params=_params("arbitrary", "arbitrary"),
        name="moe_ffn",
    )(src_tok, block_e, n_used, hn, wg, wu, wd)


def _combine_kernel(dest_ref, x_ref, route_ref, g_ref, y_hbm, o_ref, ya, yb, sems):
    i = pl.program_id(0)
    tm = o_ref.shape[0]
    slot = i % 2

    def copies(blk, sl, r):
        t2 = 2 * (blk * tm + r)
        return (_row_copy(y_hbm, dest_ref[t2], ya.at[sl], r, sems.at[0, sl]),
                _row_copy(y_hbm, dest_ref[t2 + 1], yb.at[sl], r, sems.at[1, sl]))

    def issue(blk, sl):
        def body(r, c):
            ca, cb = copies(blk, sl, r)
            ca.start()
            cb.start()
            return c
        lax.fori_loop(0, tm, body, 0, unroll=8)

    @pl.when(i == 0)
    def _():
        issue(0, 0)

    @pl.when(i + 1 < pl.num_programs(0))
    def _():
        issue(i + 1, 1 - slot)

    def drain(r, c):
        ca, cb = copies(i, slot, r)
        ca.wait()
        cb.wait()
        return c

    lax.fori_loop(0, tm, drain, 0, unroll=8)
    route = route_ref[...]
    x = x_ref[...] + (ya[slot] * route[:, 2:3] + yb[slot] * route[:, 3:4])
    o_ref[...] = _rms(x, g_ref[...])


def combine(dest, x, route, g, y_buf, *, tm):
    n, d = x.shape
    return pl.pallas_call(
        _combine_kernel,
        grid_spec=pltpu.PrefetchScalarGridSpec(
            num_scalar_prefetch=1,
            grid=(n // tm,),
            in_specs=[pl.BlockSpec((tm, d), lambda i, dst: (i, 0)),
                      pl.BlockSpec((tm, LANES), lambda i, dst: (i, 0)),
                      pl.BlockSpec((1, d), lambda i, dst: (0, 0)),
                      pl.BlockSpec(memory_space=pl.ANY)],
            out_specs=pl.BlockSpec((tm, d), lambda i, dst: (i, 0)),
            scratch_shapes=[pltpu.VMEM((2, tm, d), F32), pltpu.VMEM((2, tm, d), F32),
                            pltpu.SemaphoreType.DMA((2, 2))]),
        out_shape=jax.ShapeDtypeStruct((n, d), F32),
        compiler_params=_params("arbitrary"),
        name="moe_combine",
    )(dest, x, route, g, y_buf)


def _spread_rope(w):
    half = QK_ROPE // 2
    z = jnp.zeros(w.shape[:-1] + (half,), w.dtype)
    return jnp.concatenate([w[..., :half], z, w[..., half:], z], axis=-1)


def _tile(n, want):
    t = min(n, want)
    assert n % t == 0, (n, t)
    return t


def kernel(x, positions, a_norm_mix, rg_w_in, rg_conv_w, rg_conv_b, rg_w_a, rg_b_a, rg_w_x, rg_b_x, rg_lambda, rg_w_out, a_norm_ffn, ff_w_gate, ff_w_up, ff_w_down, b_norm_mix, mla_w_in, mla_q_norm, mla_w_q_up, mla_kv_norm, mla_w_kv_up, mla_w_out, b_norm_ffn, moe_w_router, moe_w_gate, moe_w_up, moe_w_down, final_norm):
    batch, seq, d = x.shape
    n = batch * seq
    assert a_norm_mix.shape[0] == 1 and b_norm_mix.shape[0] == 1
    xf = x.reshape(n, d)
    vec = lambda a: a.reshape(1, -1).astype(F32)

    u = norm_mm(xf, vec(a_norm_mix[0]), rg_w_in[0].astype(BF16),
                tm=_tile(n, 1024), tn=1024, out_dtype=F32)
    wax = jnp.concatenate([rg_w_a[0], rg_w_x[0]], axis=-1).astype(BF16)
    y = rglru_core(u, rg_conv_w[0], vec(rg_conv_b[0]), wax, vec(rg_b_a[0]), vec(rg_b_x[0]),
                   vec(rg_lambda[0]), batch=batch, seq=seq, t=_tile(seq, 256))
    x1 = mm_res(y, rg_w_out[0].astype(BF16), xf, tm=_tile(n, 1024), tn=1024)
    x2 = ffn_dense(x1, vec(a_norm_ffn[0]), ff_w_gate[0].astype(BF16), ff_w_up[0].astype(BF16),
                   ff_w_down[0].astype(BF16), tm=_tile(n, 512), tc=512)

    w_in = mla_w_in[0]
    w_in_p = jnp.concatenate([w_in[:, :Q_LORA + KV_LORA], _spread_rope(w_in[:, Q_LORA + KV_LORA:])],
                             axis=1).astype(BF16)
    wq = mla_w_q_up[0].reshape(Q_LORA, MLA_HEADS, QK_DIM)
    wq_p = jnp.concatenate([wq[..., :QK_NOPE].reshape(Q_LORA, -1),
                            _spread_rope(wq[..., QK_NOPE:]).reshape(Q_LORA, -1)], axis=1).astype(BF16)
    wkv = mla_w_kv_up[0].reshape(KV_LORA, MLA_HEADS, QK_NOPE + V_DIM)
    wkv_p = jnp.concatenate([wkv[..., :QK_NOPE].reshape(KV_LORA, -1),
                             wkv[..., QK_NOPE:].reshape(KV_LORA, -1)], axis=1).astype(BF16)
    inv_freq = 1.0 / (ROPE_THETA ** (jnp.arange(0, QK_ROPE, 2, dtype=F32) / QK_ROPE))
    freq = _spread_rope(jnp.concatenate([inv_freq, inv_freq])[None, :])
    ones = jnp.ones((1, QK_ROPE // 2), F32)
    sign = _spread_rope(jnp.concatenate([-ones, ones], axis=1))
    pos = positions.reshape(n, 1).astype(jnp.int32)
    q_nope, q_pe, k_nope, k_pe, v = mla_proj(
        x2, pos, vec(b_norm_mix[0]), w_in_p, vec(mla_q_norm[0]), wq_p, vec(mla_kv_norm[0]), wkv_p,
        freq, sign, tm=_tile(n, 512))
    o = attention(q_nope, q_pe, k_nope, k_pe, v, batch=batch, seq=seq, tq=_tile(seq, 512), heads=2)
    w_router = jnp.pad(moe_w_router[0].astype(F32), ((0, 0), (0, LANES - N_EXPERTS)))
    wr_hi = w_router.astype(BF16)
    w_router = jnp.concatenate([wr_hi, (w_router - wr_hi.astype(F32)).astype(BF16)], axis=1)
    x3, hn, route, cnt = out_router(o, mla_w_out[0].astype(BF16), x2, vec(b_norm_ffn[0]), w_router,
                                    tm=_tile(n, 512))

    tm = _tile(2 * n, 512)
    counts = cnt[0, :N_EXPERTS].astype(jnp.int32)
    padded = ((counts + tm - 1) // tm) * tm
    pad_end = jnp.cumsum(padded)
    pad_start = pad_end - padded
    experts = route[:, 0:2].astype(jnp.int32)
    ranks = route[:, 4:6].astype(jnp.int32)
    dest = (pad_start[experts] + ranks).reshape(-1)
    p = 2 * n + N_EXPERTS * tm
    n_blocks = p // tm
    src_tok = jnp.zeros((p,), jnp.int32).at[dest].set(jnp.arange(2 * n, dtype=jnp.int32) // 2)
    block_start = jnp.arange(n_blocks, dtype=jnp.int32) * tm
    block_e = jnp.minimum(jnp.sum(block_start[:, None] >= pad_end[None, :], axis=1),
                          N_EXPERTS - 1).astype(jnp.int32)
    n_used = (pad_end[-1:] // tm).astype(jnp.int32)

    y_buf = moe_ffn(src_tok, block_e, n_used, hn, moe_w_gate[0].astype(BF16), moe_w_up[0].astype(BF16),
                    moe_w_down[0].astype(BF16), tm=tm, tc=512)
    out = combine(dest, x3, route, vec(final_norm), y_buf, tm=_tile(n, 256))
    return out.reshape(batch, seq, d)
```

```python
import functools
import math

import jax
import jax.numpy as jnp
from jax import lax
from jax.experimental import pallas as pl
from jax.experimental.pallas import tpu as pltpu

EPS = 1e-6
RG_HEADS = 16
RG_BLOCK = 128
CONV_W = 4
RG_C = 8.0
MLA_HEADS = 16
Q_LORA = 512
KV_LORA = 512
QK_NOPE = 128
QK_ROPE = 64
V_DIM = 128
QK_DIM = QK_NOPE + QK_ROPE
ROPE_THETA = 10000.0
N_EXPERTS = 8
LANES = 128
SUBLANES = 8
VMEM_LIMIT_BYTES = 56 * 1024 * 1024
FF_TC = 512

BF16 = jnp.bfloat16
F32 = jnp.float32


def _params(*sem):
    return pltpu.CompilerParams(dimension_semantics=sem, vmem_limit_bytes=VMEM_LIMIT_BYTES)


def _rms(x, g):
    return x * lax.rsqrt(jnp.mean(x * x, axis=-1, keepdims=True) + EPS) * g


def _dot(a, b):
    return jnp.dot(a, b, preferred_element_type=F32)


def _sigmoid(x):
    return 1.0 / (1.0 + jnp.exp(-x))


def _norm_mm_kernel(x_ref, g_ref, w_ref, o_ref, hn_ref):
    @pl.when(pl.program_id(1) == 0)
    def _():
        hn_ref[...] = _rms(x_ref[...], g_ref[...]).astype(BF16)

    o_ref[...] = _dot(hn_ref[...], w_ref[...]).astype(o_ref.dtype)


def norm_mm(x, g, w, *, tm, tn, out_dtype):
    n, d = x.shape
    f = w.shape[1]
    return pl.pallas_call(
        _norm_mm_kernel,
        grid=(n // tm, f // tn),
        in_specs=[pl.BlockSpec((tm, d), lambda i, j: (i, 0)),
                  pl.BlockSpec((1, d), lambda i, j: (0, 0)),
                  pl.BlockSpec((d, tn), lambda i, j: (0, j))],
        out_specs=pl.BlockSpec((tm, tn), lambda i, j: (i, j)),
        out_shape=jax.ShapeDtypeStruct((n, f), out_dtype),
        scratch_shapes=[pltpu.VMEM((tm, d), BF16)],
        compiler_params=_params("parallel", "arbitrary"),
        name="norm_mm",
    )(x, g, w)


def _mm_res_kernel(y_ref, w_ref, r_ref, o_ref):
    o_ref[...] = r_ref[...] + _dot(y_ref[...], w_ref[...])


def mm_res(y, w, res, *, tm, tn):
    n, k = y.shape
    f = w.shape[1]
    return pl.pallas_call(
        _mm_res_kernel,
        grid=(n // tm, f // tn),
        in_specs=[pl.BlockSpec((tm, k), lambda i, j: (i, 0)),
                  pl.BlockSpec((k, tn), lambda i, j: (0, j)),
                  pl.BlockSpec((tm, tn), lambda i, j: (i, j))],
        out_specs=pl.BlockSpec((tm, tn), lambda i, j: (i, j)),
        out_shape=jax.ShapeDtypeStruct((n, f), F32),
        compiler_params=_params("parallel", "parallel"),
        name="mm_res",
    )(y, w, res)


def _gelu_tanh(x):
    c = math.sqrt(2.0 / math.pi)
    return 0.5 * x * (1.0 + jnp.tanh(c * (x + 0.044715 * (x * x * x))))


def _softplus(z):
    return jnp.maximum(z, 0.0) + jnp.log1p(jnp.exp(-jnp.abs(z)))


def _rglru_kernel(gate_ref, rec_ref, cw_ref, cb_ref, wax_ref, ba_ref, bx_ref, lam_ref,
                  y_ref, xbuf, a_s, b_s, h_s):
    s = pl.program_id(1)
    t = rec_ref.shape[0]
    d = rec_ref.shape[1]

    @pl.when(s == 0)
    def _():
        xbuf[0:SUBLANES, :] = jnp.zeros((SUBLANES, d), F32)
        h_s[...] = jnp.zeros_like(h_s)

    @pl.when(s > 0)
    def _():
        xbuf[0:SUBLANES, :] = xbuf[t:t + SUBLANES, :]

    xbuf[SUBLANES:t + SUBLANES, :] = rec_ref[...].astype(F32)

    row = lax.broadcasted_iota(jnp.int32, (t, RG_BLOCK), 0) % SUBLANES
    for h in range(RG_HEADS):
        c0, c1 = h * RG_BLOCK, (h + 1) * RG_BLOCK
        xc = cb_ref[:, c0:c1] + jnp.zeros((t, RG_BLOCK), F32)
        for k in range(CONV_W):
            sh = CONV_W - 1 - k
            xc = xc + xbuf[SUBLANES - sh:SUBLANES - sh + t, c0:c1] * cw_ref[k:k + 1, c0:c1]
        gg = _dot(xc.astype(BF16), wax_ref[h])
        r = _sigmoid(gg[:, :RG_BLOCK] + ba_ref[:, c0:c1])
        i = _sigmoid(gg[:, RG_BLOCK:] + bx_ref[:, c0:c1])
        log_a = -RG_C * r * _softplus(-lam_ref[:, c0:c1])
        a = jnp.exp(log_a)
        b = jnp.sqrt(1.0 - a * a) * i * xc
        for sft in (1, 2, 4):
            a_sh = pltpu.roll(a, sft, axis=0)
            b_sh = pltpu.roll(b, sft, axis=0)
            m = row >= sft
            b = jnp.where(m, a * b_sh + b, b)
            a = jnp.where(m, a * a_sh, a)
        a_s[:, c0:c1] = a
        b_s[:, c0:c1] = b

    def group(gi, hc):
        r0 = pl.multiple_of(gi * SUBLANES, SUBLANES)
        rows = b_s[pl.ds(r0, SUBLANES), :] + a_s[pl.ds(r0, SUBLANES), :] * hc
        b_s[pl.ds(r0, SUBLANES), :] = rows
        return rows[SUBLANES - 1:SUBLANES, :]

    h_s[...] = lax.fori_loop(0, t // SUBLANES, group, h_s[...])
    y_ref[...] = (b_s[...] * _gelu_tanh(gate_ref[...].astype(F32))).astype(y_ref.dtype)


def rglru_core(u, conv_w, conv_b, wax, b_a, b_x, lam, *, batch, seq, t):
    n = u.shape[0]
    d = u.shape[1] // 2
    ns = seq // t
    vec = pl.BlockSpec((1, d), lambda b, s: (0, 0))
    return pl.pallas_call(
        _rglru_kernel,
        grid=(batch, ns),
        in_specs=[pl.BlockSpec((t, d), lambda b, s: (b * ns + s, 0)),
                  pl.BlockSpec((t, d), lambda b, s: (b * ns + s, 1)),
                  pl.BlockSpec((CONV_W, d), lambda b, s: (0, 0)),
                  vec,
                  pl.BlockSpec((RG_HEADS, RG_BLOCK, 2 * RG_BLOCK), lambda b, s: (0, 0, 0)),
                  vec, vec, vec],
        out_specs=pl.BlockSpec((t, d), lambda b, s: (b * ns + s, 0)),
        out_shape=jax.ShapeDtypeStruct((n, d), BF16),
        scratch_shapes=[pltpu.VMEM((t + SUBLANES, d), F32),
                        pltpu.VMEM((t, d), F32),
                        pltpu.VMEM((t, d), F32),
                        pltpu.VMEM((1, d), F32)],
        compiler_params=_params("arbitrary", "arbitrary"),
        name="rglru_core",
    )(u, u, conv_w, conv_b, wax, b_a, b_x, lam)


def _swiglu_acc(h, wg_ref, wu_ref, wd_ref):
    g = _dot(h, wg_ref[...])
    u = _dot(h, wu_ref[...])
    a = (g * _sigmoid(g) * u).astype(BF16)
    return _dot(a, wd_ref[...])


def _ffn_kernel(x_ref, g_ref, wg_ref, wu_ref, wd_ref, o_ref, hn_ref):
    @pl.when(pl.program_id(1) == 0)
    def _():
        x = x_ref[...]
        hn_ref[...] = _rms(x, g_ref[...]).astype(BF16)
        o_ref[...] = x

    o_ref[...] += _swiglu_acc(hn_ref[...], wg_ref, wu_ref, wd_ref)


def ffn_dense(x, g, wg, wu, wd, *, tm):
    n, d = x.shape
    nj, _, tc = wg.shape
    f = nj * tc
    return pl.pallas_call(
        _ffn_kernel,
        grid=(n // tm, f // tc),
        in_specs=[pl.BlockSpec((tm, d), lambda i, j: (i, 0)),
                  pl.BlockSpec((1, d), lambda i, j: (0, 0)),
                  pl.BlockSpec((None, d, tc), lambda i, j: (j, 0, 0)),
                  pl.BlockSpec((None, d, tc), lambda i, j: (j, 0, 0)),
                  pl.BlockSpec((tc, d), lambda i, j: (j, 0))],
        out_specs=pl.BlockSpec((tm, d), lambda i, j: (i, 0)),
        out_shape=jax.ShapeDtypeStruct((n, d), F32),
        scratch_shapes=[pltpu.VMEM((tm, d), BF16)],
        compiler_params=_params("parallel", "arbitrary"),
        name="ffn_dense",
    )(x, g, wg, wu, wd)


def _mla_proj_kernel(x_ref, pos_ref, g_ref, win_ref, qn_ref, wq_ref, kvn_ref, wkv_ref,
                     freq_ref, sign_ref, qnope_ref, qpe_ref, knope_ref, kpe_ref, v_ref):
    d = x_ref.shape[1]
    hn = _rms(x_ref[...], g_ref[...]).astype(BF16)
    u = _dot(hn, win_ref[...])
    ang = pos_ref[...].astype(F32) * freq_ref[...]
    cos = jnp.cos(ang)
    sin = jnp.sin(ang) * sign_ref[...]

    def rope(xs):
        return xs * cos + pltpu.roll(xs, LANES // 2, axis=1) * sin

    kpe_ref[...] = rope(u[:, Q_LORA + KV_LORA:]).astype(BF16)
    q = _dot(_rms(u[:, :Q_LORA], qn_ref[...]).astype(BF16), wq_ref[...])
    qnope_ref[...] = q[:, :d].astype(BF16)
    for h in range(MLA_HEADS):
        c0, c1 = h * LANES, (h + 1) * LANES
        qpe_ref[:, c0:c1] = rope(q[:, d + c0:d + c1]).astype(BF16)
    kv = _dot(_rms(u[:, Q_LORA:Q_LORA + KV_LORA], kvn_ref[...]).astype(BF16), wkv_ref[...])
    knope_ref[...] = kv[:, :d].astype(BF16)
    v_ref[...] = kv[:, d:].astype(BF16)


def mla_proj(x, pos, g, w_in, q_norm, w_q, kv_norm, w_kv, freq, sign, *, tm):
    n, d = x.shape
    row = lambda w: pl.BlockSpec((tm, w), lambda i: (i, 0))
    full = lambda a: pl.BlockSpec(a.shape, lambda i: (0, 0))
    big = jax.ShapeDtypeStruct((n, d), BF16)
    return pl.pallas_call(
        _mla_proj_kernel,
        grid=(n // tm,),
        in_specs=[row(d), row(1), full(g), full(w_in), full(q_norm), full(w_q), full(kv_norm),
                  full(w_kv), full(freq), full(sign)],
        out_specs=[row(d), row(d), row(d), row(LANES), row(d)],
        out_shape=[big, big, big, jax.ShapeDtypeStruct((n, LANES), BF16), big],
        compiler_params=_params("parallel"),
        name="mla_proj",
    )(x, pos, g, w_in, q_norm, w_q, kv_norm, w_kv, freq, sign)


def _attn_kernel(qn_ref, qp_ref, kn_ref, kp_ref, v_ref, o_ref, m_s, acc_s, bias_s):
    qi = pl.program_id(2)
    tq = qn_ref.shape[0]
    heads = qn_ref.shape[1] // LANES
    c = (QK_DIM ** -0.5) * math.log2(math.e)
    hs = lambda h: slice(h * LANES, (h + 1) * LANES)
    q = [jnp.concatenate([qn_ref[:, hs(h)], qp_ref[:, hs(h)]], axis=1) for h in range(heads)]
    ones = jnp.ones((tq, LANES), BF16)
    m_s[...] = jnp.full_like(m_s, -jnp.inf)
    acc_s[...] = jnp.zeros_like(acc_s)
    q_idx = lax.broadcasted_iota(jnp.int32, (tq, tq), 0)
    k_idx = lax.broadcasted_iota(jnp.int32, (tq, tq), 1)
    bias_s[...] = jnp.where(k_idx <= q_idx, 0.0, -jnp.inf)

    def chunk(kj, masked):
        k0 = pl.multiple_of(kj * tq, tq)
        kp = kp_ref[pl.ds(k0, tq), :]
        for h in range(heads):
            k = jnp.concatenate([kn_ref[pl.ds(k0, tq), hs(h)], kp], axis=1)
            s = lax.dot_general(q[h], k, (((1,), (1,)), ((), ())), preferred_element_type=F32)
            if masked:
                s = s + bias_s[...]
            slabs = [s[:, hs(j)] for j in range(tq // LANES)]
            part = functools.reduce(jnp.maximum, slabs)
            m_prev = m_s[h]
            m_new = jnp.maximum(m_prev, jnp.max(part, axis=1, keepdims=True))
            p = jnp.concatenate([jnp.exp2((sl - m_new) * c) for sl in slabs], axis=1)
            alpha = jnp.exp2((m_prev - m_new) * c)
            vv = jnp.concatenate([v_ref[pl.ds(k0, tq), hs(h)], ones], axis=1)
            acc = acc_s[h]
            acc = jnp.concatenate([acc[:, :LANES] * alpha, acc[:, LANES:] * alpha], axis=1)
            acc_s[h] = acc + _dot(p.astype(BF16), vv)
            m_s[h] = m_new

    def body(kj, carry):
        chunk(kj, False)
        return carry

    lax.fori_loop(0, qi, body, 0)
    chunk(qi, True)
    for h in range(heads):
        acc = acc_s[h]
        o_ref[:, hs(h)] = (acc[:, :LANES] / acc[:, LANES:]).astype(o_ref.dtype)


def attention(q_nope, q_pe, k_nope, k_pe, v, *, batch, seq, tq, heads):
    n, d = q_nope.shape
    nq = seq // tq
    w = heads * LANES
    qspec = pl.BlockSpec((tq, w), lambda b, h, i: (b * nq + i, h))
    kspec = pl.BlockSpec((seq, w), lambda b, h, i: (b, h))
    return pl.pallas_call(
        _attn_kernel,
        grid=(batch, MLA_HEADS // heads, nq),
        in_specs=[qspec, qspec, kspec,
                  pl.BlockSpec((seq, LANES), lambda b, h, i: (b, 0)),
                  kspec],
        out_specs=qspec,
        out_shape=jax.ShapeDtypeStruct((n, d), BF16),
        scratch_shapes=[pltpu.VMEM((heads, tq, LANES), F32),
                        pltpu.VMEM((heads, tq, 2 * LANES), F32),
                        pltpu.VMEM((tq, tq), F32)],
        compiler_params=_params("parallel", "parallel", "arbitrary"),
        name="attention",
    )(q_nope, q_pe, k_nope, k_pe, v)


def _out_router_kernel(o_ref, w_ref, r_ref, g_ref, wr_ref, x_ref, hn_ref, route_ref, cnt_ref, carry):
    i = pl.program_id(0)
    tm = o_ref.shape[0]

    @pl.when(i == 0)
    def _():
        carry[...] = jnp.zeros_like(carry)

    x = r_ref[...] + _dot(o_ref[...], w_ref[...])
    x_ref[...] = x
    hn = _rms(x, g_ref[...])
    hn_ref[...] = hn
    hn_hi = hn.astype(BF16)
    hn_lo = (hn - hn_hi.astype(F32)).astype(BF16)
    parts = _dot(hn_hi, wr_ref[...]) + _dot(hn_lo, wr_ref[...])
    logits = parts[:, :LANES] + parts[:, LANES:]
    lane = lax.broadcasted_iota(jnp.int32, (tm, LANES), 1)
    lane_f = lane.astype(F32)
    neg = -jnp.inf
    logits = jnp.where(lane < N_EXPERTS, logits, neg)
    m1 = jnp.max(logits, axis=1, keepdims=True)
    e1 = jnp.min(jnp.where(logits == m1, lane_f, float(LANES)), axis=1, keepdims=True)
    rest = jnp.where(lane_f == e1, neg, logits)
    m2 = jnp.max(rest, axis=1, keepdims=True)
    e2 = jnp.min(jnp.where(rest == m2, lane_f, float(LANES)), axis=1, keepdims=True)
    z = jnp.exp(m2 - m1)
    g1 = 1.0 / (1.0 + z)
    g2 = z / (1.0 + z)
    sel1 = lane_f == e1
    sel2 = lane_f == e2
    sel = jnp.where(sel1 | sel2, 1.0, 0.0)
    rr = lax.broadcasted_iota(jnp.int32, (tm, tm), 0)
    cc = lax.broadcasted_iota(jnp.int32, (tm, tm), 1)
    tril = jnp.where(cc < rr, 1.0, 0.0).astype(BF16)
    before = _dot(tril, sel.astype(BF16)) + carry[...]
    rank1 = jnp.sum(jnp.where(sel1, before, 0.0), axis=1, keepdims=True)
    rank2 = jnp.sum(jnp.where(sel2, before, 0.0), axis=1, keepdims=True)
    total = carry[...] + jnp.sum(sel, axis=0, keepdims=True)
    carry[...] = total
    cnt_ref[...] = jnp.broadcast_to(total, cnt_ref.shape)
    route = jnp.zeros((tm, LANES), F32)
    for col, val in enumerate((e1, e2, g1, g2, rank1, rank2)):
        route = jnp.where(lane == col, val, route)
    route_ref[...] = route


def out_router(o, w_out, res, g, w_router, *, tm):
    n, d = res.shape
    row = lambda w: pl.BlockSpec((tm, w), lambda i: (i, 0))
    full = lambda a: pl.BlockSpec(a.shape, lambda i: (0, 0))
    return pl.pallas_call(
        _out_router_kernel,
        grid=(n // tm,),
        in_specs=[row(d), full(w_out), row(d), full(g), full(w_router)],
        out_specs=[row(d), row(d), row(LANES), pl.BlockSpec((SUBLANES, LANES), lambda i: (0, 0))],
        out_shape=[jax.ShapeDtypeStruct((n, d), F32), jax.ShapeDtypeStruct((n, d), F32),
                   jax.ShapeDtypeStruct((n, LANES), F32),
                   jax.ShapeDtypeStruct((SUBLANES, LANES), F32)],
        scratch_shapes=[pltpu.VMEM((1, LANES), F32)],
        compiler_params=_params("arbitrary"),
        name="out_router",
    )(o, w_out, res, g, w_router)


GATHER_SLOTS = 3


def _row_copy(src_hbm, row, dst, r, sem):
    return pltpu.make_async_copy(src_hbm.at[pl.ds(row, 1), :], dst.at[pl.ds(r, 1), :], sem)


def _moe_ffn_kernel(src_ref, be_ref, nu_ref, hn_hbm, wg_ref, wu_ref, wd_ref, o_ref, xg, xb, sems, *, per_step):
    i = pl.program_id(0)
    j = pl.program_id(1)
    tm = o_ref.shape[0]
    rows = xg.shape[1]
    n_used = nu_ref[0]
    slot = i % GATHER_SLOTS

    def start_row(blk, sl, r):
        tok = src_ref[blk * tm + jnp.minimum(r, tm - 1)]
        _row_copy(hn_hbm, tok, xg.at[sl], r, sems.at[sl]).start()

    @pl.when((i == 0) & (j == 0))
    def _():
        def first(r, c):
            start_row(0, 0, r)
            start_row(1, 1, r)
            return c
        lax.fori_loop(0, rows, first, 0, unroll=8)

    @pl.when((j == 0) & (i <= n_used + 1))
    def _():
        pltpu.make_async_copy(hn_hbm.at[pl.ds(0, rows), :], xg.at[slot], sems.at[slot]).wait()

    @pl.when((j == 0) & (i < n_used))
    def _():
        xb[...] = xg[slot, :tm, :].astype(BF16)

    @pl.when(j == 0)
    def _():
        o_ref[...] = jnp.zeros_like(o_ref)

    @pl.when(i < n_used)
    def _():
        nxt = jnp.minimum(i + 2, n_used - 1)
        for u in range(per_step):
            start_row(nxt, (i + 2) % GATHER_SLOTS, j * per_step + u)
        o_ref[...] += _swiglu_acc(xb[...], wg_ref, wu_ref, wd_ref)


def moe_ffn(src_tok, block_e, n_used, hn, wg, wu, wd, *, tm):
    d = hn.shape[1]
    p = src_tok.shape[0]
    _, nj, _, tc = wg.shape
    per_step = -(-tm // nj)
    while (per_step * nj) % SUBLANES:
        per_step += 1

    def blk(i, nu):
        return jnp.minimum(i, nu[0] - 1)

    def col(i, j, nu):
        return jnp.where(i < nu[0], j, nj - 1)

    return pl.pallas_call(
        functools.partial(_moe_ffn_kernel, per_step=per_step),
        grid_spec=pltpu.PrefetchScalarGridSpec(
            num_scalar_prefetch=3,
            grid=(p // tm, nj),
            in_specs=[pl.BlockSpec(memory_space=pl.ANY),
                      pl.BlockSpec((None, None, d, tc), lambda i, j, s, be, nu: (be[blk(i, nu)], col(i, j, nu), 0, 0)),
                      pl.BlockSpec((None, None, d, tc), lambda i, j, s, be, nu: (be[blk(i, nu)], col(i, j, nu), 0, 0)),
                      pl.BlockSpec((None, tc, d), lambda i, j, s, be, nu: (be[blk(i, nu)], col(i, j, nu), 0))],
            out_specs=pl.BlockSpec((tm, d), lambda i, j, s, be, nu: (i, 0)),
            scratch_shapes=[pltpu.VMEM((GATHER_SLOTS, per_step * nj, d), F32), pltpu.VMEM((tm, d), BF16),
                            pltpu.SemaphoreType.DMA((GATHER_SLOTS,))]),
        out_shape=jax.ShapeDtypeStruct((p, d), F32),
        compiler_params=_params("arbitrary", "arbitrary"),
        name="moe_ffn",
    )(src_tok, block_e, n_used, hn, wg, wu, wd)


def _combine_kernel(dest_ref, x_ref, route_ref, g_ref, y_hbm, o_ref, ya, yb, sems):
    i = pl.program_id(0)
    tm = o_ref.shape[0]
    slot = i % 2

    def copies(blk, sl, r):
        t2 = 2 * (blk * tm + r)
        return (_row_copy(y_hbm, dest_ref[t2], ya.at[sl], r, sems.at[0, sl]),
                _row_copy(y_hbm, dest_ref[t2 + 1], yb.at[sl], r, sems.at[1, sl]))

    def issue(blk, sl):
        def body(r, c):
            ca, cb = copies(blk, sl, r)
            ca.start()
            cb.start()
            return c
        lax.fori_loop(0, tm, body, 0, unroll=8)

    @pl.when(i == 0)
    def _():
        issue(0, 0)

    @pl.when(i + 1 < pl.num_programs(0))
    def _():
        issue(i + 1, 1 - slot)

    pltpu.make_async_copy(y_hbm.at[pl.ds(0, tm), :], ya.at[slot], sems.at[0, slot]).wait()
    pltpu.make_async_copy(y_hbm.at[pl.ds(0, tm), :], yb.at[slot], sems.at[1, slot]).wait()
    route = route_ref[...]
    x = x_ref[...] + (ya[slot] * route[:, 2:3] + yb[slot] * route[:, 3:4])
    o_ref[...] = _rms(x, g_ref[...])


def combine(dest, x, route, g, y_buf, *, tm):
    n, d = x.shape
    return pl.pallas_call(
        _combine_kernel,
        grid_spec=pltpu.PrefetchScalarGridSpec(
            num_scalar_prefetch=1,
            grid=(n // tm,),
            in_specs=[pl.BlockSpec((tm, d), lambda i, dst: (i, 0)),
                      pl.BlockSpec((tm, LANES), lambda i, dst: (i, 0)),
                      pl.BlockSpec((1, d), lambda i, dst: (0, 0)),
                      pl.BlockSpec(memory_space=pl.ANY)],
            out_specs=pl.BlockSpec((tm, d), lambda i, dst: (i, 0)),
            scratch_shapes=[pltpu.VMEM((2, tm, d), F32), pltpu.VMEM((2, tm, d), F32),
                            pltpu.SemaphoreType.DMA((2, 2))]),
        out_shape=jax.ShapeDtypeStruct((n, d), F32),
        compiler_params=_params("arbitrary"),
        name="moe_combine",
    )(dest, x, route, g, y_buf)


def _spread_rope(w):
    half = QK_ROPE // 2
    z = jnp.zeros(w.shape[:-1] + (half,), w.dtype)
    return jnp.concatenate([w[..., :half], z, w[..., half:], z], axis=-1)


def _col_blocks(w, tc):
    *lead, d, f = w.shape
    w = w.reshape(*lead, d, f // tc, tc)
    return jnp.swapaxes(w, -3, -2).astype(BF16)


def _tile(n, want):
    t = min(n, want)
    assert n % t == 0, (n, t)
    return t


def kernel(x, positions, a_norm_mix, rg_w_in, rg_conv_w, rg_conv_b, rg_w_a, rg_b_a, rg_w_x, rg_b_x, rg_lambda, rg_w_out, a_norm_ffn, ff_w_gate, ff_w_up, ff_w_down, b_norm_mix, mla_w_in, mla_q_norm, mla_w_q_up, mla_kv_norm, mla_w_kv_up, mla_w_out, b_norm_ffn, moe_w_router, moe_w_gate, moe_w_up, moe_w_down, final_norm):
    batch, seq, d = x.shape
    n = batch * seq
    assert a_norm_mix.shape[0] == 1 and b_norm_mix.shape[0] == 1
    xf = x.reshape(n, d)
    vec = lambda a: a.reshape(1, -1).astype(F32)

    u = norm_mm(xf, vec(a_norm_mix[0]), rg_w_in[0].astype(BF16),
                tm=_tile(n, 1024), tn=1024, out_dtype=BF16)
    wax = jnp.concatenate([rg_w_a[0], rg_w_x[0]], axis=-1).astype(BF16)
    y = rglru_core(u, rg_conv_w[0], vec(rg_conv_b[0]), wax, vec(rg_b_a[0]), vec(rg_b_x[0]),
                   vec(rg_lambda[0]), batch=batch, seq=seq, t=_tile(seq, 256))
    x1 = mm_res(y, rg_w_out[0].astype(BF16), xf, tm=_tile(n, 1024), tn=1024)
    x2 = ffn_dense(x1, vec(a_norm_ffn[0]), _col_blocks(ff_w_gate[0], FF_TC), _col_blocks(ff_w_up[0], FF_TC),
                   ff_w_down[0].astype(BF16), tm=_tile(n, 512))

    w_in = mla_w_in[0]
    w_in_p = jnp.concatenate([w_in[:, :Q_LORA + KV_LORA], _spread_rope(w_in[:, Q_LORA + KV_LORA:])],
                             axis=1).astype(BF16)
    wq = mla_w_q_up[0].reshape(Q_LORA, MLA_HEADS, QK_DIM)
    wq_p = jnp.concatenate([wq[..., :QK_NOPE].reshape(Q_LORA, -1),
                            _spread_rope(wq[..., QK_NOPE:]).reshape(Q_LORA, -1)], axis=1).astype(BF16)
    wkv = mla_w_kv_up[0].reshape(KV_LORA, MLA_HEADS, QK_NOPE + V_DIM)
    wkv_p = jnp.concatenate([wkv[..., :QK_NOPE].reshape(KV_LORA, -1),
                             wkv[..., QK_NOPE:].reshape(KV_LORA, -1)], axis=1).astype(BF16)
    inv_freq = 1.0 / (ROPE_THETA ** (jnp.arange(0, QK_ROPE, 2, dtype=F32) / QK_ROPE))
    freq = _spread_rope(jnp.concatenate([inv_freq, inv_freq])[None, :])
    ones = jnp.ones((1, QK_ROPE // 2), F32)
    sign = _spread_rope(jnp.concatenate([-ones, ones], axis=1))
    pos = positions.reshape(n, 1).astype(jnp.int32)
    q_nope, q_pe, k_nope, k_pe, v = mla_proj(
        x2, pos, vec(b_norm_mix[0]), w_in_p, vec(mla_q_norm[0]), wq_p, vec(mla_kv_norm[0]), wkv_p,
        freq, sign, tm=_tile(n, 512))
    o = attention(q_nope, q_pe, k_nope, k_pe, v, batch=batch, seq=seq, tq=_tile(seq, 512), heads=8)
    w_router = jnp.pad(moe_w_router[0].astype(F32), ((0, 0), (0, LANES - N_EXPERTS)))
    wr_hi = w_router.astype(BF16)
    w_router = jnp.concatenate([wr_hi, (w_router - wr_hi.astype(F32)).astype(BF16)], axis=1)
    x3, hn, route, cnt = out_router(o, mla_w_out[0].astype(BF16), x2, vec(b_norm_ffn[0]), w_router,
                                    tm=_tile(n, 512))

    tm = _tile(2 * n, 512)
    counts = cnt[0, :N_EXPERTS].astype(jnp.int32)
    padded = ((counts + tm - 1) // tm) * tm
    pad_end = jnp.cumsum(padded)
    pad_start = pad_end - padded
    experts = route[:, 0:2].astype(jnp.int32)
    ranks = route[:, 4:6].astype(jnp.int32)
    dest = (pad_start[experts] + ranks).reshape(-1)
    p = 2 * n + (N_EXPERTS + GATHER_SLOTS - 2) * tm
    n_blocks = p // tm
    src_tok = jnp.zeros((p,), jnp.int32).at[dest].set(jnp.arange(2 * n, dtype=jnp.int32) // 2)
    block_start = jnp.arange(n_blocks, dtype=jnp.int32) * tm
    block_e = jnp.minimum(jnp.sum(block_start[:, None] >= pad_end[None, :], axis=1),
                          N_EXPERTS - 1).astype(jnp.int32)
    n_used = (pad_end[-1:] // tm).astype(jnp.int32)

    y_buf = moe_ffn(src_tok, block_e, n_used, hn, _col_blocks(moe_w_gate[0], FF_TC),
                    _col_blocks(moe_w_up[0], FF_TC), moe_w_down[0].astype(BF16), tm=tm)
    out = combine(dest, x3, route, vec(final_norm), y_buf, tm=_tile(n, 256))
    return out.reshape(batch, seq, d)
```

```python
import functools
import math

import jax
import jax.numpy as jnp
from jax import lax
from jax.experimental import pallas as pl
from jax.experimental.pallas import tpu as pltpu

EPS = 1e-6
RG_HEADS = 16
RG_BLOCK = 128
CONV_W = 4
RG_C = 8.0
MLA_HEADS = 16
Q_LORA = 512
KV_LORA = 512
QK_NOPE = 128
QK_ROPE = 64
V_DIM = 128
QK_DIM = QK_NOPE + QK_ROPE
ROPE_THETA = 10000.0
N_EXPERTS = 8
LANES = 128
SUBLANES = 8
VMEM_LIMIT_BYTES = 56 * 1024 * 1024
FF_TC = 256
FF_TM = 1024

BF16 = jnp.bfloat16
F32 = jnp.float32


def _params(*sem):
    return pltpu.CompilerParams(dimension_semantics=sem, vmem_limit_bytes=VMEM_LIMIT_BYTES)


def _rms(x, g):
    return x * lax.rsqrt(jnp.mean(x * x, axis=-1, keepdims=True) + EPS) * g


def _dot(a, b):
    return jnp.dot(a, b, preferred_element_type=F32)


def _sigmoid(x):
    return 1.0 / (1.0 + jnp.exp(-x))


def _norm_mm_kernel(x_ref, g_ref, w_ref, o_ref, hn_ref):
    @pl.when(pl.program_id(1) == 0)
    def _():
        hn_ref[...] = _rms(x_ref[...], g_ref[...]).astype(BF16)

    o_ref[...] = _dot(hn_ref[...], w_ref[...]).astype(o_ref.dtype)


def norm_mm(x, g, w, *, tm, tn, out_dtype):
    n, d = x.shape
    f = w.shape[1]
    return pl.pallas_call(
        _norm_mm_kernel,
        grid=(n // tm, f // tn),
        in_specs=[pl.BlockSpec((tm, d), lambda i, j: (i, 0)),
                  pl.BlockSpec((1, d), lambda i, j: (0, 0)),
                  pl.BlockSpec((d, tn), lambda i, j: (0, j))],
        out_specs=pl.BlockSpec((tm, tn), lambda i, j: (i, j)),
        out_shape=jax.ShapeDtypeStruct((n, f), out_dtype),
        scratch_shapes=[pltpu.VMEM((tm, d), BF16)],
        compiler_params=_params("parallel", "arbitrary"),
        name="norm_mm",
    )(x, g, w)


def _mm_res_kernel(y_ref, w_ref, r_ref, o_ref):
    o_ref[...] = r_ref[...] + _dot(y_ref[...], w_ref[...])


def mm_res(y, w, res, *, tm, tn):
    n, k = y.shape
    f = w.shape[1]
    return pl.pallas_call(
        _mm_res_kernel,
        grid=(n // tm, f // tn),
        in_specs=[pl.BlockSpec((tm, k), lambda i, j: (i, 0)),
                  pl.BlockSpec((k, tn), lambda i, j: (0, j)),
                  pl.BlockSpec((tm, tn), lambda i, j: (i, j))],
        out_specs=pl.BlockSpec((tm, tn), lambda i, j: (i, j)),
        out_shape=jax.ShapeDtypeStruct((n, f), F32),
        compiler_params=_params("parallel", "parallel"),
        name="mm_res",
    )(y, w, res)


def _gelu_tanh(x):
    c = math.sqrt(2.0 / math.pi)
    return 0.5 * x * (1.0 + jnp.tanh(c * (x + 0.044715 * (x * x * x))))


def _softplus(z):
    return jnp.maximum(z, 0.0) + jnp.log1p(jnp.exp(-jnp.abs(z)))


def _rglru_kernel(gate_ref, rec_ref, cw_ref, cb_ref, wax_ref, ba_ref, bx_ref, lam_ref,
                  y_ref, xbuf, a_s, b_s, h_s):
    s = pl.program_id(1)
    t = rec_ref.shape[0]
    d = rec_ref.shape[1]

    @pl.when(s == 0)
    def _():
        xbuf[0:SUBLANES, :] = jnp.zeros((SUBLANES, d), F32)
        h_s[...] = jnp.zeros_like(h_s)

    @pl.when(s > 0)
    def _():
        xbuf[0:SUBLANES, :] = xbuf[t:t + SUBLANES, :]

    xbuf[SUBLANES:t + SUBLANES, :] = rec_ref[...].astype(F32)

    row = lax.broadcasted_iota(jnp.int32, (t, RG_BLOCK), 0) % SUBLANES
    for h in range(RG_HEADS):
        c0, c1 = h * RG_BLOCK, (h + 1) * RG_BLOCK
        xc = cb_ref[:, c0:c1] + jnp.zeros((t, RG_BLOCK), F32)
        for k in range(CONV_W):
            sh = CONV_W - 1 - k
            xc = xc + xbuf[SUBLANES - sh:SUBLANES - sh + t, c0:c1] * cw_ref[k:k + 1, c0:c1]
        gg = _dot(xc.astype(BF16), wax_ref[h])
        r = _sigmoid(gg[:, :RG_BLOCK] + ba_ref[:, c0:c1])
        i = _sigmoid(gg[:, RG_BLOCK:] + bx_ref[:, c0:c1])
        log_a = -RG_C * r * _softplus(-lam_ref[:, c0:c1])
        a = jnp.exp(log_a)
        b = jnp.sqrt(1.0 - a * a) * i * xc
        for sft in (1, 2, 4):
            a_sh = pltpu.roll(a, sft, axis=0)
            b_sh = pltpu.roll(b, sft, axis=0)
            m = row >= sft
            b = jnp.where(m, a * b_sh + b, b)
            a = jnp.where(m, a * a_sh, a)
        a_s[:, c0:c1] = a
        b_s[:, c0:c1] = b

    def group(gi, hc):
        r0 = pl.multiple_of(gi * SUBLANES, SUBLANES)
        rows = b_s[pl.ds(r0, SUBLANES), :] + a_s[pl.ds(r0, SUBLANES), :] * hc
        b_s[pl.ds(r0, SUBLANES), :] = rows
        return rows[SUBLANES - 1:SUBLANES, :]

    h_s[...] = lax.fori_loop(0, t // SUBLANES, group, h_s[...])
    y_ref[...] = (b_s[...] * _gelu_tanh(gate_ref[...].astype(F32))).astype(y_ref.dtype)


def rglru_core(u, conv_w, conv_b, wax, b_a, b_x, lam, *, batch, seq, t):
    n = u.shape[0]
    d = u.shape[1] // 2
    ns = seq // t
    vec = pl.BlockSpec((1, d), lambda b, s: (0, 0))
    return pl.pallas_call(
        _rglru_kernel,
        grid=(batch, ns),
        in_specs=[pl.BlockSpec((t, d), lambda b, s: (b * ns + s, 0)),
                  pl.BlockSpec((t, d), lambda b, s: (b * ns + s, 1)),
                  pl.BlockSpec((CONV_W, d), lambda b, s: (0, 0)),
                  vec,
                  pl.BlockSpec((RG_HEADS, RG_BLOCK, 2 * RG_BLOCK), lambda b, s: (0, 0, 0)),
                  vec, vec, vec],
        out_specs=pl.BlockSpec((t, d), lambda b, s: (b * ns + s, 0)),
        out_shape=jax.ShapeDtypeStruct((n, d), BF16),
        scratch_shapes=[pltpu.VMEM((t + SUBLANES, d), F32),
                        pltpu.VMEM((t, d), F32),
                        pltpu.VMEM((t, d), F32),
                        pltpu.VMEM((1, d), F32)],
        compiler_params=_params("arbitrary", "arbitrary"),
        name="rglru_core",
    )(u, u, conv_w, conv_b, wax, b_a, b_x, lam)


def _swiglu_acc(h, wg_ref, wu_ref, wd_ref):
    g = _dot(h, wg_ref[...])
    u = _dot(h, wu_ref[...])
    a = (g * _sigmoid(g) * u).astype(BF16)
    return _dot(a, wd_ref[...])


def _ffn_kernel(x_ref, g_ref, wg_ref, wu_ref, wd_ref, o_ref, hn_ref):
    @pl.when(pl.program_id(1) == 0)
    def _():
        x = x_ref[...]
        hn_ref[...] = _rms(x, g_ref[...]).astype(BF16)
        o_ref[...] = x

    o_ref[...] += _swiglu_acc(hn_ref[...], wg_ref, wu_ref, wd_ref)


def ffn_dense(x, g, wg, wu, wd, *, tm, tc):
    n, d = x.shape
    f = wg.shape[1]
    return pl.pallas_call(
        _ffn_kernel,
        grid=(n // tm, f // tc),
        in_specs=[pl.BlockSpec((tm, d), lambda i, j: (i, 0)),
                  pl.BlockSpec((1, d), lambda i, j: (0, 0)),
                  pl.BlockSpec((d, tc), lambda i, j: (0, j)),
                  pl.BlockSpec((d, tc), lambda i, j: (0, j)),
                  pl.BlockSpec((tc, d), lambda i, j: (j, 0))],
        out_specs=pl.BlockSpec((tm, d), lambda i, j: (i, 0)),
        out_shape=jax.ShapeDtypeStruct((n, d), F32),
        scratch_shapes=[pltpu.VMEM((tm, d), BF16)],
        compiler_params=_params("parallel", "arbitrary"),
        name="ffn_dense",
    )(x, g, wg, wu, wd)


def _mla_proj_kernel(x_ref, pos_ref, g_ref, win_ref, qn_ref, wq_ref, kvn_ref, wkv_ref,
                     freq_ref, sign_ref, qnope_ref, qpe_ref, knope_ref, kpe_ref, v_ref):
    d = x_ref.shape[1]
    hn = _rms(x_ref[...], g_ref[...]).astype(BF16)
    u = _dot(hn, win_ref[...])
    ang = pos_ref[...].astype(F32) * freq_ref[...]
    cos = jnp.cos(ang)
    sin = jnp.sin(ang) * sign_ref[...]

    def rope(xs):
        return xs * cos + pltpu.roll(xs, LANES // 2, axis=1) * sin

    kpe_ref[...] = rope(u[:, Q_LORA + KV_LORA:]).astype(BF16)
    q = _dot(_rms(u[:, :Q_LORA], qn_ref[...]).astype(BF16), wq_ref[...])
    qnope_ref[...] = q[:, :d].astype(BF16)
    for h in range(MLA_HEADS):
        c0, c1 = h * LANES, (h + 1) * LANES
        qpe_ref[:, c0:c1] = rope(q[:, d + c0:d + c1]).astype(BF16)
    kv = _dot(_rms(u[:, Q_LORA:Q_LORA + KV_LORA], kvn_ref[...]).astype(BF16), wkv_ref[...])
    knope_ref[...] = kv[:, :d].astype(BF16)
    v_ref[...] = kv[:, d:].astype(BF16)


def mla_proj(x, pos, g, w_in, q_norm, w_q, kv_norm, w_kv, freq, sign, *, tm):
    n, d = x.shape
    row = lambda w: pl.BlockSpec((tm, w), lambda i: (i, 0))
    full = lambda a: pl.BlockSpec(a.shape, lambda i: (0, 0))
    big = jax.ShapeDtypeStruct((n, d), BF16)
    return pl.pallas_call(
        _mla_proj_kernel,
        grid=(n // tm,),
        in_specs=[row(d), row(1), full(g), full(w_in), full(q_norm), full(w_q), full(kv_norm),
                  full(w_kv), full(freq), full(sign)],
        out_specs=[row(d), row(d), row(d), row(LANES), row(d)],
        out_shape=[big, big, big, jax.ShapeDtypeStruct((n, LANES), BF16), big],
        compiler_params=_params("parallel"),
        name="mla_proj",
    )(x, pos, g, w_in, q_norm, w_q, kv_norm, w_kv, freq, sign)


def _attn_kernel(qn_ref, qp_ref, kn_ref, kp_ref, v_ref, o_ref, m_s, acc_s, bias_s):
    qi = pl.program_id(2)
    tq = qn_ref.shape[0]
    heads = qn_ref.shape[1] // LANES
    c = (QK_DIM ** -0.5) * math.log2(math.e)
    hs = lambda h: slice(h * LANES, (h + 1) * LANES)
    q = [jnp.concatenate([qn_ref[:, hs(h)], qp_ref[:, hs(h)]], axis=1) for h in range(heads)]
    ones = jnp.ones((tq, LANES), BF16)
    m_s[...] = jnp.full_like(m_s, -jnp.inf)
    acc_s[...] = jnp.zeros_like(acc_s)
    q_idx = lax.broadcasted_iota(jnp.int32, (tq, tq), 0)
    k_idx = lax.broadcasted_iota(jnp.int32, (tq, tq), 1)
    bias_s[...] = jnp.where(k_idx <= q_idx, 0.0, -jnp.inf)

    def chunk(kj, masked):
        k0 = pl.multiple_of(kj * tq, tq)
        kp = kp_ref[pl.ds(k0, tq), :]
        for h in range(heads):
            k = jnp.concatenate([kn_ref[pl.ds(k0, tq), hs(h)], kp], axis=1)
            s = lax.dot_general(q[h], k, (((1,), (1,)), ((), ())), preferred_element_type=F32)
            if masked:
                s = s + bias_s[...]
            slabs = [s[:, hs(j)] for j in range(tq // LANES)]
            part = functools.reduce(jnp.maximum, slabs)
            m_prev = m_s[h]
            m_new = jnp.maximum(m_prev, jnp.max(part, axis=1, keepdims=True))
            p = jnp.concatenate([jnp.exp2((sl - m_new) * c) for sl in slabs], axis=1)
            alpha = jnp.exp2((m_prev - m_new) * c)
            vv = jnp.concatenate([v_ref[pl.ds(k0, tq), hs(h)], ones], axis=1)
            acc = acc_s[h]
            acc = jnp.concatenate([acc[:, :LANES] * alpha, acc[:, LANES:] * alpha], axis=1)
            acc_s[h] = acc + _dot(p.astype(BF16), vv)
            m_s[h] = m_new

    def body(kj, carry):
        chunk(kj, False)
        return carry

    lax.fori_loop(0, qi, body, 0)
    chunk(qi, True)
    for h in range(heads):
        acc = acc_s[h]
        o_ref[:, hs(h)] = (acc[:, :LANES] / acc[:, LANES:]).astype(o_ref.dtype)


def attention(q_nope, q_pe, k_nope, k_pe, v, *, batch, seq, tq, heads):
    n, d = q_nope.shape
    nq = seq // tq
    w = heads * LANES
    qspec = pl.BlockSpec((tq, w), lambda b, h, i: (b * nq + i, h))
    kspec = pl.BlockSpec((seq, w), lambda b, h, i: (b, h))
    return pl.pallas_call(
        _attn_kernel,
        grid=(batch, MLA_HEADS // heads, nq),
        in_specs=[qspec, qspec, kspec,
                  pl.BlockSpec((seq, LANES), lambda b, h, i: (b, 0)),
                  kspec],
        out_specs=qspec,
        out_shape=jax.ShapeDtypeStruct((n, d), BF16),
        scratch_shapes=[pltpu.VMEM((heads, tq, LANES), F32),
                        pltpu.VMEM((heads, tq, 2 * LANES), F32),
                        pltpu.VMEM((tq, tq), F32)],
        compiler_params=_params("parallel", "parallel", "arbitrary"),
        name="attention",
    )(q_nope, q_pe, k_nope, k_pe, v)


def _out_router_kernel(o_ref, w_ref, r_ref, g_ref, wr_ref, x_ref, hn_ref, route_ref, cnt_ref, carry):
    i = pl.program_id(0)
    tm = o_ref.shape[0]

    @pl.when(i == 0)
    def _():
        carry[...] = jnp.zeros_like(carry)

    x = r_ref[...] + _dot(o_ref[...], w_ref[...])
    x_ref[...] = x
    hn = _rms(x, g_ref[...])
    hn_ref[...] = hn
    hn_hi = hn.astype(BF16)
    hn_lo = (hn - hn_hi.astype(F32)).astype(BF16)
    parts = _dot(hn_hi, wr_ref[...]) + _dot(hn_lo, wr_ref[...])
    logits = parts[:, :LANES] + parts[:, LANES:]
    lane = lax.broadcasted_iota(jnp.int32, (tm, LANES), 1)
    lane_f = lane.astype(F32)
    neg = -jnp.inf
    logits = jnp.where(lane < N_EXPERTS, logits, neg)
    m1 = jnp.max(logits, axis=1, keepdims=True)
    e1 = jnp.min(jnp.where(logits == m1, lane_f, float(LANES)), axis=1, keepdims=True)
    rest = jnp.where(lane_f == e1, neg, logits)
    m2 = jnp.max(rest, axis=1, keepdims=True)
    e2 = jnp.min(jnp.where(rest == m2, lane_f, float(LANES)), axis=1, keepdims=True)
    z = jnp.exp(m2 - m1)
    g1 = 1.0 / (1.0 + z)
    g2 = z / (1.0 + z)
    sel1 = lane_f == e1
    sel2 = lane_f == e2
    sel = jnp.where(sel1 | sel2, 1.0, 0.0)
    rr = lax.broadcasted_iota(jnp.int32, (tm, tm), 0)
    cc = lax.broadcasted_iota(jnp.int32, (tm, tm), 1)
    tril = jnp.where(cc < rr, 1.0, 0.0).astype(BF16)
    before = _dot(tril, sel.astype(BF16)) + carry[...]
    rank1 = jnp.sum(jnp.where(sel1, before, 0.0), axis=1, keepdims=True)
    rank2 = jnp.sum(jnp.where(sel2, before, 0.0), axis=1, keepdims=True)
    total = carry[...] + jnp.sum(sel, axis=0, keepdims=True)
    carry[...] = total
    cnt_ref[...] = jnp.broadcast_to(total, cnt_ref.shape)
    route = jnp.zeros((tm, LANES), F32)
    for col, val in enumerate((e1, e2, g1, g2, rank1, rank2)):
        route = jnp.where(lane == col, val, route)
    route_ref[...] = route


def out_router(o, w_out, res, g, w_router, *, tm):
    n, d = res.shape
    row = lambda w: pl.BlockSpec((tm, w), lambda i: (i, 0))
    full = lambda a: pl.BlockSpec(a.shape, lambda i: (0, 0))
    return pl.pallas_call(
        _out_router_kernel,
        grid=(n // tm,),
        in_specs=[row(d), full(w_out), row(d), full(g), full(w_router)],
        out_specs=[row(d), row(d), row(LANES), pl.BlockSpec((SUBLANES, LANES), lambda i: (0, 0))],
        out_shape=[jax.ShapeDtypeStruct((n, d), F32), jax.ShapeDtypeStruct((n, d), F32),
                   jax.ShapeDtypeStruct((n, LANES), F32),
                   jax.ShapeDtypeStruct((SUBLANES, LANES), F32)],
        scratch_shapes=[pltpu.VMEM((1, LANES), F32)],
        compiler_params=_params("arbitrary"),
        name="out_router",
    )(o, w_out, res, g, w_router)


GATHER_SLOTS = 2


def _row_copy(src_hbm, row, dst, r, sem):
    return pltpu.make_async_copy(src_hbm.at[pl.ds(row, 1), :], dst.at[pl.ds(r, 1), :], sem)


def _moe_ffn_kernel(src_ref, be_ref, valid_ref, nu_ref, hn_hbm, wg_ref, wu_ref, wd_ref, o_ref, xg, xb, sems,
                    *, per_step):
    i = pl.program_id(0)
    j = pl.program_id(1)
    tm = o_ref.shape[0]
    rows = xg.shape[1]
    n_used = nu_ref[0]
    slot = i % GATHER_SLOTS

    def start_row(blk, sl, r):
        tok = src_ref[blk * tm + jnp.minimum(r, tm - 1)]
        _row_copy(hn_hbm, tok, xg.at[sl], r, sems.at[sl]).start()

    @pl.when((i == 0) & (j == 0))
    def _():
        def first(r, c):
            start_row(0, 0, r)
            return c
        lax.fori_loop(0, rows, first, 0, unroll=8)

    @pl.when((j == 0) & (i <= n_used))
    def _():
        pltpu.make_async_copy(hn_hbm.at[pl.ds(0, rows), :], xg.at[slot], sems.at[slot]).wait()

    @pl.when((j == 0) & (i < n_used))
    def _():
        xb[...] = xg[slot, :tm, :].astype(BF16)

    @pl.when(j == 0)
    def _():
        o_ref[...] = jnp.zeros_like(o_ref)

    def step(m):
        nxt = jnp.minimum(i + 1, n_used - 1)
        for u in range(per_step):
            start_row(nxt, (i + 1) % GATHER_SLOTS, j * per_step + u)
        o_ref[:m, :] += _swiglu_acc(xb[:m, :], wg_ref, wu_ref, wd_ref)

    valid = valid_ref[i]

    @pl.when((i < n_used) & (valid > tm // 2))
    def _():
        step(tm)

    @pl.when((i < n_used) & (valid <= tm // 2))
    def _():
        step(tm // 2)


def moe_ffn(src_tok, block_e, block_valid, n_used, hn, wg, wu, wd, *, tm, tc):
    d = hn.shape[1]
    p = src_tok.shape[0]
    nj = wg.shape[2] // tc
    per_step = -(-tm // nj)
    while (per_step * nj) % SUBLANES:
        per_step += 1

    def blk(i, nu):
        return jnp.minimum(i, nu[0] - 1)

    def col(i, j, nu):
        return jnp.where(i < nu[0], j, nj - 1)

    return pl.pallas_call(
        functools.partial(_moe_ffn_kernel, per_step=per_step),
        grid_spec=pltpu.PrefetchScalarGridSpec(
            num_scalar_prefetch=4,
            grid=(p // tm, nj),
            in_specs=[pl.BlockSpec(memory_space=pl.ANY),
                      pl.BlockSpec((None, d, tc), lambda i, j, s, be, bv, nu: (be[blk(i, nu)], 0, col(i, j, nu))),
                      pl.BlockSpec((None, d, tc), lambda i, j, s, be, bv, nu: (be[blk(i, nu)], 0, col(i, j, nu))),
                      pl.BlockSpec((None, tc, d), lambda i, j, s, be, bv, nu: (be[blk(i, nu)], col(i, j, nu), 0))],
            out_specs=pl.BlockSpec((tm, d), lambda i, j, s, be, bv, nu: (i, 0)),
            scratch_shapes=[pltpu.VMEM((GATHER_SLOTS, per_step * nj, d), F32), pltpu.VMEM((tm, d), BF16),
                            pltpu.SemaphoreType.DMA((GATHER_SLOTS,))]),
        out_shape=jax.ShapeDtypeStruct((p, d), F32),
        compiler_params=_params("arbitrary", "arbitrary"),
        name="moe_ffn",
    )(src_tok, block_e, block_valid, n_used, hn, wg, wu, wd)


def _combine_kernel(dest_ref, x_ref, route_ref, g_ref, y_hbm, o_ref, ya, yb, sems):
    i = pl.program_id(0)
    tm = o_ref.shape[0]
    slot = i % 2

    def copies(blk, sl, r):
        t2 = 2 * (blk * tm + r)
        return (_row_copy(y_hbm, dest_ref[t2], ya.at[sl], r, sems.at[0, sl]),
                _row_copy(y_hbm, dest_ref[t2 + 1], yb.at[sl], r, sems.at[1, sl]))

    def issue(blk, sl):
        def body(r, c):
            ca, cb = copies(blk, sl, r)
            ca.start()
            cb.start()
            return c
        lax.fori_loop(0, tm, body, 0, unroll=8)

    @pl.when(i == 0)
    def _():
        issue(0, 0)

    @pl.when(i + 1 < pl.num_programs(0))
    def _():
        issue(i + 1, 1 - slot)

    pltpu.make_async_copy(y_hbm.at[pl.ds(0, tm), :], ya.at[slot], sems.at[0, slot]).wait()
    pltpu.make_async_copy(y_hbm.at[pl.ds(0, tm), :], yb.at[slot], sems.at[1, slot]).wait()
    route = route_ref[...]
    x = x_ref[...] + (ya[slot] * route[:, 2:3] + yb[slot] * route[:, 3:4])
    o_ref[...] = _rms(x, g_ref[...])


def combine(dest, x, route, g, y_buf, *, tm):
    n, d = x.shape
    return pl.pallas_call(
        _combine_kernel,
        grid_spec=pltpu.PrefetchScalarGridSpec(
            num_scalar_prefetch=1,
            grid=(n // tm,),
            in_specs=[pl.BlockSpec((tm, d), lambda i, dst: (i, 0)),
                      pl.BlockSpec((tm, LANES), lambda i, dst: (i, 0)),
                      pl.BlockSpec((1, d), lambda i, dst: (0, 0)),
                      pl.BlockSpec(memory_space=pl.ANY)],
            out_specs=pl.BlockSpec((tm, d), lambda i, dst: (i, 0)),
            scratch_shapes=[pltpu.VMEM((2, tm, d), F32), pltpu.VMEM((2, tm, d), F32),
                            pltpu.SemaphoreType.DMA((2, 2))]),
        out_shape=jax.ShapeDtypeStruct((n, d), F32),
        compiler_params=_params("arbitrary"),
        name="moe_combine",
    )(dest, x, route, g, y_buf)


def _spread_rope(w):
    half = QK_ROPE // 2
    z = jnp.zeros(w.shape[:-1] + (half,), w.dtype)
    return jnp.concatenate([w[..., :half], z, w[..., half:], z], axis=-1)


def _tile(n, want):
    t = min(n, want)
    assert n % t == 0, (n, t)
    return t


def kernel(x, positions, a_norm_mix, rg_w_in, rg_conv_w, rg_conv_b, rg_w_a, rg_b_a, rg_w_x, rg_b_x, rg_lambda, rg_w_out, a_norm_ffn, ff_w_gate, ff_w_up, ff_w_down, b_norm_mix, mla_w_in, mla_q_norm, mla_w_q_up, mla_kv_norm, mla_w_kv_up, mla_w_out, b_norm_ffn, moe_w_router, moe_w_gate, moe_w_up, moe_w_down, final_norm):
    batch, seq, d = x.shape
    n = batch * seq
    assert a_norm_mix.shape[0] == 1 and b_norm_mix.shape[0] == 1
    xf = x.reshape(n, d)
    vec = lambda a: a.reshape(1, -1).astype(F32)

    u = norm_mm(xf, vec(a_norm_mix[0]), rg_w_in[0].astype(BF16),
                tm=_tile(n, 1024), tn=1024, out_dtype=BF16)
    wax = jnp.concatenate([rg_w_a[0], rg_w_x[0]], axis=-1).astype(BF16)
    y = rglru_core(u, rg_conv_w[0], vec(rg_conv_b[0]), wax, vec(rg_b_a[0]), vec(rg_b_x[0]),
                   vec(rg_lambda[0]), batch=batch, seq=seq, t=_tile(seq, 256))
    x1 = mm_res(y, rg_w_out[0].astype(BF16), xf, tm=_tile(n, 1024), tn=1024)
    x2 = ffn_dense(x1, vec(a_norm_ffn[0]), ff_w_gate[0].astype(BF16), ff_w_up[0].astype(BF16),
                   ff_w_down[0].astype(BF16), tm=_tile(n, FF_TM), tc=FF_TC)

    w_in = mla_w_in[0]
    w_in_p = jnp.concatenate([w_in[:, :Q_LORA + KV_LORA], _spread_rope(w_in[:, Q_LORA + KV_LORA:])],
                             axis=1).astype(BF16)
    wq = mla_w_q_up[0].reshape(Q_LORA, MLA_HEADS, QK_DIM)
    wq_p = jnp.concatenate([wq[..., :QK_NOPE].reshape(Q_LORA, -1),
                            _spread_rope(wq[..., QK_NOPE:]).reshape(Q_LORA, -1)], axis=1).astype(BF16)
    wkv = mla_w_kv_up[0].reshape(KV_LORA, MLA_HEADS, QK_NOPE + V_DIM)
    wkv_p = jnp.concatenate([wkv[..., :QK_NOPE].reshape(KV_LORA, -1),
                             wkv[..., QK_NOPE:].reshape(KV_LORA, -1)], axis=1).astype(BF16)
    inv_freq = 1.0 / (ROPE_THETA ** (jnp.arange(0, QK_ROPE, 2, dtype=F32) / QK_ROPE))
    freq = _spread_rope(jnp.concatenate([inv_freq, inv_freq])[None, :])
    ones = jnp.ones((1, QK_ROPE // 2), F32)
    sign = _spread_rope(jnp.concatenate([-ones, ones], axis=1))
    pos = positions.reshape(n, 1).astype(jnp.int32)
    q_nope, q_pe, k_nope, k_pe, v = mla_proj(
        x2, pos, vec(b_norm_mix[0]), w_in_p, vec(mla_q_norm[0]), wq_p, vec(mla_kv_norm[0]), wkv_p,
        freq, sign, tm=_tile(n, 512))
    o = attention(q_nope, q_pe, k_nope, k_pe, v, batch=batch, seq=seq, tq=_tile(seq, 512), heads=8)
    w_router = jnp.pad(moe_w_router[0].astype(F32), ((0, 0), (0, LANES - N_EXPERTS)))
    wr_hi = w_router.astype(BF16)
    w_router = jnp.concatenate([wr_hi, (w_router - wr_hi.astype(F32)).astype(BF16)], axis=1)
    x3, hn, route, cnt = out_router(o, mla_w_out[0].astype(BF16), x2, vec(b_norm_ffn[0]), w_router,
                                    tm=_tile(n, 512))

    tm = _tile(2 * n, FF_TM)
    counts = cnt[0, :N_EXPERTS].astype(jnp.int32)
    padded = ((counts + tm - 1) // tm) * tm
    pad_end = jnp.cumsum(padded)
    pad_start = pad_end - padded
    experts = route[:, 0:2].astype(jnp.int32)
    ranks = route[:, 4:6].astype(jnp.int32)
    dest = (pad_start[experts] + ranks).reshape(-1)
    p = 2 * n + N_EXPERTS * tm
    n_blocks = p // tm
    src_tok = jnp.zeros((p,), jnp.int32).at[dest].set(jnp.arange(2 * n, dtype=jnp.int32) // 2)
    block_start = jnp.arange(n_blocks, dtype=jnp.int32) * tm
    block_e = jnp.minimum(jnp.sum(block_start[:, None] >= pad_end[None, :], axis=1),
                          N_EXPERTS - 1).astype(jnp.int32)
    block_valid = jnp.clip((pad_start + counts)[block_e] - block_start, 0, tm).astype(jnp.int32)
    n_used = (pad_end[-1:] // tm).astype(jnp.int32)

    y_buf = moe_ffn(src_tok, block_e, block_valid, n_used, hn, moe_w_gate[0].astype(BF16),
                    moe_w_up[0].astype(BF16), moe_w_down[0].astype(BF16), tm=tm, tc=FF_TC)
    out = combine(dest, x3, route, vec(final_norm), y_buf, tm=_tile(n, 256))
    return out.reshape(batch, seq, d)
```

```python
import functools
import math

import jax
import jax.numpy as jnp
from jax import lax
from jax.experimental import pallas as pl
from jax.experimental.pallas import tpu as pltpu

EPS = 1e-6
RG_HEADS = 16
RG_BLOCK = 128
CONV_W = 4
RG_C = 8.0
MLA_HEADS = 16
Q_LORA = 512
KV_LORA = 512
QK_NOPE = 128
QK_ROPE = 64
V_DIM = 128
QK_DIM = QK_NOPE + QK_ROPE
ROPE_THETA = 10000.0
N_EXPERTS = 8
LANES = 128
SUBLANES = 8
VMEM_LIMIT_BYTES = 56 * 1024 * 1024
FF_TC = 256
FF_TM = 1024

BF16 = jnp.bfloat16
F32 = jnp.float32


def _params(*sem):
    return pltpu.CompilerParams(dimension_semantics=sem, vmem_limit_bytes=VMEM_LIMIT_BYTES)


def _rms(x, g):
    return x * lax.rsqrt(jnp.mean(x * x, axis=-1, keepdims=True) + EPS) * g


def _dot(a, b):
    return jnp.dot(a, b, preferred_element_type=F32)


def _sigmoid(x):
    return 1.0 / (1.0 + jnp.exp(-x))


def _sigmoid_tanh(x):
    return 0.5 * jnp.tanh(0.5 * x) + 0.5


def _norm_mm_kernel(x_ref, g_ref, w_ref, o_ref, hn_ref):
    @pl.when(pl.program_id(1) == 0)
    def _():
        hn_ref[...] = _rms(x_ref[...], g_ref[...]).astype(BF16)

    o_ref[...] = _dot(hn_ref[...], w_ref[...]).astype(o_ref.dtype)


def norm_mm(x, g, w, *, tm, tn, out_dtype):
    n, d = x.shape
    f = w.shape[1]
    return pl.pallas_call(
        _norm_mm_kernel,
        grid=(n // tm, f // tn),
        in_specs=[pl.BlockSpec((tm, d), lambda i, j: (i, 0)),
                  pl.BlockSpec((1, d), lambda i, j: (0, 0)),
                  pl.BlockSpec((d, tn), lambda i, j: (0, j))],
        out_specs=pl.BlockSpec((tm, tn), lambda i, j: (i, j)),
        out_shape=jax.ShapeDtypeStruct((n, f), out_dtype),
        scratch_shapes=[pltpu.VMEM((tm, d), BF16)],
        compiler_params=_params("parallel", "arbitrary"),
        name="norm_mm",
    )(x, g, w)


def _mm_res_kernel(y_ref, w_ref, r_ref, o_ref):
    o_ref[...] = r_ref[...] + _dot(y_ref[...], w_ref[...])


def mm_res(y, w, res, *, tm, tn):
    n, k = y.shape
    f = w.shape[1]
    return pl.pallas_call(
        _mm_res_kernel,
        grid=(n // tm, f // tn),
        in_specs=[pl.BlockSpec((tm, k), lambda i, j: (i, 0)),
                  pl.BlockSpec((k, tn), lambda i, j: (0, j)),
                  pl.BlockSpec((tm, tn), lambda i, j: (i, j))],
        out_specs=pl.BlockSpec((tm, tn), lambda i, j: (i, j)),
        out_shape=jax.ShapeDtypeStruct((n, f), F32),
        compiler_params=_params("parallel", "parallel"),
        name="mm_res",
    )(y, w, res)


def _gelu_tanh(x):
    c = math.sqrt(2.0 / math.pi)
    return 0.5 * x * (1.0 + jnp.tanh(c * (x + 0.044715 * (x * x * x))))


def _softplus(z):
    return jnp.maximum(z, 0.0) + jnp.log1p(jnp.exp(-jnp.abs(z)))


def _rglru_kernel(gate_ref, rec_ref, cw_ref, cb_ref, wax_ref, ba_ref, bx_ref, lam_ref,
                  y_ref, xbuf, a_s, b_s, h_s):
    s = pl.program_id(1)
    t = rec_ref.shape[0]
    d = rec_ref.shape[1]

    @pl.when(s == 0)
    def _():
        xbuf[0:SUBLANES, :] = jnp.zeros((SUBLANES, d), F32)
        h_s[...] = jnp.zeros_like(h_s)

    @pl.when(s > 0)
    def _():
        xbuf[0:SUBLANES, :] = xbuf[t:t + SUBLANES, :]

    xbuf[SUBLANES:t + SUBLANES, :] = rec_ref[...].astype(F32)

    row = lax.broadcasted_iota(jnp.int32, (t // SUBLANES, SUBLANES, RG_BLOCK), 1)
    for h in range(RG_HEADS):
        c0, c1 = h * RG_BLOCK, (h + 1) * RG_BLOCK
        xc = cb_ref[:, c0:c1] + jnp.zeros((t, RG_BLOCK), F32)
        for k in range(CONV_W):
            sh = CONV_W - 1 - k
            xc = xc + xbuf[SUBLANES - sh:SUBLANES - sh + t, c0:c1] * cw_ref[k:k + 1, c0:c1]
        gg = _dot(xc.astype(BF16), wax_ref[h])
        r = _sigmoid_tanh(gg[:, :RG_BLOCK] + ba_ref[:, c0:c1])
        i = _sigmoid_tanh(gg[:, RG_BLOCK:] + bx_ref[:, c0:c1])
        log_a = -RG_C * r * _softplus(-lam_ref[:, c0:c1])
        a = jnp.exp(log_a)
        om = 1.0 - a * a
        b = jnp.where(om > 0.0, om * lax.rsqrt(om), 0.0) * i * xc
        a = a.reshape(t // SUBLANES, SUBLANES, RG_BLOCK)
        b = b.reshape(t // SUBLANES, SUBLANES, RG_BLOCK)
        for sft in (1, 2, 4):
            a_sh = pltpu.roll(a, sft, axis=1)
            b_sh = pltpu.roll(b, sft, axis=1)
            m = row >= sft
            b = jnp.where(m, a * b_sh + b, b)
            a = jnp.where(m, a * a_sh, a)
        a_s[:, c0:c1] = a.reshape(t, RG_BLOCK)
        b_s[:, c0:c1] = b.reshape(t, RG_BLOCK)

    def group(gi, hc):
        r0 = pl.multiple_of(gi * SUBLANES, SUBLANES)
        rows = b_s[pl.ds(r0, SUBLANES), :] + a_s[pl.ds(r0, SUBLANES), :] * hc
        b_s[pl.ds(r0, SUBLANES), :] = rows
        return rows[SUBLANES - 1:SUBLANES, :]

    h_s[...] = lax.fori_loop(0, t // SUBLANES, group, h_s[...])
    y_ref[...] = (b_s[...] * _gelu_tanh(gate_ref[...].astype(F32))).astype(y_ref.dtype)


def rglru_core(u, conv_w, conv_b, wax, b_a, b_x, lam, *, batch, seq, t):
    n = u.shape[0]
    d = u.shape[1] // 2
    ns = seq // t
    vec = pl.BlockSpec((1, d), lambda b, s: (0, 0))
    return pl.pallas_call(
        _rglru_kernel,
        grid=(batch, ns),
        in_specs=[pl.BlockSpec((t, d), lambda b, s: (b * ns + s, 0)),
                  pl.BlockSpec((t, d), lambda b, s: (b * ns + s, 1)),
                  pl.BlockSpec((CONV_W, d), lambda b, s: (0, 0)),
                  vec,
                  pl.BlockSpec((RG_HEADS, RG_BLOCK, 2 * RG_BLOCK), lambda b, s: (0, 0, 0)),
                  vec, vec, vec],
        out_specs=pl.BlockSpec((t, d), lambda b, s: (b * ns + s, 0)),
        out_shape=jax.ShapeDtypeStruct((n, d), BF16),
        scratch_shapes=[pltpu.VMEM((t + SUBLANES, d), F32),
                        pltpu.VMEM((t, d), F32),
                        pltpu.VMEM((t, d), F32),
                        pltpu.VMEM((1, d), F32)],
        compiler_params=_params("arbitrary", "arbitrary"),
        name="rglru_core",
    )(u, u, conv_w, conv_b, wax, b_a, b_x, lam)


def _swiglu_acc(h, wg_ref, wu_ref, wd_ref):
    g = _dot(h, wg_ref[...].astype(BF16))
    u = _dot(h, wu_ref[...].astype(BF16))
    a = (g * _sigmoid(g) * u).astype(BF16)
    return _dot(a, wd_ref[...].astype(BF16))


def _ffn_kernel(x_ref, g_ref, wg_ref, wu_ref, wd_ref, o_ref, hn_ref):
    @pl.when(pl.program_id(1) == 0)
    def _():
        x = x_ref[...]
        hn_ref[...] = _rms(x, g_ref[...]).astype(BF16)
        o_ref[...] = x

    o_ref[...] += _swiglu_acc(hn_ref[...], wg_ref, wu_ref, wd_ref)


def ffn_dense(x, g, wg, wu, wd, *, tm, tc):
    n, d = x.shape
    f = wg.shape[1]
    return pl.pallas_call(
        _ffn_kernel,
        grid=(n // tm, f // tc),
        in_specs=[pl.BlockSpec((tm, d), lambda i, j: (i, 0)),
                  pl.BlockSpec((1, d), lambda i, j: (0, 0)),
                  pl.BlockSpec((d, tc), lambda i, j: (0, j)),
                  pl.BlockSpec((d, tc), lambda i, j: (0, j)),
                  pl.BlockSpec((tc, d), lambda i, j: (j, 0))],
        out_specs=pl.BlockSpec((tm, d), lambda i, j: (i, 0)),
        out_shape=jax.ShapeDtypeStruct((n, d), F32),
        scratch_shapes=[pltpu.VMEM((tm, d), BF16)],
        compiler_params=_params("parallel", "arbitrary"),
        name="ffn_dense",
    )(x, g, wg, wu, wd)


def _mla_proj_kernel(x_ref, pos_ref, g_ref, win_ref, qn_ref, wq_ref, kvn_ref, wkv_ref,
                     freq_ref, sign_ref, qnope_ref, qpe_ref, knope_ref, kpe_ref, v_ref):
    d = x_ref.shape[1]
    hn = _rms(x_ref[...], g_ref[...]).astype(BF16)
    u = _dot(hn, win_ref[...])
    ang = pos_ref[...].astype(F32) * freq_ref[...]
    cos = jnp.cos(ang)
    sin = jnp.sin(ang) * sign_ref[...]

    def rope(xs):
        return xs * cos + pltpu.roll(xs, LANES // 2, axis=1) * sin

    kpe_ref[...] = rope(u[:, Q_LORA + KV_LORA:]).astype(BF16)
    q = _dot(_rms(u[:, :Q_LORA], qn_ref[...]).astype(BF16), wq_ref[...])
    qnope_ref[...] = q[:, :d].astype(BF16)
    for h in range(MLA_HEADS):
        c0, c1 = h * LANES, (h + 1) * LANES
        qpe_ref[:, c0:c1] = rope(q[:, d + c0:d + c1]).astype(BF16)
    kv = _dot(_rms(u[:, Q_LORA:Q_LORA + KV_LORA], kvn_ref[...]).astype(BF16), wkv_ref[...])
    knope_ref[...] = kv[:, :d].astype(BF16)
    v_ref[...] = kv[:, d:].astype(BF16)


def mla_proj(x, pos, g, w_in, q_norm, w_q, kv_norm, w_kv, freq, sign, *, tm):
    n, d = x.shape
    row = lambda w: pl.BlockSpec((tm, w), lambda i: (i, 0))
    full = lambda a: pl.BlockSpec(a.shape, lambda i: (0, 0))
    big = jax.ShapeDtypeStruct((n, d), BF16)
    return pl.pallas_call(
        _mla_proj_kernel,
        grid=(n // tm,),
        in_specs=[row(d), row(1), full(g), full(w_in), full(q_norm), full(w_q), full(kv_norm),
                  full(w_kv), full(freq), full(sign)],
        out_specs=[row(d), row(d), row(d), row(LANES), row(d)],
        out_shape=[big, big, big, jax.ShapeDtypeStruct((n, LANES), BF16), big],
        compiler_params=_params("parallel"),
        name="mla_proj",
    )(x, pos, g, w_in, q_norm, w_q, kv_norm, w_kv, freq, sign)


def _attn_kernel(qn_ref, qp_ref, kn_ref, kp_ref, v_ref, o_ref, m_s, acc_s, bias_s):
    qi = pl.program_id(2)
    tq = qn_ref.shape[0]
    heads = qn_ref.shape[1] // LANES
    c = (QK_DIM ** -0.5) * math.log2(math.e)
    hs = lambda h: slice(h * LANES, (h + 1) * LANES)
    q = [jnp.concatenate([qn_ref[:, hs(h)], qp_ref[:, hs(h)]], axis=1) for h in range(heads)]
    ones = jnp.ones((tq, LANES), BF16)
    m_s[...] = jnp.full_like(m_s, -jnp.inf)
    acc_s[...] = jnp.zeros_like(acc_s)
    q_idx = lax.broadcasted_iota(jnp.int32, (tq, tq), 0)
    k_idx = lax.broadcasted_iota(jnp.int32, (tq, tq), 1)
    bias_s[...] = jnp.where(k_idx <= q_idx, 0.0, -jnp.inf)

    def chunk(kj, masked):
        k0 = pl.multiple_of(kj * tq, tq)
        kp = kp_ref[pl.ds(k0, tq), :]
        for h in range(heads):
            k = jnp.concatenate([kn_ref[pl.ds(k0, tq), hs(h)], kp], axis=1)
            s = lax.dot_general(q[h], k, (((1,), (1,)), ((), ())), preferred_element_type=F32)
            if masked:
                s = s + bias_s[...]
            slabs = [s[:, hs(j)] for j in range(tq // LANES)]
            part = functools.reduce(jnp.maximum, slabs)
            m_prev = m_s[h]
            m_new = jnp.maximum(m_prev, jnp.max(part, axis=1, keepdims=True))
            p = jnp.concatenate([jnp.exp2((sl - m_new) * c) for sl in slabs], axis=1)
            alpha = jnp.exp2((m_prev - m_new) * c)
            vv = jnp.concatenate([v_ref[pl.ds(k0, tq), hs(h)], ones], axis=1)
            acc = acc_s[h]
            acc = jnp.concatenate([acc[:, :LANES] * alpha, acc[:, LANES:] * alpha], axis=1)
            acc_s[h] = acc + _dot(p.astype(BF16), vv)
            m_s[h] = m_new

    def body(kj, carry):
        chunk(kj, False)
        return carry

    lax.fori_loop(0, qi, body, 0)
    chunk(qi, True)
    for h in range(heads):
        acc = acc_s[h]
        o_ref[:, hs(h)] = (acc[:, :LANES] / acc[:, LANES:]).astype(o_ref.dtype)


def attention(q_nope, q_pe, k_nope, k_pe, v, *, batch, seq, tq, heads):
    n, d = q_nope.shape
    nq = seq // tq
    w = heads * LANES
    qspec = pl.BlockSpec((tq, w), lambda b, h, i: (b * nq + i, h))
    kspec = pl.BlockSpec((seq, w), lambda b, h, i: (b, h))
    return pl.pallas_call(
        _attn_kernel,
        grid=(batch, MLA_HEADS // heads, nq),
        in_specs=[qspec, qspec, kspec,
                  pl.BlockSpec((seq, LANES), lambda b, h, i: (b, 0)),
                  kspec],
        out_specs=qspec,
        out_shape=jax.ShapeDtypeStruct((n, d), BF16),
        scratch_shapes=[pltpu.VMEM((heads, tq, LANES), F32),
                        pltpu.VMEM((heads, tq, 2 * LANES), F32),
                        pltpu.VMEM((tq, tq), F32)],
        compiler_params=_params("parallel", "parallel", "arbitrary"),
        name="attention",
    )(q_nope, q_pe, k_nope, k_pe, v)


def _out_router_kernel(o_ref, w_ref, r_ref, g_ref, wr_ref, x_ref, hn_ref, route_ref, cnt_ref, carry):
    i = pl.program_id(0)
    tm = o_ref.shape[0]

    @pl.when(i == 0)
    def _():
        carry[...] = jnp.zeros_like(carry)

    x = r_ref[...] + _dot(o_ref[...], w_ref[...])
    x_ref[...] = x
    hn = _rms(x, g_ref[...])
    hn_ref[...] = hn
    hn_hi = hn.astype(BF16)
    hn_lo = (hn - hn_hi.astype(F32)).astype(BF16)
    parts = _dot(hn_hi, wr_ref[...]) + _dot(hn_lo, wr_ref[...])
    logits = parts[:, :LANES] + parts[:, LANES:]
    lane = lax.broadcasted_iota(jnp.int32, (tm, LANES), 1)
    lane_f = lane.astype(F32)
    neg = -jnp.inf
    logits = jnp.where(lane < N_EXPERTS, logits, neg)
    m1 = jnp.max(logits, axis=1, keepdims=True)
    e1 = jnp.min(jnp.where(logits == m1, lane_f, float(LANES)), axis=1, keepdims=True)
    rest = jnp.where(lane_f == e1, neg, logits)
    m2 = jnp.max(rest, axis=1, keepdims=True)
    e2 = jnp.min(jnp.where(rest == m2, lane_f, float(LANES)), axis=1, keepdims=True)
    z = jnp.exp(m2 - m1)
    g1 = 1.0 / (1.0 + z)
    g2 = z / (1.0 + z)
    sel1 = lane_f == e1
    sel2 = lane_f == e2
    sel = jnp.where(sel1 | sel2, 1.0, 0.0)
    rr = lax.broadcasted_iota(jnp.int32, (tm, tm), 0)
    cc = lax.broadcasted_iota(jnp.int32, (tm, tm), 1)
    tril = jnp.where(cc < rr, 1.0, 0.0).astype(BF16)
    before = _dot(tril, sel.astype(BF16)) + carry[...]
    rank1 = jnp.sum(jnp.where(sel1, before, 0.0), axis=1, keepdims=True)
    rank2 = jnp.sum(jnp.where(sel2, before, 0.0), axis=1, keepdims=True)
    total = carry[...] + jnp.sum(sel, axis=0, keepdims=True)
    carry[...] = total
    cnt_ref[...] = jnp.broadcast_to(total, cnt_ref.shape)
    route = jnp.zeros((tm, LANES), F32)
    for col, val in enumerate((e1, e2, g1, g2, rank1, rank2)):
        route = jnp.where(lane == col, val, route)
    route_ref[...] = route


def out_router(o, w_out, res, g, w_router, *, tm):
    n, d = res.shape
    row = lambda w: pl.BlockSpec((tm, w), lambda i: (i, 0))
    full = lambda a: pl.BlockSpec(a.shape, lambda i: (0, 0))
    return pl.pallas_call(
        _out_router_kernel,
        grid=(n // tm,),
        in_specs=[row(d), full(w_out), row(d), full(g), full(w_router)],
        out_specs=[row(d), row(d), row(LANES), pl.BlockSpec((SUBLANES, LANES), lambda i: (0, 0))],
        out_shape=[jax.ShapeDtypeStruct((n, d), F32), jax.ShapeDtypeStruct((n, d), F32),
                   jax.ShapeDtypeStruct((n, LANES), F32),
                   jax.ShapeDtypeStruct((SUBLANES, LANES), F32)],
        scratch_shapes=[pltpu.VMEM((1, LANES), F32)],
        compiler_params=_params("arbitrary"),
        name="out_router",
    )(o, w_out, res, g, w_router)


GATHER_SLOTS = 2


def _row_copy(src_hbm, row, dst, r, sem):
    return pltpu.make_async_copy(src_hbm.at[pl.ds(row, 1), :], dst.at[pl.ds(r, 1), :], sem)


def _moe_ffn_kernel(src_ref, be_ref, valid_ref, nu_ref, hn_hbm, wg_ref, wu_ref, wd_ref, o_ref, xg, xb, sems,
                    *, per_step):
    i = pl.program_id(0)
    j = pl.program_id(1)
    tm = o_ref.shape[0]
    rows = xg.shape[1]
    n_used = nu_ref[0]
    slot = i % GATHER_SLOTS

    def start_row(blk, sl, r):
        tok = src_ref[blk * tm + jnp.minimum(r, tm - 1)]
        _row_copy(hn_hbm, tok, xg.at[sl], r, sems.at[sl]).start()

    @pl.when((i == 0) & (j == 0))
    def _():
        def first(r, c):
            start_row(0, 0, r)
            return c
        lax.fori_loop(0, rows, first, 0, unroll=8)

    @pl.when((j == 0) & (i <= n_used))
    def _():
        pltpu.make_async_copy(hn_hbm.at[pl.ds(0, rows), :], xg.at[slot], sems.at[slot]).wait()

    @pl.when((j == 0) & (i < n_used))
    def _():
        xb[...] = xg[slot, :tm, :].astype(BF16)

    @pl.when(j == 0)
    def _():
        o_ref[...] = jnp.zeros_like(o_ref)

    def step(m):
        nxt = jnp.minimum(i + 1, n_used - 1)
        for u in range(per_step):
            start_row(nxt, (i + 1) % GATHER_SLOTS, j * per_step + u)
        o_ref[:m, :] += _swiglu_acc(xb[:m, :], wg_ref, wu_ref, wd_ref)

    valid = valid_ref[i]

    @pl.when((i < n_used) & (valid > tm // 2))
    def _():
        step(tm)

    @pl.when((i < n_used) & (valid <= tm // 2))
    def _():
        step(tm // 2)


def moe_ffn(src_tok, block_e, block_valid, n_used, hn, wg, wu, wd, *, tm, tc):
    d = hn.shape[1]
    p = src_tok.shape[0]
    nj = wg.shape[2] // tc
    per_step = -(-tm // nj)
    while (per_step * nj) % SUBLANES:
        per_step += 1

    def blk(i, nu):
        return jnp.minimum(i, nu[0] - 1)

    def col(i, j, nu):
        return jnp.where(i < nu[0], j, nj - 1)

    return pl.pallas_call(
        functools.partial(_moe_ffn_kernel, per_step=per_step),
        grid_spec=pltpu.PrefetchScalarGridSpec(
            num_scalar_prefetch=4,
            grid=(p // tm, nj),
            in_specs=[pl.BlockSpec(memory_space=pl.ANY),
                      pl.BlockSpec((None, d, tc), lambda i, j, s, be, bv, nu: (be[blk(i, nu)], 0, col(i, j, nu))),
                      pl.BlockSpec((None, d, tc), lambda i, j, s, be, bv, nu: (be[blk(i, nu)], 0, col(i, j, nu))),
                      pl.BlockSpec((None, tc, d), lambda i, j, s, be, bv, nu: (be[blk(i, nu)], col(i, j, nu), 0))],
            out_specs=pl.BlockSpec((tm, d), lambda i, j, s, be, bv, nu: (i, 0)),
            scratch_shapes=[pltpu.VMEM((GATHER_SLOTS, per_step * nj, d), F32), pltpu.VMEM((tm, d), BF16),
                            pltpu.SemaphoreType.DMA((GATHER_SLOTS,))]),
        out_shape=jax.ShapeDtypeStruct((p, d), F32),
        compiler_params=_params("arbitrary", "arbitrary"),
        name="moe_ffn",
    )(src_tok, block_e, block_valid, n_used, hn, wg, wu, wd)


def _combine_kernel(dest_ref, x_ref, route_ref, g_ref, y_hbm, o_ref, ya, yb, sems):
    i = pl.program_id(0)
    tm = o_ref.shape[0]
    slot = i % 2

    def copies(blk, sl, r):
        t2 = 2 * (blk * tm + r)
        return (_row_copy(y_hbm, dest_ref[t2], ya.at[sl], r, sems.at[0, sl]),
                _row_copy(y_hbm, dest_ref[t2 + 1], yb.at[sl], r, sems.at[1, sl]))

    def issue(blk, sl):
        def body(r, c):
            ca, cb = copies(blk, sl, r)
            ca.start()
            cb.start()
            return c
        lax.fori_loop(0, tm, body, 0, unroll=8)

    @pl.when(i == 0)
    def _():
        issue(0, 0)

    @pl.when(i + 1 < pl.num_programs(0))
    def _():
        issue(i + 1, 1 - slot)

    pltpu.make_async_copy(y_hbm.at[pl.ds(0, tm), :], ya.at[slot], sems.at[0, slot]).wait()
    pltpu.make_async_copy(y_hbm.at[pl.ds(0, tm), :], yb.at[slot], sems.at[1, slot]).wait()
    route = route_ref[...]
    x = x_ref[...] + (ya[slot] * route[:, 2:3] + yb[slot] * route[:, 3:4])
    o_ref[...] = _rms(x, g_ref[...])


def combine(dest, x, route, g, y_buf, *, tm):
    n, d = x.shape
    return pl.pallas_call(
        _combine_kernel,
        grid_spec=pltpu.PrefetchScalarGridSpec(
            num_scalar_prefetch=1,
            grid=(n // tm,),
            in_specs=[pl.BlockSpec((tm, d), lambda i, dst: (i, 0)),
                      pl.BlockSpec((tm, LANES), lambda i, dst: (i, 0)),
                      pl.BlockSpec((1, d), lambda i, dst: (0, 0)),
                      pl.BlockSpec(memory_space=pl.ANY)],
            out_specs=pl.BlockSpec((tm, d), lambda i, dst: (i, 0)),
            scratch_shapes=[pltpu.VMEM((2, tm, d), F32), pltpu.VMEM((2, tm, d), F32),
                            pltpu.SemaphoreType.DMA((2, 2))]),
        out_shape=jax.ShapeDtypeStruct((n, d), F32),
        compiler_params=_params("arbitrary"),
        name="moe_combine",
    )(dest, x, route, g, y_buf)


def _spread_rope(w):
    half = QK_ROPE // 2
    z = jnp.zeros(w.shape[:-1] + (half,), w.dtype)
    return jnp.concatenate([w[..., :half], z, w[..., half:], z], axis=-1)


def _tile(n, want):
    t = min(n, want)
    assert n % t == 0, (n, t)
    return t


def kernel(x, positions, a_norm_mix, rg_w_in, rg_conv_w, rg_conv_b, rg_w_a, rg_b_a, rg_w_x, rg_b_x, rg_lambda, rg_w_out, a_norm_ffn, ff_w_gate, ff_w_up, ff_w_down, b_norm_mix, mla_w_in, mla_q_norm, mla_w_q_up, mla_kv_norm, mla_w_kv_up, mla_w_out, b_norm_ffn, moe_w_router, moe_w_gate, moe_w_up, moe_w_down, final_norm):
    batch, seq, d = x.shape
    n = batch * seq
    assert a_norm_mix.shape[0] == 1 and b_norm_mix.shape[0] == 1
    xf = x.reshape(n, d)
    vec = lambda a: a.reshape(1, -1).astype(F32)

    u = norm_mm(xf, vec(a_norm_mix[0]), rg_w_in[0].astype(BF16),
                tm=_tile(n, 1024), tn=1024, out_dtype=BF16)
    wax = jnp.concatenate([rg_w_a[0], rg_w_x[0]], axis=-1).astype(BF16)
    y = rglru_core(u, rg_conv_w[0], vec(rg_conv_b[0]), wax, vec(rg_b_a[0]), vec(rg_b_x[0]),
                   vec(rg_lambda[0]), batch=batch, seq=seq, t=_tile(seq, 256))
    x1 = mm_res(y, rg_w_out[0].astype(BF16), xf, tm=_tile(n, 1024), tn=1024)
    x2 = ffn_dense(x1, vec(a_norm_ffn[0]), ff_w_gate[0], ff_w_up[0], ff_w_down[0],
                   tm=_tile(n, FF_TM), tc=FF_TC)

    w_in = mla_w_in[0]
    w_in_p = jnp.concatenate([w_in[:, :Q_LORA + KV_LORA], _spread_rope(w_in[:, Q_LORA + KV_LORA:])],
                             axis=1).astype(BF16)
    wq = mla_w_q_up[0].reshape(Q_LORA, MLA_HEADS, QK_DIM)
    wq_p = jnp.concatenate([wq[..., :QK_NOPE].reshape(Q_LORA, -1),
                            _spread_rope(wq[..., QK_NOPE:]).reshape(Q_LORA, -1)], axis=1).astype(BF16)
    wkv = mla_w_kv_up[0].reshape(KV_LORA, MLA_HEADS, QK_NOPE + V_DIM)
    wkv_p = jnp.concatenate([wkv[..., :QK_NOPE].reshape(KV_LORA, -1),
                             wkv[..., QK_NOPE:].reshape(KV_LORA, -1)], axis=1).astype(BF16)
    inv_freq = 1.0 / (ROPE_THETA ** (jnp.arange(0, QK_ROPE, 2, dtype=F32) / QK_ROPE))
    freq = _spread_rope(jnp.concatenate([inv_freq, inv_freq])[None, :])
    ones = jnp.ones((1, QK_ROPE // 2), F32)
    sign = _spread_rope(jnp.concatenate([-ones, ones], axis=1))
    pos = positions.reshape(n, 1).astype(jnp.int32)
    q_nope, q_pe, k_nope, k_pe, v = mla_proj(
        x2, pos, vec(b_norm_mix[0]), w_in_p, vec(mla_q_norm[0]), wq_p, vec(mla_kv_norm[0]), wkv_p,
        freq, sign, tm=_tile(n, 512))
    o = attention(q_nope, q_pe, k_nope, k_pe, v, batch=batch, seq=seq, tq=_tile(seq, 512), heads=8)
    w_router = jnp.pad(moe_w_router[0].astype(F32), ((0, 0), (0, LANES - N_EXPERTS)))
    wr_hi = w_router.astype(BF16)
    w_router = jnp.concatenate([wr_hi, (w_router - wr_hi.astype(F32)).astype(BF16)], axis=1)
    x3, hn, route, cnt = out_router(o, mla_w_out[0].astype(BF16), x2, vec(b_norm_ffn[0]), w_router,
                                    tm=_tile(n, 512))

    tm = _tile(2 * n, FF_TM)
    counts = cnt[0, :N_EXPERTS].astype(jnp.int32)
    padded = ((counts + tm - 1) // tm) * tm
    pad_end = jnp.cumsum(padded)
    pad_start = pad_end - padded
    experts = route[:, 0:2].astype(jnp.int32)
    ranks = route[:, 4:6].astype(jnp.int32)
    dest = (pad_start[experts] + ranks).reshape(-1)
    p = 2 * n + N_EXPERTS * tm
    n_blocks = p // tm
    src_tok = jnp.zeros((p,), jnp.int32).at[dest].set(jnp.arange(2 * n, dtype=jnp.int32) // 2)
    block_start = jnp.arange(n_blocks, dtype=jnp.int32) * tm
    block_e = jnp.minimum(jnp.sum(block_start[:, None] >= pad_end[None, :], axis=1),
                          N_EXPERTS - 1).astype(jnp.int32)
    block_valid = jnp.clip((pad_start + counts)[block_e] - block_start, 0, tm).astype(jnp.int32)
    n_used = (pad_end[-1:] // tm).astype(jnp.int32)

    y_buf = moe_ffn(src_tok, block_e, block_valid, n_used, hn, moe_w_gate[0], moe_w_up[0], moe_w_down[0],
                    tm=tm, tc=FF_TC)
    out = combine(dest, x3, route, vec(final_norm), y_buf, tm=_tile(n, 256))
    return out.reshape(batch, seq, d)
```

```python
import functools
import math

import jax
import jax.numpy as jnp
from jax import lax
from jax.experimental import pallas as pl
from jax.experimental.pallas import tpu as pltpu

EPS = 1e-6
RG_HEADS = 16
RG_BLOCK = 128
CONV_W = 4
RG_C = 8.0
MLA_HEADS = 16
Q_LORA = 512
KV_LORA = 512
QK_NOPE = 128
QK_ROPE = 64
V_DIM = 128
QK_DIM = QK_NOPE + QK_ROPE
ROPE_THETA = 10000.0
N_EXPERTS = 8
LANES = 128
SUBLANES = 8
VMEM_LIMIT_BYTES = 56 * 1024 * 1024
FF_TC = 256
FF_TM = 1024

BF16 = jnp.bfloat16
F32 = jnp.float32


def _params(*sem):
    return pltpu.CompilerParams(dimension_semantics=sem, vmem_limit_bytes=VMEM_LIMIT_BYTES)


def _rms(x, g):
    return x * lax.rsqrt(jnp.mean(x * x, axis=-1, keepdims=True) + EPS) * g


def _dot(a, b):
    return jnp.dot(a, b, preferred_element_type=F32)


def _sigmoid(x):
    return 1.0 / (1.0 + jnp.exp(-x))


def _sigmoid_tanh(x):
    return 0.5 * jnp.tanh(0.5 * x) + 0.5


def _norm_mm_kernel(x_ref, g_ref, w_ref, o_ref, hn_ref):
    @pl.when(pl.program_id(1) == 0)
    def _():
        hn_ref[...] = _rms(x_ref[...], g_ref[...]).astype(BF16)

    o_ref[...] = _dot(hn_ref[...], w_ref[...]).astype(o_ref.dtype)


def norm_mm(x, g, w, *, tm, tn, out_dtype):
    n, d = x.shape
    f = w.shape[1]
    return pl.pallas_call(
        _norm_mm_kernel,
        grid=(n // tm, f // tn),
        in_specs=[pl.BlockSpec((tm, d), lambda i, j: (i, 0)),
                  pl.BlockSpec((1, d), lambda i, j: (0, 0)),
                  pl.BlockSpec((d, tn), lambda i, j: (0, j))],
        out_specs=pl.BlockSpec((tm, tn), lambda i, j: (i, j)),
        out_shape=jax.ShapeDtypeStruct((n, f), out_dtype),
        scratch_shapes=[pltpu.VMEM((tm, d), BF16)],
        compiler_params=_params("parallel", "arbitrary"),
        name="norm_mm",
    )(x, g, w)


def _mm_res_kernel(y_ref, w_ref, r_ref, o_ref):
    o_ref[...] = r_ref[...] + _dot(y_ref[...], w_ref[...])


def mm_res(y, w, res, *, tm, tn):
    n, k = y.shape
    f = w.shape[1]
    return pl.pallas_call(
        _mm_res_kernel,
        grid=(n // tm, f // tn),
        in_specs=[pl.BlockSpec((tm, k), lambda i, j: (i, 0)),
                  pl.BlockSpec((k, tn), lambda i, j: (0, j)),
                  pl.BlockSpec((tm, tn), lambda i, j: (i, j))],
        out_specs=pl.BlockSpec((tm, tn), lambda i, j: (i, j)),
        out_shape=jax.ShapeDtypeStruct((n, f), F32),
        compiler_params=_params("parallel", "parallel"),
        name="mm_res",
    )(y, w, res)


def _gelu_tanh(x):
    c = math.sqrt(2.0 / math.pi)
    return 0.5 * x * (1.0 + jnp.tanh(c * (x + 0.044715 * (x * x * x))))


def _softplus(z):
    return jnp.maximum(z, 0.0) + jnp.log1p(jnp.exp(-jnp.abs(z)))


def _rglru_kernel(gate_ref, rec_ref, cw_ref, cb_ref, wax_ref, ba_ref, bx_ref, lam_ref,
                  y_ref, xbuf, a_s, b_s, h_s):
    s = pl.program_id(1)
    t = rec_ref.shape[0]
    d = rec_ref.shape[1]

    @pl.when(s == 0)
    def _():
        xbuf[0:SUBLANES, :] = jnp.zeros((SUBLANES, d), F32)
        h_s[...] = jnp.zeros_like(h_s)

    @pl.when(s > 0)
    def _():
        xbuf[0:SUBLANES, :] = xbuf[t:t + SUBLANES, :]

    xbuf[SUBLANES:t + SUBLANES, :] = rec_ref[...].astype(F32)

    row = lax.broadcasted_iota(jnp.int32, (t // SUBLANES, SUBLANES, RG_BLOCK), 1)
    for h in range(RG_HEADS):
        c0, c1 = h * RG_BLOCK, (h + 1) * RG_BLOCK
        xc = cb_ref[:, c0:c1] + jnp.zeros((t, RG_BLOCK), F32)
        for k in range(CONV_W):
            sh = CONV_W - 1 - k
            xc = xc + xbuf[SUBLANES - sh:SUBLANES - sh + t, c0:c1] * cw_ref[k:k + 1, c0:c1]
        gg = _dot(xc.astype(BF16), wax_ref[h])
        r = _sigmoid_tanh(gg[:, :RG_BLOCK] + ba_ref[:, c0:c1])
        i = _sigmoid_tanh(gg[:, RG_BLOCK:] + bx_ref[:, c0:c1])
        log_a = -RG_C * r * _softplus(-lam_ref[:, c0:c1])
        a = jnp.exp(log_a)
        om = 1.0 - a * a
        b = jnp.where(om > 0.0, om * lax.rsqrt(om), 0.0) * i * xc
        a = a.reshape(t // SUBLANES, SUBLANES, RG_BLOCK)
        b = b.reshape(t // SUBLANES, SUBLANES, RG_BLOCK)
        for sft in (1, 2, 4):
            a_sh = pltpu.roll(a, sft, axis=1)
            b_sh = pltpu.roll(b, sft, axis=1)
            m = row >= sft
            b = jnp.where(m, a * b_sh + b, b)
            a = jnp.where(m, a * a_sh, a)
        a_s[:, c0:c1] = a.reshape(t, RG_BLOCK)
        b_s[:, c0:c1] = b.reshape(t, RG_BLOCK)

    def group(gi, hc):
        r0 = pl.multiple_of(gi * SUBLANES, SUBLANES)
        rows = b_s[pl.ds(r0, SUBLANES), :] + a_s[pl.ds(r0, SUBLANES), :] * hc
        b_s[pl.ds(r0, SUBLANES), :] = rows
        return rows[SUBLANES - 1:SUBLANES, :]

    h_s[...] = lax.fori_loop(0, t // SUBLANES, group, h_s[...])
    y_ref[...] = (b_s[...] * _gelu_tanh(gate_ref[...].astype(F32))).astype(y_ref.dtype)


def rglru_core(u, conv_w, conv_b, wax, b_a, b_x, lam, *, batch, seq, t):
    n = u.shape[0]
    d = u.shape[1] // 2
    ns = seq // t
    vec = pl.BlockSpec((1, d), lambda b, s: (0, 0))
    return pl.pallas_call(
        _rglru_kernel,
        grid=(batch, ns),
        in_specs=[pl.BlockSpec((t, d), lambda b, s: (b * ns + s, 0)),
                  pl.BlockSpec((t, d), lambda b, s: (b * ns + s, 1)),
                  pl.BlockSpec((CONV_W, d), lambda b, s: (0, 0)),
                  vec,
                  pl.BlockSpec((RG_HEADS, RG_BLOCK, 2 * RG_BLOCK), lambda b, s: (0, 0, 0)),
                  vec, vec, vec],
        out_specs=pl.BlockSpec((t, d), lambda b, s: (b * ns + s, 0)),
        out_shape=jax.ShapeDtypeStruct((n, d), BF16),
        scratch_shapes=[pltpu.VMEM((t + SUBLANES, d), F32),
                        pltpu.VMEM((t, d), F32),
                        pltpu.VMEM((t, d), F32),
                        pltpu.VMEM((1, d), F32)],
        compiler_params=_params("arbitrary", "arbitrary"),
        name="rglru_core",
    )(u, u, conv_w, conv_b, wax, b_a, b_x, lam)


def _swiglu_acc(h, wg_ref, wu_ref, wd_ref):
    g = _dot(h, wg_ref[...].astype(BF16))
    u = _dot(h, wu_ref[...].astype(BF16))
    a = (g * _sigmoid(g) * u).astype(BF16)
    return _dot(a, wd_ref[...].astype(BF16))


def _ffn_kernel(x_ref, g_ref, wg_ref, wu_ref, wd_ref, o_ref, hn_ref):
    @pl.when(pl.program_id(1) == 0)
    def _():
        x = x_ref[...]
        hn_ref[...] = _rms(x, g_ref[...]).astype(BF16)
        o_ref[...] = x

    o_ref[...] += _swiglu_acc(hn_ref[...], wg_ref, wu_ref, wd_ref)


def ffn_dense(x, g, wg, wu, wd, *, tm, tc):
    n, d = x.shape
    f = wg.shape[1]
    return pl.pallas_call(
        _ffn_kernel,
        grid=(n // tm, f // tc),
        in_specs=[pl.BlockSpec((tm, d), lambda i, j: (i, 0)),
                  pl.BlockSpec((1, d), lambda i, j: (0, 0)),
                  pl.BlockSpec((d, tc), lambda i, j: (0, j)),
                  pl.BlockSpec((d, tc), lambda i, j: (0, j)),
                  pl.BlockSpec((tc, d), lambda i, j: (j, 0))],
        out_specs=pl.BlockSpec((tm, d), lambda i, j: (i, 0)),
        out_shape=jax.ShapeDtypeStruct((n, d), F32),
        scratch_shapes=[pltpu.VMEM((tm, d), BF16)],
        compiler_params=_params("parallel", "arbitrary"),
        name="ffn_dense",
    )(x, g, wg, wu, wd)


def _mla_proj_kernel(x_ref, pos_ref, g_ref, win_ref, qn_ref, wq_ref, kvn_ref, wkv_ref,
                     freq_ref, sign_ref, qnope_ref, qpe_ref, knope_ref, kpe_ref, v_ref):
    d = x_ref.shape[1]
    hn = _rms(x_ref[...], g_ref[...]).astype(BF16)
    u = _dot(hn, win_ref[...])
    ang = pos_ref[...].astype(F32) * freq_ref[...]
    cos = jnp.cos(ang)
    sin = jnp.sin(ang) * sign_ref[...]

    def rope(xs):
        return xs * cos + pltpu.roll(xs, LANES // 2, axis=1) * sin

    kpe_ref[...] = rope(u[:, Q_LORA + KV_LORA:]).astype(BF16)
    q = _dot(_rms(u[:, :Q_LORA], qn_ref[...]).astype(BF16), wq_ref[...])
    qnope_ref[...] = q[:, :d].astype(BF16)
    for h in range(MLA_HEADS):
        c0, c1 = h * LANES, (h + 1) * LANES
        qpe_ref[:, c0:c1] = rope(q[:, d + c0:d + c1]).astype(BF16)
    kv = _dot(_rms(u[:, Q_LORA:Q_LORA + KV_LORA], kvn_ref[...]).astype(BF16), wkv_ref[...])
    knope_ref[...] = kv[:, :d].astype(BF16)
    v_ref[...] = kv[:, d:].astype(BF16)


def mla_proj(x, pos, g, w_in, q_norm, w_q, kv_norm, w_kv, freq, sign, *, tm):
    n, d = x.shape
    row = lambda w: pl.BlockSpec((tm, w), lambda i: (i, 0))
    full = lambda a: pl.BlockSpec(a.shape, lambda i: (0, 0))
    big = jax.ShapeDtypeStruct((n, d), BF16)
    return pl.pallas_call(
        _mla_proj_kernel,
        grid=(n // tm,),
        in_specs=[row(d), row(1), full(g), full(w_in), full(q_norm), full(w_q), full(kv_norm),
                  full(w_kv), full(freq), full(sign)],
        out_specs=[row(d), row(d), row(d), row(LANES), row(d)],
        out_shape=[big, big, big, jax.ShapeDtypeStruct((n, LANES), BF16), big],
        compiler_params=_params("parallel"),
        name="mla_proj",
    )(x, pos, g, w_in, q_norm, w_q, kv_norm, w_kv, freq, sign)


def _attn_kernel(qn_ref, qp_ref, kn_ref, kp_ref, v_ref, o_ref, m_s, acc_s, bias_s):
    qi = pl.program_id(2)
    tq = qn_ref.shape[0]
    heads = qn_ref.shape[1] // LANES
    c = (QK_DIM ** -0.5) * math.log2(math.e)
    hs = lambda h: slice(h * LANES, (h + 1) * LANES)
    q = [jnp.concatenate([qn_ref[:, hs(h)], qp_ref[:, hs(h)]], axis=1) for h in range(heads)]
    ones = jnp.ones((tq, LANES), BF16)
    m_s[...] = jnp.full_like(m_s, -jnp.inf)
    acc_s[...] = jnp.zeros_like(acc_s)
    q_idx = lax.broadcasted_iota(jnp.int32, (tq, tq), 0)
    k_idx = lax.broadcasted_iota(jnp.int32, (tq, tq), 1)
    bias_s[...] = jnp.where(k_idx <= q_idx, 0.0, -jnp.inf)

    def chunk(kj, masked):
        k0 = pl.multiple_of(kj * tq, tq)
        kp = kp_ref[pl.ds(k0, tq), :]
        for h in range(heads):
            k = jnp.concatenate([kn_ref[pl.ds(k0, tq), hs(h)], kp], axis=1)
            s = lax.dot_general(q[h], k, (((1,), (1,)), ((), ())), preferred_element_type=F32)
            if masked:
                s = s + bias_s[...]
            slabs = [s[:, hs(j)] for j in range(tq // LANES)]
            part = functools.reduce(jnp.maximum, slabs)
            m_prev = m_s[h]
            m_new = jnp.maximum(m_prev, jnp.max(part, axis=1, keepdims=True))
            p = jnp.concatenate([jnp.exp2((sl - m_new) * c) for sl in slabs], axis=1)
            alpha = jnp.exp2((m_prev - m_new) * c)
            vv = jnp.concatenate([v_ref[pl.ds(k0, tq), hs(h)], ones], axis=1)
            acc = acc_s[h]
            acc = jnp.concatenate([acc[:, :LANES] * alpha, acc[:, LANES:] * alpha], axis=1)
            acc_s[h] = acc + _dot(p.astype(BF16), vv)
            m_s[h] = m_new

    def body(kj, carry):
        chunk(kj, False)
        return carry

    lax.fori_loop(0, qi, body, 0)
    chunk(qi, True)
    for h in range(heads):
        acc = acc_s[h]
        o_ref[:, hs(h)] = (acc[:, :LANES] / acc[:, LANES:]).astype(o_ref.dtype)


def attention(q_nope, q_pe, k_nope, k_pe, v, *, batch, seq, tq, heads):
    n, d = q_nope.shape
    nq = seq // tq
    w = heads * LANES
    qspec = pl.BlockSpec((tq, w), lambda b, h, i: (b * nq + i, h))
    kspec = pl.BlockSpec((seq, w), lambda b, h, i: (b, h))
    return pl.pallas_call(
        _attn_kernel,
        grid=(batch, MLA_HEADS // heads, nq),
        in_specs=[qspec, qspec, kspec,
                  pl.BlockSpec((seq, LANES), lambda b, h, i: (b, 0)),
                  kspec],
        out_specs=qspec,
        out_shape=jax.ShapeDtypeStruct((n, d), BF16),
        scratch_shapes=[pltpu.VMEM((heads, tq, LANES), F32),
                        pltpu.VMEM((heads, tq, 2 * LANES), F32),
                        pltpu.VMEM((tq, tq), F32)],
        compiler_params=_params("parallel", "parallel", "arbitrary"),
        name="attention",
    )(q_nope, q_pe, k_nope, k_pe, v)


def _out_router_kernel(o_ref, w_ref, r_ref, g_ref, wr_ref, x_ref, hn_ref, route_ref, cnt_ref, carry):
    i = pl.program_id(0)
    tm = o_ref.shape[0]

    @pl.when(i == 0)
    def _():
        carry[...] = jnp.zeros_like(carry)

    x = r_ref[...] + _dot(o_ref[...], w_ref[...])
    x_ref[...] = x
    hn = _rms(x, g_ref[...])
    hn_ref[...] = hn
    hn_hi = hn.astype(BF16)
    hn_lo = (hn - hn_hi.astype(F32)).astype(BF16)
    parts = _dot(hn_hi, wr_ref[...]) + _dot(hn_lo, wr_ref[...])
    logits = parts[:, :LANES] + parts[:, LANES:]
    lane = lax.broadcasted_iota(jnp.int32, (tm, LANES), 1)
    lane_f = lane.astype(F32)
    neg = -jnp.inf
    logits = jnp.where(lane < N_EXPERTS, logits, neg)
    m1 = jnp.max(logits, axis=1, keepdims=True)
    e1 = jnp.min(jnp.where(logits == m1, lane_f, float(LANES)), axis=1, keepdims=True)
    rest = jnp.where(lane_f == e1, neg, logits)
    m2 = jnp.max(rest, axis=1, keepdims=True)
    e2 = jnp.min(jnp.where(rest == m2, lane_f, float(LANES)), axis=1, keepdims=True)
    z = jnp.exp(m2 - m1)
    g1 = 1.0 / (1.0 + z)
    g2 = z / (1.0 + z)
    sel1 = lane_f == e1
    sel2 = lane_f == e2
    sel = jnp.where(sel1 | sel2, 1.0, 0.0)
    rr = lax.broadcasted_iota(jnp.int32, (tm, tm), 0)
    cc = lax.broadcasted_iota(jnp.int32, (tm, tm), 1)
    tril = jnp.where(cc < rr, 1.0, 0.0).astype(BF16)
    before = _dot(tril, sel.astype(BF16)) + carry[...]
    rank1 = jnp.sum(jnp.where(sel1, before, 0.0), axis=1, keepdims=True)
    rank2 = jnp.sum(jnp.where(sel2, before, 0.0), axis=1, keepdims=True)
    total = carry[...] + jnp.sum(sel, axis=0, keepdims=True)
    carry[...] = total
    cnt_ref[...] = jnp.broadcast_to(total, cnt_ref.shape)
    route = jnp.zeros((tm, LANES), F32)
    for col, val in enumerate((e1, e2, g1, g2, rank1, rank2)):
        route = jnp.where(lane == col, val, route)
    route_ref[...] = route


def out_router(o, w_out, res, g, w_router, *, tm):
    n, d = res.shape
    row = lambda w: pl.BlockSpec((tm, w), lambda i: (i, 0))
    full = lambda a: pl.BlockSpec(a.shape, lambda i: (0, 0))
    return pl.pallas_call(
        _out_router_kernel,
        grid=(n // tm,),
        in_specs=[row(d), full(w_out), row(d), full(g), full(w_router)],
        out_specs=[row(d), row(d), row(LANES), pl.BlockSpec((SUBLANES, LANES), lambda i: (0, 0))],
        out_shape=[jax.ShapeDtypeStruct((n, d), F32), jax.ShapeDtypeStruct((n, d), F32),
                   jax.ShapeDtypeStruct((n, LANES), F32),
                   jax.ShapeDtypeStruct((SUBLANES, LANES), F32)],
        scratch_shapes=[pltpu.VMEM((1, LANES), F32)],
        compiler_params=_params("arbitrary"),
        name="out_router",
    )(o, w_out, res, g, w_router)


GATHER_SLOTS = 2
GATHER_PRIORITY = 1
MOE_ROW_PARTS = 4


def _row_copy(src_hbm, row, dst, r, sem):
    return pltpu.make_async_copy(src_hbm.at[pl.ds(row, 1), :], dst.at[pl.ds(r, 1), :], sem)


def _moe_ffn_kernel(src_ref, be_ref, valid_ref, nu_ref, hn_hbm, wg_ref, wu_ref, wd_ref, o_ref, xg, xb, sems,
                    *, per_step):
    i = pl.program_id(0)
    j = pl.program_id(1)
    tm = o_ref.shape[0]
    rows = xg.shape[1]
    n_used = nu_ref[0]
    slot = i % GATHER_SLOTS

    def start_row(blk, sl, r):
        tok = src_ref[blk * tm + jnp.minimum(r, tm - 1)]
        _row_copy(hn_hbm, tok, xg.at[sl], r, sems.at[sl]).start(priority=GATHER_PRIORITY)

    @pl.when((i == 0) & (j == 0))
    def _():
        def first(r, c):
            start_row(0, 0, r)
            return c
        lax.fori_loop(0, rows, first, 0, unroll=8)

    @pl.when((j == 0) & (i <= n_used))
    def _():
        pltpu.make_async_copy(hn_hbm.at[pl.ds(0, rows), :], xg.at[slot], sems.at[slot]).wait()

    @pl.when((j == 0) & (i < n_used))
    def _():
        xb[...] = xg[slot, :tm, :].astype(BF16)

    @pl.when(j == 0)
    def _():
        o_ref[...] = jnp.zeros_like(o_ref)

    def step(m):
        nxt = jnp.minimum(i + 1, n_used - 1)
        for u in range(per_step):
            start_row(nxt, (i + 1) % GATHER_SLOTS, j * per_step + u)
        o_ref[:m, :] += _swiglu_acc(xb[:m, :], wg_ref, wu_ref, wd_ref)

    quarter = tm // MOE_ROW_PARTS
    parts = (valid_ref[i] + quarter - 1) // quarter
    for k in range(1, MOE_ROW_PARTS + 1):
        @pl.when((i < n_used) & (parts == k))
        def _(k=k):
            step(k * quarter)


def moe_ffn(src_tok, block_e, block_valid, n_used, hn, wg, wu, wd, *, tm, tc):
    d = hn.shape[1]
    p = src_tok.shape[0]
    nj = wg.shape[2] // tc
    per_step = -(-tm // nj)
    while (per_step * nj) % SUBLANES:
        per_step += 1

    def blk(i, nu):
        return jnp.minimum(i, nu[0] - 1)

    def col(i, j, nu):
        return jnp.where(i < nu[0], j, nj - 1)

    return pl.pallas_call(
        functools.partial(_moe_ffn_kernel, per_step=per_step),
        grid_spec=pltpu.PrefetchScalarGridSpec(
            num_scalar_prefetch=4,
            grid=(p // tm, nj),
            in_specs=[pl.BlockSpec(memory_space=pl.ANY),
                      pl.BlockSpec((None, d, tc), lambda i, j, s, be, bv, nu: (be[blk(i, nu)], 0, col(i, j, nu))),
                      pl.BlockSpec((None, d, tc), lambda i, j, s, be, bv, nu: (be[blk(i, nu)], 0, col(i, j, nu))),
                      pl.BlockSpec((None, tc, d), lambda i, j, s, be, bv, nu: (be[blk(i, nu)], col(i, j, nu), 0))],
            out_specs=pl.BlockSpec((tm, d), lambda i, j, s, be, bv, nu: (i, 0)),
            scratch_shapes=[pltpu.VMEM((GATHER_SLOTS, per_step * nj, d), F32), pltpu.VMEM((tm, d), BF16),
                            pltpu.SemaphoreType.DMA((GATHER_SLOTS,))]),
        out_shape=jax.ShapeDtypeStruct((p, d), F32),
        compiler_params=_params("arbitrary", "arbitrary"),
        name="moe_ffn",
    )(src_tok, block_e, block_valid, n_used, hn, wg, wu, wd)


def _combine_kernel(dest_ref, x_ref, route_ref, g_ref, y_hbm, o_ref, ya, yb, sems):
    i = pl.program_id(0)
    tm = o_ref.shape[0]
    slot = i % 2

    def copies(blk, sl, r):
        t2 = 2 * (blk * tm + r)
        return (_row_copy(y_hbm, dest_ref[t2], ya.at[sl], r, sems.at[0, sl]),
                _row_copy(y_hbm, dest_ref[t2 + 1], yb.at[sl], r, sems.at[1, sl]))

    def issue(blk, sl):
        def body(r, c):
            ca, cb = copies(blk, sl, r)
            ca.start(priority=0)
            cb.start(priority=1)
            return c
        lax.fori_loop(0, tm, body, 0, unroll=8)

    @pl.when(i == 0)
    def _():
        issue(0, 0)

    @pl.when(i + 1 < pl.num_programs(0))
    def _():
        issue(i + 1, 1 - slot)

    pltpu.make_async_copy(y_hbm.at[pl.ds(0, tm), :], ya.at[slot], sems.at[0, slot]).wait()
    pltpu.make_async_copy(y_hbm.at[pl.ds(0, tm), :], yb.at[slot], sems.at[1, slot]).wait()
    route = route_ref[...]
    x = x_ref[...] + (ya[slot] * route[:, 2:3] + yb[slot] * route[:, 3:4])
    o_ref[...] = _rms(x, g_ref[...])


def combine(dest, x, route, g, y_buf, *, tm):
    n, d = x.shape
    return pl.pallas_call(
        _combine_kernel,
        grid_spec=pltpu.PrefetchScalarGridSpec(
            num_scalar_prefetch=1,
            grid=(n // tm,),
            in_specs=[pl.BlockSpec((tm, d), lambda i, dst: (i, 0)),
                      pl.BlockSpec((tm, LANES), lambda i, dst: (i, 0)),
                      pl.BlockSpec((1, d), lambda i, dst: (0, 0)),
                      pl.BlockSpec(memory_space=pl.ANY)],
            out_specs=pl.BlockSpec((tm, d), lambda i, dst: (i, 0)),
            scratch_shapes=[pltpu.VMEM((2, tm, d), F32), pltpu.VMEM((2, tm, d), F32),
                            pltpu.SemaphoreType.DMA((2, 2))]),
        out_shape=jax.ShapeDtypeStruct((n, d), F32),
        compiler_params=_params("arbitrary"),
        name="moe_combine",
    )(dest, x, route, g, y_buf)


def _spread_rope(w):
    half = QK_ROPE // 2
    z = jnp.zeros(w.shape[:-1] + (half,), w.dtype)
    return jnp.concatenate([w[..., :half], z, w[..., half:], z], axis=-1)


def _tile(n, want):
    t = min(n, want)
    assert n % t == 0, (n, t)
    return t


def kernel(x, positions, a_norm_mix, rg_w_in, rg_conv_w, rg_conv_b, rg_w_a, rg_b_a, rg_w_x, rg_b_x, rg_lambda, rg_w_out, a_norm_ffn, ff_w_gate, ff_w_up, ff_w_down, b_norm_mix, mla_w_in, mla_q_norm, mla_w_q_up, mla_kv_norm, mla_w_kv_up, mla_w_out, b_norm_ffn, moe_w_router, moe_w_gate, moe_w_up, moe_w_down, final_norm):
    batch, seq, d = x.shape
    n = batch * seq
    assert a_norm_mix.shape[0] == 1 and b_norm_mix.shape[0] == 1
    xf = x.reshape(n, d)
    vec = lambda a: a.reshape(1, -1).astype(F32)

    u = norm_mm(xf, vec(a_norm_mix[0]), rg_w_in[0].astype(BF16),
                tm=_tile(n, 1024), tn=1024, out_dtype=BF16)
    wax = jnp.concatenate([rg_w_a[0], rg_w_x[0]], axis=-1).astype(BF16)
    y = rglru_core(u, rg_conv_w[0], vec(rg_conv_b[0]), wax, vec(rg_b_a[0]), vec(rg_b_x[0]),
                   vec(rg_lambda[0]), batch=batch, seq=seq, t=_tile(seq, 256))
    x1 = mm_res(y, rg_w_out[0].astype(BF16), xf, tm=_tile(n, 1024), tn=1024)
    x2 = ffn_dense(x1, vec(a_norm_ffn[0]), ff_w_gate[0], ff_w_up[0], ff_w_down[0],
                   tm=_tile(n, FF_TM), tc=FF_TC)

    w_in = mla_w_in[0]
    w_in_p = jnp.concatenate([w_in[:, :Q_LORA + KV_LORA], _spread_rope(w_in[:, Q_LORA + KV_LORA:])],
                             axis=1).astype(BF16)
    wq = mla_w_q_up[0].reshape(Q_LORA, MLA_HEADS, QK_DIM)
    wq_p = jnp.concatenate([wq[..., :QK_NOPE].reshape(Q_LORA, -1),
                            _spread_rope(wq[..., QK_NOPE:]).reshape(Q_LORA, -1)], axis=1).astype(BF16)
    wkv = mla_w_kv_up[0].reshape(KV_LORA, MLA_HEADS, QK_NOPE + V_DIM)
    wkv_p = jnp.concatenate([wkv[..., :QK_NOPE].reshape(KV_LORA, -1),
                             wkv[..., QK_NOPE:].reshape(KV_LORA, -1)], axis=1).astype(BF16)
    inv_freq = 1.0 / (ROPE_THETA ** (jnp.arange(0, QK_ROPE, 2, dtype=F32) / QK_ROPE))
    freq = _spread_rope(jnp.concatenate([inv_freq, inv_freq])[None, :])
    ones = jnp.ones((1, QK_ROPE // 2), F32)
    sign = _spread_rope(jnp.concatenate([-ones, ones], axis=1))
    pos = positions.reshape(n, 1).astype(jnp.int32)
    q_nope, q_pe, k_nope, k_pe, v = mla_proj(
        x2, pos, vec(b_norm_mix[0]), w_in_p, vec(mla_q_norm[0]), wq_p, vec(mla_kv_norm[0]), wkv_p,
        freq, sign, tm=_tile(n, 512))
    o = attention(q_nope, q_pe, k_nope, k_pe, v, batch=batch, seq=seq, tq=_tile(seq, 512), heads=8)
    w_router = jnp.pad(moe_w_router[0].astype(F32), ((0, 0), (0, LANES - N_EXPERTS)))
    wr_hi = w_router.astype(BF16)
    w_router = jnp.concatenate([wr_hi, (w_router - wr_hi.astype(F32)).astype(BF16)], axis=1)
    x3, hn, route, cnt = out_router(o, mla_w_out[0].astype(BF16), x2, vec(b_norm_ffn[0]), w_router,
                                    tm=_tile(n, 512))

    tm = _tile(2 * n, FF_TM)
    counts = cnt[0, :N_EXPERTS].astype(jnp.int32)
    padded = ((counts + tm - 1) // tm) * tm
    pad_end = jnp.cumsum(padded)
    pad_start = pad_end - padded
    experts = route[:, 0:2].astype(jnp.int32)
    ranks = route[:, 4:6].astype(jnp.int32)
    dest = (pad_start[experts] + ranks).reshape(-1)
    p = 2 * n + N_EXPERTS * tm
    n_blocks = p // tm
    src_tok = jnp.zeros((p,), jnp.int32).at[dest].set(jnp.arange(2 * n, dtype=jnp.int32) // 2,
                                                      unique_indices=True, mode='promise_in_bounds')
    block_start = jnp.arange(n_blocks, dtype=jnp.int32) * tm
    block_e = jnp.minimum(jnp.sum(block_start[:, None] >= pad_end[None, :], axis=1),
                          N_EXPERTS - 1).astype(jnp.int32)
    block_valid = jnp.clip((pad_start + counts)[block_e] - block_start, 0, tm).astype(jnp.int32)
    n_used = (pad_end[-1:] // tm).astype(jnp.int32)

    y_buf = moe_ffn(src_tok, block_e, block_valid, n_used, hn, moe_w_gate[0], moe_w_up[0], moe_w_down[0],
                    tm=tm, tc=FF_TC)
    out = combine(dest, x3, route, vec(final_norm), y_buf, tm=_tile(n, 256))
    return out.reshape(batch, seq, d)
```

```python
import functools
import math

import jax
import jax.numpy as jnp
from jax import lax
from jax.experimental import pallas as pl
from jax.experimental.pallas import tpu as pltpu

EPS = 1e-6
RG_HEADS = 16
RG_BLOCK = 128
CONV_W = 4
RG_C = 8.0
MLA_HEADS = 16
Q_LORA = 512
KV_LORA = 512
QK_NOPE = 128
QK_ROPE = 64
V_DIM = 128
QK_DIM = QK_NOPE + QK_ROPE
ROPE_THETA = 10000.0
N_EXPERTS = 8
LANES = 128
SUBLANES = 8
VMEM_LIMIT_BYTES = 56 * 1024 * 1024
FF_TC = 256
FF_TM = 1024

BF16 = jnp.bfloat16
F32 = jnp.float32


def _params(*sem):
    return pltpu.CompilerParams(dimension_semantics=sem, vmem_limit_bytes=VMEM_LIMIT_BYTES)


def _rms(x, g):
    return x * lax.rsqrt(jnp.mean(x * x, axis=-1, keepdims=True) + EPS) * g


def _dot(a, b):
    return jnp.dot(a, b, preferred_element_type=F32)


def _sigmoid(x):
    return 1.0 / (1.0 + jnp.exp(-x))


def _sigmoid_tanh(x):
    return 0.5 * jnp.tanh(0.5 * x) + 0.5


def _norm_mm_kernel(x_ref, g_ref, w_ref, o_ref, hn_ref):
    @pl.when(pl.program_id(1) == 0)
    def _():
        hn_ref[...] = _rms(x_ref[...], g_ref[...]).astype(BF16)

    o_ref[...] = _dot(hn_ref[...], w_ref[...]).astype(o_ref.dtype)


def norm_mm(x, g, w, *, tm, tn, out_dtype):
    n, d = x.shape
    f = w.shape[1]
    return pl.pallas_call(
        _norm_mm_kernel,
        grid=(n // tm, f // tn),
        in_specs=[pl.BlockSpec((tm, d), lambda i, j: (i, 0)),
                  pl.BlockSpec((1, d), lambda i, j: (0, 0)),
                  pl.BlockSpec((d, tn), lambda i, j: (0, j))],
        out_specs=pl.BlockSpec((tm, tn), lambda i, j: (i, j)),
        out_shape=jax.ShapeDtypeStruct((n, f), out_dtype),
        scratch_shapes=[pltpu.VMEM((tm, d), BF16)],
        compiler_params=_params("parallel", "arbitrary"),
        name="norm_mm",
    )(x, g, w)


def _mm_res_kernel(y_ref, w_ref, r_ref, o_ref):
    o_ref[...] = r_ref[...] + _dot(y_ref[...], w_ref[...])


def mm_res(y, w, res, *, tm, tn):
    n, k = y.shape
    f = w.shape[1]
    return pl.pallas_call(
        _mm_res_kernel,
        grid=(n // tm, f // tn),
        in_specs=[pl.BlockSpec((tm, k), lambda i, j: (i, 0)),
                  pl.BlockSpec((k, tn), lambda i, j: (0, j)),
                  pl.BlockSpec((tm, tn), lambda i, j: (i, j))],
        out_specs=pl.BlockSpec((tm, tn), lambda i, j: (i, j)),
        out_shape=jax.ShapeDtypeStruct((n, f), F32),
        compiler_params=_params("parallel", "parallel"),
        name="mm_res",
    )(y, w, res)


def _gelu_tanh(x):
    c = math.sqrt(2.0 / math.pi)
    return 0.5 * x * (1.0 + jnp.tanh(c * (x + 0.044715 * (x * x * x))))


def _softplus(z):
    return jnp.maximum(z, 0.0) + jnp.log1p(jnp.exp(-jnp.abs(z)))


def _rglru_kernel(gate_ref, rec_ref, cw_ref, cb_ref, wax_ref, ba_ref, bx_ref, lam_ref,
                  y_ref, xbuf, a_s, b_s, h_s):
    s = pl.program_id(1)
    t = rec_ref.shape[0]
    d = rec_ref.shape[1]

    @pl.when(s == 0)
    def _():
        xbuf[0:SUBLANES, :] = jnp.zeros((SUBLANES, d), F32)
        h_s[...] = jnp.zeros_like(h_s)

    @pl.when(s > 0)
    def _():
        xbuf[0:SUBLANES, :] = xbuf[t:t + SUBLANES, :]

    xbuf[SUBLANES:t + SUBLANES, :] = rec_ref[...].astype(F32)

    row = lax.broadcasted_iota(jnp.int32, (t // SUBLANES, SUBLANES, RG_BLOCK), 1)
    for h in range(RG_HEADS):
        c0, c1 = h * RG_BLOCK, (h + 1) * RG_BLOCK
        xc = cb_ref[:, c0:c1] + jnp.zeros((t, RG_BLOCK), F32)
        for k in range(CONV_W):
            sh = CONV_W - 1 - k
            xc = xc + xbuf[SUBLANES - sh:SUBLANES - sh + t, c0:c1] * cw_ref[k:k + 1, c0:c1]
        gg = _dot(xc.astype(BF16), wax_ref[h])
        r = _sigmoid_tanh(gg[:, :RG_BLOCK] + ba_ref[:, c0:c1])
        i = _sigmoid_tanh(gg[:, RG_BLOCK:] + bx_ref[:, c0:c1])
        log_a = -RG_C * r * _softplus(-lam_ref[:, c0:c1])
        a = jnp.exp(log_a)
        om = 1.0 - a * a
        b = jnp.where(om > 0.0, om * lax.rsqrt(om), 0.0) * i * xc
        a = a.reshape(t // SUBLANES, SUBLANES, RG_BLOCK)
        b = b.reshape(t // SUBLANES, SUBLANES, RG_BLOCK)
        for sft in (1, 2, 4):
            a_sh = pltpu.roll(a, sft, axis=1)
            b_sh = pltpu.roll(b, sft, axis=1)
            m = row >= sft
            b = jnp.where(m, a * b_sh + b, b)
            a = jnp.where(m, a * a_sh, a)
        a_s[:, c0:c1] = a.reshape(t, RG_BLOCK)
        b_s[:, c0:c1] = b.reshape(t, RG_BLOCK)

    def group(gi, hc):
        r0 = pl.multiple_of(gi * SUBLANES, SUBLANES)
        rows = b_s[pl.ds(r0, SUBLANES), :] + a_s[pl.ds(r0, SUBLANES), :] * hc
        b_s[pl.ds(r0, SUBLANES), :] = rows
        return rows[SUBLANES - 1:SUBLANES, :]

    h_s[...] = lax.fori_loop(0, t // SUBLANES, group, h_s[...])
    y_ref[...] = (b_s[...] * _gelu_tanh(gate_ref[...].astype(F32))).astype(y_ref.dtype)


def rglru_core(u, conv_w, conv_b, wax, b_a, b_x, lam, *, batch, seq, t):
    n = u.shape[0]
    d = u.shape[1] // 2
    ns = seq // t
    vec = pl.BlockSpec((1, d), lambda b, s: (0, 0))
    return pl.pallas_call(
        _rglru_kernel,
        grid=(batch, ns),
        in_specs=[pl.BlockSpec((t, d), lambda b, s: (b * ns + s, 0)),
                  pl.BlockSpec((t, d), lambda b, s: (b * ns + s, 1)),
                  pl.BlockSpec((CONV_W, d), lambda b, s: (0, 0)),
                  vec,
                  pl.BlockSpec((RG_HEADS, RG_BLOCK, 2 * RG_BLOCK), lambda b, s: (0, 0, 0)),
                  vec, vec, vec],
        out_specs=pl.BlockSpec((t, d), lambda b, s: (b * ns + s, 0)),
        out_shape=jax.ShapeDtypeStruct((n, d), BF16),
        scratch_shapes=[pltpu.VMEM((t + SUBLANES, d), F32),
                        pltpu.VMEM((t, d), F32),
                        pltpu.VMEM((t, d), F32),
                        pltpu.VMEM((1, d), F32)],
        compiler_params=_params("arbitrary", "arbitrary"),
        name="rglru_core",
    )(u, u, conv_w, conv_b, wax, b_a, b_x, lam)


def _swiglu_acc(h, wg_ref, wu_ref, wd_ref):
    g = _dot(h, wg_ref[...].astype(BF16))
    u = _dot(h, wu_ref[...].astype(BF16))
    a = (g * _sigmoid(g) * u).astype(BF16)
    return _dot(a, wd_ref[...].astype(BF16))


def _ffn_kernel(x_ref, g_ref, wg_ref, wu_ref, wd_ref, o_ref, hn_ref):
    @pl.when(pl.program_id(1) == 0)
    def _():
        x = x_ref[...]
        hn_ref[...] = _rms(x, g_ref[...]).astype(BF16)
        o_ref[...] = x

    o_ref[...] += _swiglu_acc(hn_ref[...], wg_ref, wu_ref, wd_ref)


def ffn_dense(x, g, wg, wu, wd, *, tm, tc):
    n, d = x.shape
    f = wg.shape[1]
    return pl.pallas_call(
        _ffn_kernel,
        grid=(n // tm, f // tc),
        in_specs=[pl.BlockSpec((tm, d), lambda i, j: (i, 0)),
                  pl.BlockSpec((1, d), lambda i, j: (0, 0)),
                  pl.BlockSpec((d, tc), lambda i, j: (0, j)),
                  pl.BlockSpec((d, tc), lambda i, j: (0, j)),
                  pl.BlockSpec((tc, d), lambda i, j: (j, 0))],
        out_specs=pl.BlockSpec((tm, d), lambda i, j: (i, 0)),
        out_shape=jax.ShapeDtypeStruct((n, d), F32),
        scratch_shapes=[pltpu.VMEM((tm, d), BF16)],
        compiler_params=_params("parallel", "arbitrary"),
        name="ffn_dense",
    )(x, g, wg, wu, wd)


def _mla_proj_kernel(x_ref, pos_ref, g_ref, win_ref, qn_ref, wq_ref, kvn_ref, wkv_ref,
                     freq_ref, sign_ref, qnope_ref, qpe_ref, knope_ref, kpe_ref, v_ref):
    d = x_ref.shape[1]
    hn = _rms(x_ref[...], g_ref[...]).astype(BF16)
    u = _dot(hn, win_ref[...])
    ang = pos_ref[...].astype(F32) * freq_ref[...]
    cos = jnp.cos(ang)
    sin = jnp.sin(ang) * sign_ref[...]

    def rope(xs):
        return xs * cos + pltpu.roll(xs, LANES // 2, axis=1) * sin

    kpe_ref[...] = rope(u[:, Q_LORA + KV_LORA:]).astype(BF16)
    q = _dot(_rms(u[:, :Q_LORA], qn_ref[...]).astype(BF16), wq_ref[...])
    qnope_ref[...] = q[:, :d].astype(BF16)
    for h in range(MLA_HEADS):
        c0, c1 = h * LANES, (h + 1) * LANES
        qpe_ref[:, c0:c1] = rope(q[:, d + c0:d + c1]).astype(BF16)
    kv = _dot(_rms(u[:, Q_LORA:Q_LORA + KV_LORA], kvn_ref[...]).astype(BF16), wkv_ref[...])
    knope_ref[...] = kv[:, :d].astype(BF16)
    v_ref[...] = kv[:, d:].astype(BF16)


def mla_proj(x, pos, g, w_in, q_norm, w_q, kv_norm, w_kv, freq, sign, *, tm):
    n, d = x.shape
    row = lambda w: pl.BlockSpec((tm, w), lambda i: (i, 0))
    full = lambda a: pl.BlockSpec(a.shape, lambda i: (0, 0))
    big = jax.ShapeDtypeStruct((n, d), BF16)
    return pl.pallas_call(
        _mla_proj_kernel,
        grid=(n // tm,),
        in_specs=[row(d), row(1), full(g), full(w_in), full(q_norm), full(w_q), full(kv_norm),
                  full(w_kv), full(freq), full(sign)],
        out_specs=[row(d), row(d), row(d), row(LANES), row(d)],
        out_shape=[big, big, big, jax.ShapeDtypeStruct((n, LANES), BF16), big],
        compiler_params=_params("parallel"),
        name="mla_proj",
    )(x, pos, g, w_in, q_norm, w_q, kv_norm, w_kv, freq, sign)


def _attn_kernel(qn_ref, qp_ref, kn_ref, kp_ref, v_ref, o_ref, m_s, acc_s, bias_s):
    qi = pl.program_id(2)
    tq = qn_ref.shape[0]
    heads = qn_ref.shape[1] // LANES
    c = (QK_DIM ** -0.5) * math.log2(math.e)
    hs = lambda h: slice(h * LANES, (h + 1) * LANES)
    q = [jnp.concatenate([qn_ref[:, hs(h)], qp_ref[:, hs(h)]], axis=1) for h in range(heads)]
    ones = jnp.ones((tq, LANES), BF16)
    m_s[...] = jnp.full_like(m_s, -jnp.inf)
    acc_s[...] = jnp.zeros_like(acc_s)

    @pl.when((pl.program_id(0) == 0) & (pl.program_id(1) == 0) & (qi == 0))
    def _():
        q_idx = lax.broadcasted_iota(jnp.int32, (tq, tq), 0)
        k_idx = lax.broadcasted_iota(jnp.int32, (tq, tq), 1)
        bias_s[...] = jnp.where(k_idx <= q_idx, 0.0, -jnp.inf)

    def chunk(kj, masked):
        k0 = pl.multiple_of(kj * tq, tq)
        kp = kp_ref[pl.ds(k0, tq), :]
        for h in range(heads):
            k = jnp.concatenate([kn_ref[pl.ds(k0, tq), hs(h)], kp], axis=1)
            s = lax.dot_general(q[h], k, (((1,), (1,)), ((), ())), preferred_element_type=F32)
            if masked:
                s = s + bias_s[...]
            slabs = [s[:, hs(j)] for j in range(tq // LANES)]
            part = functools.reduce(jnp.maximum, slabs)
            m_prev = m_s[h]
            m_new = jnp.maximum(m_prev, jnp.max(part, axis=1, keepdims=True))
            p = jnp.concatenate([jnp.exp2((sl - m_new) * c) for sl in slabs], axis=1)
            alpha = jnp.exp2((m_prev - m_new) * c)
            vv = jnp.concatenate([v_ref[pl.ds(k0, tq), hs(h)], ones], axis=1)
            acc = acc_s[h]
            acc = jnp.concatenate([acc[:, :LANES] * alpha, acc[:, LANES:] * alpha], axis=1)
            acc_s[h] = acc + _dot(p.astype(BF16), vv)
            m_s[h] = m_new

    def body(kp2, carry):
        chunk(2 * kp2, False)
        chunk(2 * kp2 + 1, False)
        return carry

    lax.fori_loop(0, qi // 2, body, 0)

    @pl.when(qi % 2 == 1)
    def _():
        chunk(qi - 1, False)

    chunk(qi, True)
    for h in range(heads):
        acc = acc_s[h]
        o_ref[:, hs(h)] = (acc[:, :LANES] / acc[:, LANES:]).astype(o_ref.dtype)


def attention(q_nope, q_pe, k_nope, k_pe, v, *, batch, seq, tq, heads):
    n, d = q_nope.shape
    nq = seq // tq
    w = heads * LANES
    qspec = pl.BlockSpec((tq, w), lambda b, h, i: (b * nq + i, h))
    kspec = pl.BlockSpec((seq, w), lambda b, h, i: (b, h))
    return pl.pallas_call(
        _attn_kernel,
        grid=(batch, MLA_HEADS // heads, nq),
        in_specs=[qspec, qspec, kspec,
                  pl.BlockSpec((seq, LANES), lambda b, h, i: (b, 0)),
                  kspec],
        out_specs=qspec,
        out_shape=jax.ShapeDtypeStruct((n, d), BF16),
        scratch_shapes=[pltpu.VMEM((heads, tq, LANES), F32),
                        pltpu.VMEM((heads, tq, 2 * LANES), F32),
                        pltpu.VMEM((tq, tq), F32)],
        compiler_params=_params("arbitrary", "arbitrary", "arbitrary"),
        name="attention",
    )(q_nope, q_pe, k_nope, k_pe, v)


def _out_router_kernel(o_ref, w_ref, r_ref, g_ref, wr_ref, x_ref, hn_ref, route_ref, cnt_ref, carry):
    i = pl.program_id(0)
    tm = o_ref.shape[0]

    @pl.when(i == 0)
    def _():
        carry[...] = jnp.zeros_like(carry)

    x = r_ref[...] + _dot(o_ref[...], w_ref[...])
    x_ref[...] = x
    hn = _rms(x, g_ref[...])
    hn_ref[...] = hn
    hn_hi = hn.astype(BF16)
    hn_lo = (hn - hn_hi.astype(F32)).astype(BF16)
    parts = _dot(hn_hi, wr_ref[...]) + _dot(hn_lo, wr_ref[...])
    logits = parts[:, :LANES] + parts[:, LANES:]
    lane = lax.broadcasted_iota(jnp.int32, (tm, LANES), 1)
    lane_f = lane.astype(F32)
    neg = -jnp.inf
    logits = jnp.where(lane < N_EXPERTS, logits, neg)
    m1 = jnp.max(logits, axis=1, keepdims=True)
    e1 = jnp.min(jnp.where(logits == m1, lane_f, float(LANES)), axis=1, keepdims=True)
    rest = jnp.where(lane_f == e1, neg, logits)
    m2 = jnp.max(rest, axis=1, keepdims=True)
    e2 = jnp.min(jnp.where(rest == m2, lane_f, float(LANES)), axis=1, keepdims=True)
    z = jnp.exp(m2 - m1)
    g1 = 1.0 / (1.0 + z)
    g2 = z / (1.0 + z)
    sel1 = lane_f == e1
    sel2 = lane_f == e2
    sel = jnp.where(sel1 | sel2, 1.0, 0.0)
    rr = lax.broadcasted_iota(jnp.int32, (tm, tm), 0)
    cc = lax.broadcasted_iota(jnp.int32, (tm, tm), 1)
    tril = jnp.where(cc < rr, 1.0, 0.0).astype(BF16)
    before = _dot(tril, sel.astype(BF16)) + carry[...]
    rank1 = jnp.sum(jnp.where(sel1, before, 0.0), axis=1, keepdims=True)
    rank2 = jnp.sum(jnp.where(sel2, before, 0.0), axis=1, keepdims=True)
    total = carry[...] + jnp.sum(sel, axis=0, keepdims=True)
    carry[...] = total
    cnt_ref[...] = jnp.broadcast_to(total, cnt_ref.shape)
    route = jnp.zeros((tm, LANES), F32)
    for col, val in enumerate((e1, e2, g1, g2, rank1, rank2)):
        route = jnp.where(lane == col, val, route)
    route_ref[...] = route


def out_router(o, w_out, res, g, w_router, *, tm):
    n, d = res.shape
    row = lambda w: pl.BlockSpec((tm, w), lambda i: (i, 0))
    full = lambda a: pl.BlockSpec(a.shape, lambda i: (0, 0))
    return pl.pallas_call(
        _out_router_kernel,
        grid=(n // tm,),
        in_specs=[row(d), full(w_out), row(d), full(g), full(w_router)],
        out_specs=[row(d), row(d), row(LANES), pl.BlockSpec((SUBLANES, LANES), lambda i: (0, 0))],
        out_shape=[jax.ShapeDtypeStruct((n, d), F32), jax.ShapeDtypeStruct((n, d), F32),
                   jax.ShapeDtypeStruct((n, LANES), F32),
                   jax.ShapeDtypeStruct((SUBLANES, LANES), F32)],
        scratch_shapes=[pltpu.VMEM((1, LANES), F32)],
        compiler_params=_params("arbitrary"),
        name="out_router",
    )(o, w_out, res, g, w_router)


GATHER_SLOTS = 2
MOE_ROW_PARTS = 4


def _row_copy(src_hbm, row, dst, r, sem):
    return pltpu.make_async_copy(src_hbm.at[pl.ds(row, 1), :], dst.at[pl.ds(r, 1), :], sem)


def _moe_ffn_kernel(src_ref, be_ref, valid_ref, nu_ref, hn_hbm, wg_ref, wu_ref, wd_ref, o_ref, xg, xb, sems,
                    *, per_step):
    i = pl.program_id(0)
    j = pl.program_id(1)
    tm = o_ref.shape[0]
    rows = xg.shape[1]
    n_used = nu_ref[0]
    slot = i % GATHER_SLOTS

    def start_row(blk, sl, r):
        tok = src_ref[blk * tm + jnp.minimum(r, tm - 1)]
        _row_copy(hn_hbm, tok, xg.at[sl], r, sems.at[sl]).start()

    @pl.when((i == 0) & (j == 0))
    def _():
        def first(r, c):
            start_row(0, 0, r)
            return c
        lax.fori_loop(0, rows, first, 0, unroll=8)

    @pl.when((j == 0) & (i <= n_used))
    def _():
        pltpu.make_async_copy(hn_hbm.at[pl.ds(0, rows), :], xg.at[slot], sems.at[slot]).wait()

    @pl.when((j == 0) & (i < n_used))
    def _():
        xb[...] = xg[slot, :tm, :].astype(BF16)

    @pl.when(j == 0)
    def _():
        o_ref[...] = jnp.zeros_like(o_ref)

    def step(m):
        nxt = jnp.minimum(i + 1, n_used - 1)
        for u in range(per_step):
            start_row(nxt, (i + 1) % GATHER_SLOTS, j * per_step + u)
        o_ref[:m, :] += _swiglu_acc(xb[:m, :], wg_ref, wu_ref, wd_ref)

    quarter = tm // MOE_ROW_PARTS
    parts = (valid_ref[i] + quarter - 1) // quarter
    for k in range(1, MOE_ROW_PARTS + 1):
        @pl.when((i < n_used) & (parts == k))
        def _(k=k):
            step(k * quarter)


def moe_ffn(src_tok, block_e, block_valid, n_used, hn, wg, wu, wd, *, tm, tc):
    d = hn.shape[1]
    p = src_tok.shape[0]
    nj = wg.shape[2] // tc
    per_step = -(-tm // nj)
    while (per_step * nj) % SUBLANES:
        per_step += 1

    def blk(i, nu):
        return jnp.minimum(i, nu[0] - 1)

    def col(i, j, nu):
        return jnp.where(i < nu[0], j, nj - 1)

    return pl.pallas_call(
        functools.partial(_moe_ffn_kernel, per_step=per_step),
        grid_spec=pltpu.PrefetchScalarGridSpec(
            num_scalar_prefetch=4,
            grid=(p // tm, nj),
            in_specs=[pl.BlockSpec(memory_space=pl.ANY),
                      pl.BlockSpec((None, d, tc), lambda i, j, s, be, bv, nu: (be[blk(i, nu)], 0, col(i, j, nu))),
                      pl.BlockSpec((None, d, tc), lambda i, j, s, be, bv, nu: (be[blk(i, nu)], 0, col(i, j, nu))),
                      pl.BlockSpec((None, tc, d), lambda i, j, s, be, bv, nu: (be[blk(i, nu)], col(i, j, nu), 0))],
            out_specs=pl.BlockSpec((tm, d), lambda i, j, s, be, bv, nu: (i, 0)),
            scratch_shapes=[pltpu.VMEM((GATHER_SLOTS, per_step * nj, d), F32), pltpu.VMEM((tm, d), BF16),
                            pltpu.SemaphoreType.DMA((GATHER_SLOTS,))]),
        out_shape=jax.ShapeDtypeStruct((p, d), F32),
        compiler_params=_params("arbitrary", "arbitrary"),
        name="moe_ffn",
    )(src_tok, block_e, block_valid, n_used, hn, wg, wu, wd)


def _combine_kernel(dest_ref, x_ref, route_ref, g_ref, y_hbm, o_ref, ya, yb, sems):
    i = pl.program_id(0)
    tm = o_ref.shape[0]
    slot = i % 2

    def copies(blk, sl, r):
        t2 = 2 * (blk * tm + r)
        return (_row_copy(y_hbm, dest_ref[t2], ya.at[sl], r, sems.at[0, sl]),
                _row_copy(y_hbm, dest_ref[t2 + 1], yb.at[sl], r, sems.at[1, sl]))

    def issue(blk, sl):
        def body(r, c):
            ca, cb = copies(blk, sl, r)
            ca.start()
            cb.start()
            return c
        lax.fori_loop(0, tm, body, 0, unroll=8)

    @pl.when(i == 0)
    def _():
        issue(0, 0)

    @pl.when(i + 1 < pl.num_programs(0))
    def _():
        issue(i + 1, 1 - slot)

    pltpu.make_async_copy(y_hbm.at[pl.ds(0, tm), :], ya.at[slot], sems.at[0, slot]).wait()
    pltpu.make_async_copy(y_hbm.at[pl.ds(0, tm), :], yb.at[slot], sems.at[1, slot]).wait()
    route = route_ref[...]
    x = x_ref[...] + (ya[slot] * route[:, 2:3] + yb[slot] * route[:, 3:4])
    o_ref[...] = _rms(x, g_ref[...])


def combine(dest, x, route, g, y_buf, *, tm):
    n, d = x.shape
    return pl.pallas_call(
        _combine_kernel,
        grid_spec=pltpu.PrefetchScalarGridSpec(
            num_scalar_prefetch=1,
            grid=(n // tm,),
            in_specs=[pl.BlockSpec((tm, d), lambda i, dst: (i, 0)),
                      pl.BlockSpec((tm, LANES), lambda i, dst: (i, 0)),
                      pl.BlockSpec((1, d), lambda i, dst: (0, 0)),
                      pl.BlockSpec(memory_space=pl.ANY)],
            out_specs=pl.BlockSpec((tm, d), lambda i, dst: (i, 0)),
            scratch_shapes=[pltpu.VMEM((2, tm, d), F32), pltpu.VMEM((2, tm, d), F32),
                            pltpu.SemaphoreType.DMA((2, 2))]),
        out_shape=jax.ShapeDtypeStruct((n, d), F32),
        compiler_params=_params("arbitrary"),
        name="moe_combine",
    )(dest, x, route, g, y_buf)


def _spread_rope(w):
    half = QK_ROPE // 2
    z = jnp.zeros(w.shape[:-1] + (half,), w.dtype)
    return jnp.concatenate([w[..., :half], z, w[..., half:], z], axis=-1)


def _tile(n, want):
    t = min(n, want)
    assert n % t == 0, (n, t)
    return t


def kernel(x, positions, a_norm_mix, rg_w_in, rg_conv_w, rg_conv_b, rg_w_a, rg_b_a, rg_w_x, rg_b_x, rg_lambda, rg_w_out, a_norm_ffn, ff_w_gate, ff_w_up, ff_w_down, b_norm_mix, mla_w_in, mla_q_norm, mla_w_q_up, mla_kv_norm, mla_w_kv_up, mla_w_out, b_norm_ffn, moe_w_router, moe_w_gate, moe_w_up, moe_w_down, final_norm):
    batch, seq, d = x.shape
    n = batch * seq
    assert a_norm_mix.shape[0] == 1 and b_norm_mix.shape[0] == 1
    xf = x.reshape(n, d)
    vec = lambda a: a.reshape(1, -1).astype(F32)

    u = norm_mm(xf, vec(a_norm_mix[0]), rg_w_in[0].astype(BF16),
                tm=_tile(n, 1024), tn=1024, out_dtype=BF16)
    wax = jnp.concatenate([rg_w_a[0], rg_w_x[0]], axis=-1).astype(BF16)
    y = rglru_core(u, rg_conv_w[0], vec(rg_conv_b[0]), wax, vec(rg_b_a[0]), vec(rg_b_x[0]),
                   vec(rg_lambda[0]), batch=batch, seq=seq, t=_tile(seq, 256))
    x1 = mm_res(y, rg_w_out[0].astype(BF16), xf, tm=_tile(n, 1024), tn=1024)
    x2 = ffn_dense(x1, vec(a_norm_ffn[0]), ff_w_gate[0], ff_w_up[0], ff_w_down[0],
                   tm=_tile(n, FF_TM), tc=FF_TC)

    w_in = mla_w_in[0]
    w_in_p = jnp.concatenate([w_in[:, :Q_LORA + KV_LORA], _spread_rope(w_in[:, Q_LORA + KV_LORA:])],
                             axis=1).astype(BF16)
    wq = mla_w_q_up[0].reshape(Q_LORA, MLA_HEADS, QK_DIM)
    wq_p = jnp.concatenate([wq[..., :QK_NOPE].reshape(Q_LORA, -1),
                            _spread_rope(wq[..., QK_NOPE:]).reshape(Q_LORA, -1)], axis=1).astype(BF16)
    wkv = mla_w_kv_up[0].reshape(KV_LORA, MLA_HEADS, QK_NOPE + V_DIM)
    wkv_p = jnp.concatenate([wkv[..., :QK_NOPE].reshape(KV_LORA, -1),
                             wkv[..., QK_NOPE:].reshape(KV_LORA, -1)], axis=1).astype(BF16)
    inv_freq = 1.0 / (ROPE_THETA ** (jnp.arange(0, QK_ROPE, 2, dtype=F32) / QK_ROPE))
    freq = _spread_rope(jnp.concatenate([inv_freq, inv_freq])[None, :])
    ones = jnp.ones((1, QK_ROPE // 2), F32)
    sign = _spread_rope(jnp.concatenate([-ones, ones], axis=1))
    pos = positions.reshape(n, 1).astype(jnp.int32)
    q_nope, q_pe, k_nope, k_pe, v = mla_proj(
        x2, pos, vec(b_norm_mix[0]), w_in_p, vec(mla_q_norm[0]), wq_p, vec(mla_kv_norm[0]), wkv_p,
        freq, sign, tm=_tile(n, 512))
    o = attention(q_nope, q_pe, k_nope, k_pe, v, batch=batch, seq=seq, tq=_tile(seq, 512), heads=8)
    w_router = jnp.pad(moe_w_router[0].astype(F32), ((0, 0), (0, LANES - N_EXPERTS)))
    wr_hi = w_router.astype(BF16)
    w_router = jnp.concatenate([wr_hi, (w_router - wr_hi.astype(F32)).astype(BF16)], axis=1)
    x3, hn, route, cnt = out_router(o, mla_w_out[0].astype(BF16), x2, vec(b_norm_ffn[0]), w_router,
                                    tm=_tile(n, 512))

    tm = _tile(2 * n, FF_TM)
    counts = cnt[0, :N_EXPERTS].astype(jnp.int32)
    padded = ((counts + tm - 1) // tm) * tm
    pad_end = jnp.cumsum(padded)
    pad_start = pad_end - padded
    experts = route[:, 0:2].astype(jnp.int32)
    ranks = route[:, 4:6].astype(jnp.int32)
    dest = (pad_start[experts] + ranks).reshape(-1)
    p = 2 * n + N_EXPERTS * tm
    n_blocks = p // tm
    src_tok = jnp.zeros((p,), jnp.int32).at[dest].set(jnp.arange(2 * n, dtype=jnp.int32) // 2,
                                                      unique_indices=True, mode='promise_in_bounds')
    block_start = jnp.arange(n_blocks, dtype=jnp.int32) * tm
    block_e = jnp.minimum(jnp.sum(block_start[:, None] >= pad_end[None, :], axis=1),
                          N_EXPERTS - 1).astype(jnp.int32)
    block_valid = jnp.clip((pad_start + counts)[block_e] - block_start, 0, tm).astype(jnp.int32)
    n_used = (pad_end[-1:] // tm).astype(jnp.int32)

    y_buf = moe_ffn(src_tok, block_e, block_valid, n_used, hn, moe_w_gate[0], moe_w_up[0], moe_w_down[0],
                    tm=tm, tc=FF_TC)
    out = combine(dest, x3, route, vec(final_norm), y_buf, tm=_tile(n, 256))
    return out.reshape(batch, seq, d)
```

```python
import functools
import math

import jax
import jax.numpy as jnp
from jax import lax
from jax.experimental import pallas as pl
from jax.experimental.pallas import tpu as pltpu

EPS = 1e-6
RG_HEADS = 16
RG_BLOCK = 128
CONV_W = 4
RG_C = 8.0
MLA_HEADS = 16
Q_LORA = 512
KV_LORA = 512
QK_NOPE = 128
QK_ROPE = 64
V_DIM = 128
QK_DIM = QK_NOPE + QK_ROPE
ROPE_THETA = 10000.0
N_EXPERTS = 8
LANES = 128
SUBLANES = 8
VMEM_LIMIT_BYTES = 56 * 1024 * 1024
FF_TC = 256
FF_TM = 1024

BF16 = jnp.bfloat16
F32 = jnp.float32


def _params(*sem):
    return pltpu.CompilerParams(dimension_semantics=sem, vmem_limit_bytes=VMEM_LIMIT_BYTES)


def _rms(x, g):
    return x * lax.rsqrt(jnp.mean(x * x, axis=-1, keepdims=True) + EPS) * g


def _dot(a, b):
    return jnp.dot(a, b, preferred_element_type=F32)


def _sigmoid(x):
    return 1.0 / (1.0 + jnp.exp(-x))


def _sigmoid_tanh(x):
    return 0.5 * jnp.tanh(0.5 * x) + 0.5


def _gelu_tanh(x):
    c = math.sqrt(2.0 / math.pi)
    return 0.5 * x * (1.0 + jnp.tanh(c * (x + 0.044715 * (x * x * x))))


def _softplus(z):
    return jnp.maximum(z, 0.0) + jnp.log1p(jnp.exp(-jnp.abs(z)))


def _rglru_layer_kernel(xn_ref, xp_ref, g_ref, win_hbm, cw_ref, cb_ref, wax_ref, ba_ref, bx_ref, lam_ref, wout_hbm,
                        o_ref, win_s, wout_s, u_s, y_s, gate_s, hn_s, xbuf, a_s, b_s, h_s, sem, *, ns):
    q = pl.program_id(0)
    s = q % ns
    t, d = xn_ref.shape
    cur = q % 2
    nxt = (q + 1) % 2

    def project(x_ref):
        return _dot(_rms(x_ref[...], g_ref[...]).astype(BF16), win_s[...]).astype(BF16)

    @pl.when(q == 0)
    def _():
        cin = pltpu.make_async_copy(win_hbm, win_s, sem.at[0])
        cout = pltpu.make_async_copy(wout_hbm, wout_s, sem.at[1])
        cin.start()
        cout.start()
        cin.wait()
        cout.wait()
        u_s[0] = project(xp_ref)
        y_s[...] = jnp.zeros_like(y_s)

    @pl.when(s == 0)
    def _():
        xbuf[0:SUBLANES, :] = jnp.zeros((SUBLANES, d), F32)
        h_s[...] = jnp.zeros_like(h_s)

    @pl.when(s > 0)
    def _():
        xbuf[0:SUBLANES, :] = xbuf[t:t + SUBLANES, :]

    xbuf[SUBLANES:t + SUBLANES, :] = u_s[cur, :, d:].astype(F32)
    gate_s[...] = u_s[cur, :, :d]
    hn_s[...] = _rms(xn_ref[...], g_ref[...]).astype(BF16)
    pw = u_s.shape[2] // RG_HEADS
    ow = 2 * d // RG_HEADS

    row = lax.broadcasted_iota(jnp.int32, (t // SUBLANES, SUBLANES, RG_BLOCK), 1)
    for h in range(RG_HEADS):
        c0, c1 = h * RG_BLOCK, (h + 1) * RG_BLOCK
        xc = cb_ref[:, c0:c1] + jnp.zeros((t, RG_BLOCK), F32)
        for k in range(CONV_W):
            sh = CONV_W - 1 - k
            xc = xc + xbuf[SUBLANES - sh:SUBLANES - sh + t, c0:c1] * cw_ref[k:k + 1, c0:c1]
        gg = _dot(xc.astype(BF16), wax_ref[h])
        u_s[nxt, :, h * pw:(h + 1) * pw] = _dot(hn_s[...], win_s[:, h * pw:(h + 1) * pw]).astype(BF16)
        if h % 2 == 0:
            oc = slice((h // 2) * ow, (h // 2 + 1) * ow)
            o_ref[:, oc] = xp_ref[:, oc] + _dot(y_s[nxt], wout_s[:, oc])
        r = _sigmoid_tanh(gg[:, :RG_BLOCK] + ba_ref[:, c0:c1])
        i = _sigmoid_tanh(gg[:, RG_BLOCK:] + bx_ref[:, c0:c1])
        log_a = -RG_C * r * _softplus(-lam_ref[:, c0:c1])
        a = jnp.exp(log_a)
        om = 1.0 - a * a
        b = jnp.where(om > 0.0, om * lax.rsqrt(om), 0.0) * i * xc
        a = a.reshape(t // SUBLANES, SUBLANES, RG_BLOCK)
        b = b.reshape(t // SUBLANES, SUBLANES, RG_BLOCK)
        for sft in (1, 2, 4):
            a_sh = pltpu.roll(a, sft, axis=1)
            b_sh = pltpu.roll(b, sft, axis=1)
            m = row >= sft
            b = jnp.where(m, a * b_sh + b, b)
            a = jnp.where(m, a * a_sh, a)
        a_s[:, c0:c1] = a.reshape(t, RG_BLOCK)
        b_s[:, c0:c1] = b.reshape(t, RG_BLOCK)

    def group(gi, hc):
        r0 = pl.multiple_of(gi * SUBLANES, SUBLANES)
        rows = b_s[pl.ds(r0, SUBLANES), :] + a_s[pl.ds(r0, SUBLANES), :] * hc
        b_s[pl.ds(r0, SUBLANES), :] = rows
        return rows[SUBLANES - 1:SUBLANES, :]

    h_s[...] = lax.fori_loop(0, t // SUBLANES, group, h_s[...])
    y_s[cur] = (b_s[...] * _gelu_tanh(gate_s[...].astype(F32))).astype(BF16)


def rglru_layer(x, g, w_in, conv_w, conv_b, wax, b_a, b_x, lam, w_out, *, seq, t):
    n, d = x.shape
    ns = seq // t
    total = n // t
    vec = pl.BlockSpec((1, d), lambda q: (0, 0))
    hbm = pl.BlockSpec(memory_space=pl.ANY)
    return pl.pallas_call(
        functools.partial(_rglru_layer_kernel, ns=ns),
        grid=(total + 1,),
        in_specs=[pl.BlockSpec((t, d), lambda q: (jnp.minimum(q + 1, total - 1), 0)),
                  pl.BlockSpec((t, d), lambda q: (jnp.maximum(q - 1, 0), 0)),
                  vec, hbm,
                  pl.BlockSpec((CONV_W, d), lambda q: (0, 0)),
                  vec,
                  pl.BlockSpec((RG_HEADS, RG_BLOCK, 2 * RG_BLOCK), lambda q: (0, 0, 0)),
                  vec, vec, vec, hbm],
        out_specs=pl.BlockSpec((t, d), lambda q: (jnp.maximum(q - 1, 0), 0)),
        out_shape=jax.ShapeDtypeStruct((n, d), F32),
        scratch_shapes=[pltpu.VMEM((d, 2 * d), BF16),
                        pltpu.VMEM((d, d), BF16),
                        pltpu.VMEM((2, t, 2 * d), BF16),
                        pltpu.VMEM((2, t, d), BF16),
                        pltpu.VMEM((t, d), BF16),
                        pltpu.VMEM((t, d), BF16),
                        pltpu.VMEM((t + SUBLANES, d), F32),
                        pltpu.VMEM((t, d), F32),
                        pltpu.VMEM((t, d), F32),
                        pltpu.VMEM((1, d), F32),
                        pltpu.SemaphoreType.DMA((2,))],
        compiler_params=_params("arbitrary"),
        name="rglru_layer",
    )(x, x, g, w_in, conv_w, conv_b, wax, b_a, b_x, lam, w_out)


def _swiglu_acc(h, wg_ref, wu_ref, wd_ref):
    g = _dot(h, wg_ref[...].astype(BF16))
    u = _dot(h, wu_ref[...].astype(BF16))
    a = (g * _sigmoid(g) * u).astype(BF16)
    return _dot(a, wd_ref[...].astype(BF16))


def _ffn_kernel(x_ref, g_ref, wg_ref, wu_ref, wd_ref, o_ref, hn_ref):
    @pl.when(pl.program_id(1) == 0)
    def _():
        x = x_ref[...]
        hn_ref[...] = _rms(x, g_ref[...]).astype(BF16)
        o_ref[...] = x

    o_ref[...] += _swiglu_acc(hn_ref[...], wg_ref, wu_ref, wd_ref)


def ffn_dense(x, g, wg, wu, wd, *, tm, tc):
    n, d = x.shape
    f = wg.shape[1]
    return pl.pallas_call(
        _ffn_kernel,
        grid=(n // tm, f // tc),
        in_specs=[pl.BlockSpec((tm, d), lambda i, j: (i, 0)),
                  pl.BlockSpec((1, d), lambda i, j: (0, 0)),
                  pl.BlockSpec((d, tc), lambda i, j: (0, j)),
                  pl.BlockSpec((d, tc), lambda i, j: (0, j)),
                  pl.BlockSpec((tc, d), lambda i, j: (j, 0))],
        out_specs=pl.BlockSpec((tm, d), lambda i, j: (i, 0)),
        out_shape=jax.ShapeDtypeStruct((n, d), F32),
        scratch_shapes=[pltpu.VMEM((tm, d), BF16)],
        compiler_params=_params("parallel", "arbitrary"),
        name="ffn_dense",
    )(x, g, wg, wu, wd)


def _mla_proj_kernel(x_ref, pos_ref, g_ref, win_ref, qn_ref, wq_ref, kvn_ref, wkv_ref,
                     freq_ref, sign_ref, qnope_ref, qpe_ref, knope_ref, kpe_ref, v_ref):
    d = x_ref.shape[1]
    hn = _rms(x_ref[...], g_ref[...]).astype(BF16)
    u = _dot(hn, win_ref[...])
    ang = pos_ref[...].astype(F32) * freq_ref[...]
    cos = jnp.cos(ang)
    sin = jnp.sin(ang) * sign_ref[...]

    def rope(xs):
        return xs * cos + pltpu.roll(xs, LANES // 2, axis=1) * sin

    kpe_ref[...] = rope(u[:, Q_LORA + KV_LORA:]).astype(BF16)
    q = _dot(_rms(u[:, :Q_LORA], qn_ref[...]).astype(BF16), wq_ref[...])
    qnope_ref[...] = q[:, :d].astype(BF16)
    for h in range(MLA_HEADS):
        c0, c1 = h * LANES, (h + 1) * LANES
        qpe_ref[:, c0:c1] = rope(q[:, d + c0:d + c1]).astype(BF16)
    kv = _dot(_rms(u[:, Q_LORA:Q_LORA + KV_LORA], kvn_ref[...]).astype(BF16), wkv_ref[...])
    knope_ref[...] = kv[:, :d].astype(BF16)
    v_ref[...] = kv[:, d:].astype(BF16)


def mla_proj(x, pos, g, w_in, q_norm, w_q, kv_norm, w_kv, freq, sign, *, tm):
    n, d = x.shape
    row = lambda w: pl.BlockSpec((tm, w), lambda i: (i, 0))
    full = lambda a: pl.BlockSpec(a.shape, lambda i: (0, 0))
    big = jax.ShapeDtypeStruct((n, d), BF16)
    return pl.pallas_call(
        _mla_proj_kernel,
        grid=(n // tm,),
        in_specs=[row(d), row(1), full(g), full(w_in), full(q_norm), full(w_q), full(kv_norm),
                  full(w_kv), full(freq), full(sign)],
        out_specs=[row(d), row(d), row(d), row(LANES), row(d)],
        out_shape=[big, big, big, jax.ShapeDtypeStruct((n, LANES), BF16), big],
        compiler_params=_params("parallel"),
        name="mla_proj",
    )(x, pos, g, w_in, q_norm, w_q, kv_norm, w_kv, freq, sign)


def _attn_kernel(qn_ref, qp_ref, kn_ref, kp_ref, v_ref, o_ref, m_s, acc_s, bias_s):
    qi = pl.program_id(2)
    tq = qn_ref.shape[0]
    heads = qn_ref.shape[1] // LANES
    c = (QK_DIM ** -0.5) * math.log2(math.e)
    hs = lambda h: slice(h * LANES, (h + 1) * LANES)
    q = [jnp.concatenate([qn_ref[:, hs(h)], qp_ref[:, hs(h)]], axis=1) for h in range(heads)]
    ones = jnp.ones((tq, LANES), BF16)
    m_s[...] = jnp.full_like(m_s, -jnp.inf)
    acc_s[...] = jnp.zeros_like(acc_s)

    @pl.when((pl.program_id(0) == 0) & (pl.program_id(1) == 0) & (qi == 0))
    def _():
        q_idx = lax.broadcasted_iota(jnp.int32, (tq, tq), 0)
        k_idx = lax.broadcasted_iota(jnp.int32, (tq, tq), 1)
        bias_s[...] = jnp.where(k_idx <= q_idx, 0.0, -jnp.inf)

    def chunk(kj, masked):
        k0 = pl.multiple_of(kj * tq, tq)
        kp = kp_ref[pl.ds(k0, tq), :]
        for h in range(heads):
            k = jnp.concatenate([kn_ref[pl.ds(k0, tq), hs(h)], kp], axis=1)
            s = lax.dot_general(q[h], k, (((1,), (1,)), ((), ())), preferred_element_type=F32)
            if masked:
                s = s + bias_s[...]
            slabs = [s[:, hs(j)] for j in range(tq // LANES)]
            part = functools.reduce(jnp.maximum, slabs)
            m_prev = m_s[h]
            m_new = jnp.maximum(m_prev, jnp.max(part, axis=1, keepdims=True))
            p = jnp.concatenate([jnp.exp2((sl - m_new) * c) for sl in slabs], axis=1)
            alpha = jnp.exp2((m_prev - m_new) * c)
            vv = jnp.concatenate([v_ref[pl.ds(k0, tq), hs(h)], ones], axis=1)
            acc = acc_s[h]
            acc = jnp.concatenate([acc[:, :LANES] * alpha, acc[:, LANES:] * alpha], axis=1)
            acc_s[h] = acc + _dot(p.astype(BF16), vv)
            m_s[h] = m_new

    def body(kp2, carry):
        chunk(2 * kp2, False)
        chunk(2 * kp2 + 1, False)
        return carry

    lax.fori_loop(0, qi // 2, body, 0)

    @pl.when(qi % 2 == 1)
    def _():
        chunk(qi - 1, False)

    chunk(qi, True)
    for h in range(heads):
        acc = acc_s[h]
        o_ref[:, hs(h)] = (acc[:, :LANES] / acc[:, LANES:]).astype(o_ref.dtype)


def attention(q_nope, q_pe, k_nope, k_pe, v, *, batch, seq, tq, heads):
    n, d = q_nope.shape
    nq = seq // tq
    w = heads * LANES
    qspec = pl.BlockSpec((tq, w), lambda b, h, i: (b * nq + i, h))
    kspec = pl.BlockSpec((seq, w), lambda b, h, i: (b, h))
    return pl.pallas_call(
        _attn_kernel,
        grid=(batch, MLA_HEADS // heads, nq),
        in_specs=[qspec, qspec, kspec,
                  pl.BlockSpec((seq, LANES), lambda b, h, i: (b, 0)),
                  kspec],
        out_specs=qspec,
        out_shape=jax.ShapeDtypeStruct((n, d), BF16),
        scratch_shapes=[pltpu.VMEM((heads, tq, LANES), F32),
                        pltpu.VMEM((heads, tq, 2 * LANES), F32),
                        pltpu.VMEM((tq, tq), F32)],
        compiler_params=_params("arbitrary", "arbitrary", "arbitrary"),
        name="attention",
    )(q_nope, q_pe, k_nope, k_pe, v)


def _out_router_kernel(o_ref, w_ref, r_ref, g_ref, wr_ref, x_ref, hn_ref, route_ref, cnt_ref, carry):
    i = pl.program_id(0)
    tm = o_ref.shape[0]

    @pl.when(i == 0)
    def _():
        carry[...] = jnp.zeros_like(carry)

    x = r_ref[...] + _dot(o_ref[...], w_ref[...])
    x_ref[...] = x
    hn = _rms(x, g_ref[...])
    hn_ref[...] = hn
    hn_hi = hn.astype(BF16)
    hn_lo = (hn - hn_hi.astype(F32)).astype(BF16)
    parts = _dot(hn_hi, wr_ref[...]) + _dot(hn_lo, wr_ref[...])
    logits = parts[:, :LANES] + parts[:, LANES:]
    lane = lax.broadcasted_iota(jnp.int32, (tm, LANES), 1)
    lane_f = lane.astype(F32)
    neg = -jnp.inf
    logits = jnp.where(lane < N_EXPERTS, logits, neg)
    m1 = jnp.max(logits, axis=1, keepdims=True)
    e1 = jnp.min(jnp.where(logits == m1, lane_f, float(LANES)), axis=1, keepdims=True)
    rest = jnp.where(lane_f == e1, neg, logits)
    m2 = jnp.max(rest, axis=1, keepdims=True)
    e2 = jnp.min(jnp.where(rest == m2, lane_f, float(LANES)), axis=1, keepdims=True)
    z = jnp.exp(m2 - m1)
    g1 = 1.0 / (1.0 + z)
    g2 = z / (1.0 + z)
    sel1 = lane_f == e1
    sel2 = lane_f == e2
    sel = jnp.where(sel1 | sel2, 1.0, 0.0)
    rr = lax.broadcasted_iota(jnp.int32, (tm, tm), 0)
    cc = lax.broadcasted_iota(jnp.int32, (tm, tm), 1)
    tril = jnp.where(cc < rr, 1.0, 0.0).astype(BF16)
    before = _dot(tril, sel.astype(BF16)) + carry[...]
    rank1 = jnp.sum(jnp.where(sel1, before, 0.0), axis=1, keepdims=True)
    rank2 = jnp.sum(jnp.where(sel2, before, 0.0), axis=1, keepdims=True)
    total = carry[...] + jnp.sum(sel, axis=0, keepdims=True)
    carry[...] = total
    cnt_ref[...] = jnp.broadcast_to(total, cnt_ref.shape)
    route = jnp.zeros((tm, LANES), F32)
    for col, val in enumerate((e1, e2, g1, g2, rank1, rank2)):
        route = jnp.where(lane == col, val, route)
    route_ref[...] = route


def out_router(o, w_out, res, g, w_router, *, tm):
    n, d = res.shape
    row = lambda w: pl.BlockSpec((tm, w), lambda i: (i, 0))
    full = lambda a: pl.BlockSpec(a.shape, lambda i: (0, 0))
    return pl.pallas_call(
        _out_router_kernel,
        grid=(n // tm,),
        in_specs=[row(d), full(w_out), row(d), full(g), full(w_router)],
        out_specs=[row(d), row(d), row(LANES), pl.BlockSpec((SUBLANES, LANES), lambda i: (0, 0))],
        out_shape=[jax.ShapeDtypeStruct((n, d), F32), jax.ShapeDtypeStruct((n, d), F32),
                   jax.ShapeDtypeStruct((n, LANES), F32),
                   jax.ShapeDtypeStruct((SUBLANES, LANES), F32)],
        scratch_shapes=[pltpu.VMEM((1, LANES), F32)],
        compiler_params=_params("arbitrary"),
        name="out_router",
    )(o, w_out, res, g, w_router)


GATHER_SLOTS = 2
MOE_ROW_PARTS = 2


def _row_copy(src_hbm, row, dst, r, sem):
    return pltpu.make_async_copy(src_hbm.at[pl.ds(row, 1), :], dst.at[pl.ds(r, 1), :], sem)


def _moe_ffn_kernel(src_ref, be_ref, valid_ref, nu_ref, hn_hbm, wg_ref, wu_ref, wd_ref, o_ref, xg, xb, sems,
                    *, per_step):
    i = pl.program_id(0)
    j = pl.program_id(1)
    tm = o_ref.shape[0]
    rows = xg.shape[1]
    n_used = nu_ref[0]
    slot = i % GATHER_SLOTS

    def start_row(blk, sl, r):
        tok = src_ref[blk * tm + jnp.minimum(r, tm - 1)]
        _row_copy(hn_hbm, tok, xg.at[sl], r, sems.at[sl]).start()

    @pl.when((i == 0) & (j == 0))
    def _():
        def first(r, c):
            start_row(0, 0, r)
            return c
        lax.fori_loop(0, rows, first, 0, unroll=8)

    @pl.when((j == 0) & (i <= n_used))
    def _():
        pltpu.make_async_copy(hn_hbm.at[pl.ds(0, rows), :], xg.at[slot], sems.at[slot]).wait()

    @pl.when((j == 0) & (i < n_used))
    def _():
        xb[...] = xg[slot, :tm, :].astype(BF16)

    @pl.when(j == 0)
    def _():
        o_ref[...] = jnp.zeros_like(o_ref)

    def step(m):
        nxt = jnp.minimum(i + 1, n_used - 1)
        for u in range(per_step):
            start_row(nxt, (i + 1) % GATHER_SLOTS, j * per_step + u)
        o_ref[:m, :] += _swiglu_acc(xb[:m, :], wg_ref, wu_ref, wd_ref)

    quarter = tm // MOE_ROW_PARTS
    parts = (valid_ref[i] + quarter - 1) // quarter
    for k in range(1, MOE_ROW_PARTS + 1):
        @pl.when((i < n_used) & (parts == k))
        def _(k=k):
            step(k * quarter)


def moe_ffn(src_tok, block_e, block_valid, n_used, hn, wg, wu, wd, *, tm, tc):
    d = hn.shape[1]
    p = src_tok.shape[0]
    nj = wg.shape[2] // tc
    per_step = -(-tm // nj)
    while (per_step * nj) % SUBLANES:
        per_step += 1

    def blk(i, nu):
        return jnp.minimum(i, nu[0] - 1)

    def col(i, j, nu):
        return jnp.where(i < nu[0], j, nj - 1)

    return pl.pallas_call(
        functools.partial(_moe_ffn_kernel, per_step=per_step),
        grid_spec=pltpu.PrefetchScalarGridSpec(
            num_scalar_prefetch=4,
            grid=(p // tm, nj),
            in_specs=[pl.BlockSpec(memory_space=pl.ANY),
                      pl.BlockSpec((None, d, tc), lambda i, j, s, be, bv, nu: (be[blk(i, nu)], 0, col(i, j, nu))),
                      pl.BlockSpec((None, d, tc), lambda i, j, s, be, bv, nu: (be[blk(i, nu)], 0, col(i, j, nu))),
                      pl.BlockSpec((None, tc, d), lambda i, j, s, be, bv, nu: (be[blk(i, nu)], col(i, j, nu), 0))],
            out_specs=pl.BlockSpec((tm, d), lambda i, j, s, be, bv, nu: (i, 0)),
            scratch_shapes=[pltpu.VMEM((GATHER_SLOTS, per_step * nj, d), F32), pltpu.VMEM((tm, d), BF16),
                            pltpu.SemaphoreType.DMA((GATHER_SLOTS,))]),
        out_shape=jax.ShapeDtypeStruct((p, d), F32),
        compiler_params=_params("arbitrary", "arbitrary"),
        name="moe_ffn",
    )(src_tok, block_e, block_valid, n_used, hn, wg, wu, wd)


def _combine_kernel(dest_ref, x_ref, route_ref, g_ref, y_hbm, o_ref, ya, yb, sems):
    i = pl.program_id(0)
    tm = o_ref.shape[0]
    slot = i % 2

    def copies(blk, sl, r):
        t2 = 2 * (blk * tm + r)
        return (_row_copy(y_hbm, dest_ref[t2], ya.at[sl], r, sems.at[0, sl]),
                _row_copy(y_hbm, dest_ref[t2 + 1], yb.at[sl], r, sems.at[1, sl]))

    def issue(blk, sl):
        def body(r, c):
            ca, cb = copies(blk, sl, r)
            ca.start()
            cb.start()
            return c
        lax.fori_loop(0, tm, body, 0, unroll=8)

    @pl.when(i == 0)
    def _():
        issue(0, 0)

    @pl.when(i + 1 < pl.num_programs(0))
    def _():
        issue(i + 1, 1 - slot)

    pltpu.make_async_copy(y_hbm.at[pl.ds(0, tm), :], ya.at[slot], sems.at[0, slot]).wait()
    pltpu.make_async_copy(y_hbm.at[pl.ds(0, tm), :], yb.at[slot], sems.at[1, slot]).wait()
    route = route_ref[...]
    x = x_ref[...] + (ya[slot] * route[:, 2:3] + yb[slot] * route[:, 3:4])
    o_ref[...] = _rms(x, g_ref[...])


def combine(dest, x, route, g, y_buf, *, tm):
    n, d = x.shape
    return pl.pallas_call(
        _combine_kernel,
        grid_spec=pltpu.PrefetchScalarGridSpec(
            num_scalar_prefetch=1,
            grid=(n // tm,),
            in_specs=[pl.BlockSpec((tm, d), lambda i, dst: (i, 0)),
                      pl.BlockSpec((tm, LANES), lambda i, dst: (i, 0)),
                      pl.BlockSpec((1, d), lambda i, dst: (0, 0)),
                      pl.BlockSpec(memory_space=pl.ANY)],
            out_specs=pl.BlockSpec((tm, d), lambda i, dst: (i, 0)),
            scratch_shapes=[pltpu.VMEM((2, tm, d), F32), pltpu.VMEM((2, tm, d), F32),
                            pltpu.SemaphoreType.DMA((2, 2))]),
        out_shape=jax.ShapeDtypeStruct((n, d), F32),
        compiler_params=_params("arbitrary"),
        name="moe_combine",
    )(dest, x, route, g, y_buf)


def _spread_rope(w):
    half = QK_ROPE // 2
    z = jnp.zeros(w.shape[:-1] + (half,), w.dtype)
    return jnp.concatenate([w[..., :half], z, w[..., half:], z], axis=-1)


def _tile(n, want):
    t = min(n, want)
    assert n % t == 0, (n, t)
    return t


def kernel(x, positions, a_norm_mix, rg_w_in, rg_conv_w, rg_conv_b, rg_w_a, rg_b_a, rg_w_x, rg_b_x, rg_lambda, rg_w_out, a_norm_ffn, ff_w_gate, ff_w_up, ff_w_down, b_norm_mix, mla_w_in, mla_q_norm, mla_w_q_up, mla_kv_norm, mla_w_kv_up, mla_w_out, b_norm_ffn, moe_w_router, moe_w_gate, moe_w_up, moe_w_down, final_norm):
    batch, seq, d = x.shape
    n = batch * seq
    assert a_norm_mix.shape[0] == 1 and b_norm_mix.shape[0] == 1
    xf = x.reshape(n, d)
    vec = lambda a: a.reshape(1, -1).astype(F32)

    wax = jnp.concatenate([rg_w_a[0], rg_w_x[0]], axis=-1).astype(BF16)
    x1 = rglru_layer(xf, vec(a_norm_mix[0]), rg_w_in[0].astype(BF16), rg_conv_w[0], vec(rg_conv_b[0]), wax,
                     vec(rg_b_a[0]), vec(rg_b_x[0]), vec(rg_lambda[0]), rg_w_out[0].astype(BF16),
                     seq=seq, t=_tile(seq, 256))
    x2 = ffn_dense(x1, vec(a_norm_ffn[0]), ff_w_gate[0], ff_w_up[0], ff_w_down[0],
                   tm=_tile(n, FF_TM), tc=FF_TC)

    w_in = mla_w_in[0]
    w_in_p = jnp.concatenate([w_in[:, :Q_LORA + KV_LORA], _spread_rope(w_in[:, Q_LORA + KV_LORA:])],
                             axis=1).astype(BF16)
    wq = mla_w_q_up[0].reshape(Q_LORA, MLA_HEADS, QK_DIM)
    wq_p = jnp.concatenate([wq[..., :QK_NOPE].reshape(Q_LORA, -1),
                            _spread_rope(wq[..., QK_NOPE:]).reshape(Q_LORA, -1)], axis=1).astype(BF16)
    wkv = mla_w_kv_up[0].reshape(KV_LORA, MLA_HEADS, QK_NOPE + V_DIM)
    wkv_p = jnp.concatenate([wkv[..., :QK_NOPE].reshape(KV_LORA, -1),
                             wkv[..., QK_NOPE:].reshape(KV_LORA, -1)], axis=1).astype(BF16)
    inv_freq = 1.0 / (ROPE_THETA ** (jnp.arange(0, QK_ROPE, 2, dtype=F32) / QK_ROPE))
    freq = _spread_rope(jnp.concatenate([inv_freq, inv_freq])[None, :])
    ones = jnp.ones((1, QK_ROPE // 2), F32)
    sign = _spread_rope(jnp.concatenate([-ones, ones], axis=1))
    pos = positions.reshape(n, 1).astype(jnp.int32)
    q_nope, q_pe, k_nope, k_pe, v = mla_proj(
        x2, pos, vec(b_norm_mix[0]), w_in_p, vec(mla_q_norm[0]), wq_p, vec(mla_kv_norm[0]), wkv_p,
        freq, sign, tm=_tile(n, 512))
    o = attention(q_nope, q_pe, k_nope, k_pe, v, batch=batch, seq=seq, tq=_tile(seq, 512), heads=8)
    w_router = jnp.pad(moe_w_router[0].astype(F32), ((0, 0), (0, LANES - N_EXPERTS)))
    wr_hi = w_router.astype(BF16)
    w_router = jnp.concatenate([wr_hi, (w_router - wr_hi.astype(F32)).astype(BF16)], axis=1)
    x3, hn, route, cnt = out_router(o, mla_w_out[0].astype(BF16), x2, vec(b_norm_ffn[0]), w_router,
                                    tm=_tile(n, 512))

    tm = _tile(2 * n, FF_TM)
    counts = cnt[0, :N_EXPERTS].astype(jnp.int32)
    padded = ((counts + tm - 1) // tm) * tm
    pad_end = jnp.cumsum(padded)
    pad_start = pad_end - padded
    experts = route[:, 0:2].astype(jnp.int32)
    ranks = route[:, 4:6].astype(jnp.int32)
    dest = (pad_start[experts] + ranks).reshape(-1)
    p = 2 * n + N_EXPERTS * tm
    n_blocks = p // tm
    src_tok = jnp.zeros((p,), jnp.int32).at[dest].set(jnp.arange(2 * n, dtype=jnp.int32) // 2,
                                                      unique_indices=True, mode='promise_in_bounds')
    block_start = jnp.arange(n_blocks, dtype=jnp.int32) * tm
    block_e = jnp.minimum(jnp.sum(block_start[:, None] >= pad_end[None, :], axis=1),
                          N_EXPERTS - 1).astype(jnp.int32)
    block_valid = jnp.clip((pad_start + counts)[block_e] - block_start, 0, tm).astype(jnp.int32)
    n_used = (pad_end[-1:] // tm).astype(jnp.int32)

    y_buf = moe_ffn(src_tok, block_e, block_valid, n_used, hn, moe_w_gate[0], moe_w_up[0], moe_w_down[0],
                    tm=tm, tc=FF_TC)
    out = combine(dest, x3, route, vec(final_norm), y_buf, tm=_tile(n, 256))
    return out.reshape(batch, seq, d)
```

```python
import functools
import math

import jax
import jax.numpy as jnp
from jax import lax
from jax.experimental import pallas as pl
from jax.experimental.pallas import tpu as pltpu

EPS = 1e-6
RG_HEADS = 16
RG_BLOCK = 128
CONV_W = 4
RG_C = 8.0
MLA_HEADS = 16
Q_LORA = 512
KV_LORA = 512
QK_NOPE = 128
QK_ROPE = 64
V_DIM = 128
QK_DIM = QK_NOPE + QK_ROPE
ROPE_THETA = 10000.0
N_EXPERTS = 8
LANES = 128
SUBLANES = 8
VMEM_LIMIT_BYTES = 56 * 1024 * 1024
FF_TC = 256
FF_TM = 1024

BF16 = jnp.bfloat16
F32 = jnp.float32


def _params(*sem):
    return pltpu.CompilerParams(dimension_semantics=sem, vmem_limit_bytes=VMEM_LIMIT_BYTES)


def _rms(x, g):
    return x * lax.rsqrt(jnp.mean(x * x, axis=-1, keepdims=True) + EPS) * g


def _dot(a, b):
    return jnp.dot(a, b, preferred_element_type=F32)


def _sigmoid(x):
    return 1.0 / (1.0 + jnp.exp(-x))


def _sigmoid_tanh(x):
    return 0.5 * jnp.tanh(0.5 * x) + 0.5


def _gelu_tanh(x):
    c = math.sqrt(2.0 / math.pi)
    return 0.5 * x * (1.0 + jnp.tanh(c * (x + 0.044715 * (x * x * x))))


def _softplus(z):
    return jnp.maximum(z, 0.0) + jnp.log1p(jnp.exp(-jnp.abs(z)))


def _rglru_layer_kernel(xn_ref, xp_ref, g_ref, win_hbm, cw_ref, cb_ref, wax_ref, ba_ref, bx_ref, lam_ref, wout_hbm,
                        o_ref, win_s, wout_s, u_s, y_s, gate_s, hn_s, xbuf, a_s, b_s, h_s, sem, *, ns):
    q = pl.program_id(0)
    s = q % ns
    t, d = xn_ref.shape
    cur = q % 2
    nxt = (q + 1) % 2

    def project(x_ref):
        return _dot(_rms(x_ref[...], g_ref[...]).astype(BF16), win_s[...]).astype(BF16)

    @pl.when(q == 0)
    def _():
        cin = pltpu.make_async_copy(win_hbm, win_s, sem.at[0])
        cout = pltpu.make_async_copy(wout_hbm, wout_s, sem.at[1])
        cin.start()
        cout.start()
        cin.wait()
        cout.wait()
        u_s[0] = project(xp_ref)
        y_s[...] = jnp.zeros_like(y_s)

    @pl.when(s == 0)
    def _():
        xbuf[0:SUBLANES, :] = jnp.zeros((SUBLANES, d), F32)
        h_s[...] = jnp.zeros_like(h_s)

    @pl.when(s > 0)
    def _():
        xbuf[0:SUBLANES, :] = xbuf[t:t + SUBLANES, :]

    xbuf[SUBLANES:t + SUBLANES, :] = u_s[cur, :, d:].astype(F32)
    gate_s[...] = u_s[cur, :, :d]
    hn_s[...] = _rms(xn_ref[...], g_ref[...]).astype(BF16)
    pw = u_s.shape[2] // RG_HEADS
    ow = 2 * d // RG_HEADS

    row = lax.broadcasted_iota(jnp.int32, (t // SUBLANES, SUBLANES, RG_BLOCK), 1)
    for h in range(RG_HEADS):
        c0, c1 = h * RG_BLOCK, (h + 1) * RG_BLOCK
        xc = cb_ref[:, c0:c1] + jnp.zeros((t, RG_BLOCK), F32)
        for k in range(CONV_W):
            sh = CONV_W - 1 - k
            xc = xc + xbuf[SUBLANES - sh:SUBLANES - sh + t, c0:c1] * cw_ref[k:k + 1, c0:c1]
        gg = _dot(xc.astype(BF16), wax_ref[h])
        u_s[nxt, :, h * pw:(h + 1) * pw] = _dot(hn_s[...], win_s[:, h * pw:(h + 1) * pw]).astype(BF16)
        if h % 2 == 0:
            oc = slice((h // 2) * ow, (h // 2 + 1) * ow)
            o_ref[:, oc] = xp_ref[:, oc] + _dot(y_s[nxt], wout_s[:, oc])
        r = _sigmoid_tanh(gg[:, :RG_BLOCK] + ba_ref[:, c0:c1])
        i = _sigmoid_tanh(gg[:, RG_BLOCK:] + bx_ref[:, c0:c1])
        log_a = -RG_C * r * _softplus(-lam_ref[:, c0:c1])
        a = jnp.exp(log_a)
        om = 1.0 - a * a
        b = jnp.where(om > 0.0, om * lax.rsqrt(om), 0.0) * i * xc
        a = a.reshape(t // SUBLANES, SUBLANES, RG_BLOCK)
        b = b.reshape(t // SUBLANES, SUBLANES, RG_BLOCK)
        for sft in (1, 2, 4):
            a_sh = pltpu.roll(a, sft, axis=1)
            b_sh = pltpu.roll(b, sft, axis=1)
            m = row >= sft
            b = jnp.where(m, a * b_sh + b, b)
            a = jnp.where(m, a * a_sh, a)
        a_s[:, c0:c1] = a.reshape(t, RG_BLOCK)
        b_s[:, c0:c1] = b.reshape(t, RG_BLOCK)

    def group(gi, hc):
        r0 = pl.multiple_of(gi * SUBLANES, SUBLANES)
        rows = b_s[pl.ds(r0, SUBLANES), :] + a_s[pl.ds(r0, SUBLANES), :] * hc
        b_s[pl.ds(r0, SUBLANES), :] = rows
        return rows[SUBLANES - 1:SUBLANES, :]

    h_s[...] = lax.fori_loop(0, t // SUBLANES, group, h_s[...])
    y_s[cur] = (b_s[...] * _gelu_tanh(gate_s[...].astype(F32))).astype(BF16)


def rglru_layer(x, g, w_in, conv_w, conv_b, wax, b_a, b_x, lam, w_out, *, seq, t):
    n, d = x.shape
    ns = seq // t
    total = n // t
    vec = pl.BlockSpec((1, d), lambda q: (0, 0))
    hbm = pl.BlockSpec(memory_space=pl.ANY)
    return pl.pallas_call(
        functools.partial(_rglru_layer_kernel, ns=ns),
        grid=(total + 1,),
        in_specs=[pl.BlockSpec((t, d), lambda q: (jnp.minimum(q + 1, total - 1), 0)),
                  pl.BlockSpec((t, d), lambda q: (jnp.maximum(q - 1, 0), 0)),
                  vec, hbm,
                  pl.BlockSpec((CONV_W, d), lambda q: (0, 0)),
                  vec,
                  pl.BlockSpec((RG_HEADS, RG_BLOCK, 2 * RG_BLOCK), lambda q: (0, 0, 0)),
                  vec, vec, vec, hbm],
        out_specs=pl.BlockSpec((t, d), lambda q: (jnp.maximum(q - 1, 0), 0)),
        out_shape=jax.ShapeDtypeStruct((n, d), F32),
        scratch_shapes=[pltpu.VMEM((d, 2 * d), BF16),
                        pltpu.VMEM((d, d), BF16),
                        pltpu.VMEM((2, t, 2 * d), BF16),
                        pltpu.VMEM((2, t, d), BF16),
                        pltpu.VMEM((t, d), BF16),
                        pltpu.VMEM((t, d), BF16),
                        pltpu.VMEM((t + SUBLANES, d), F32),
                        pltpu.VMEM((t, d), F32),
                        pltpu.VMEM((t, d), F32),
                        pltpu.VMEM((1, d), F32),
                        pltpu.SemaphoreType.DMA((2,))],
        compiler_params=_params("arbitrary"),
        name="rglru_layer",
    )(x, x, g, w_in, conv_w, conv_b, wax, b_a, b_x, lam, w_out)


def _swiglu_acc(h, wg_ref, wu_ref, wd_ref):
    g = _dot(h, wg_ref[...].astype(BF16))
    u = _dot(h, wu_ref[...].astype(BF16))
    a = (g * _sigmoid(g) * u).astype(BF16)
    return _dot(a, wd_ref[...].astype(BF16))


def _ffn_kernel(x_ref, g_ref, wg_ref, wu_ref, wd_ref, o_ref, hn_ref):
    @pl.when(pl.program_id(1) == 0)
    def _():
        x = x_ref[...]
        hn_ref[...] = _rms(x, g_ref[...]).astype(BF16)
        o_ref[...] = x

    o_ref[...] += _swiglu_acc(hn_ref[...], wg_ref, wu_ref, wd_ref)


def ffn_dense(x, g, wg, wu, wd, *, tm, tc):
    n, d = x.shape
    f = wg.shape[1]
    return pl.pallas_call(
        _ffn_kernel,
        grid=(n // tm, f // tc),
        in_specs=[pl.BlockSpec((tm, d), lambda i, j: (i, 0)),
                  pl.BlockSpec((1, d), lambda i, j: (0, 0)),
                  pl.BlockSpec((d, tc), lambda i, j: (0, j)),
                  pl.BlockSpec((d, tc), lambda i, j: (0, j)),
                  pl.BlockSpec((tc, d), lambda i, j: (j, 0))],
        out_specs=pl.BlockSpec((tm, d), lambda i, j: (i, 0)),
        out_shape=jax.ShapeDtypeStruct((n, d), F32),
        scratch_shapes=[pltpu.VMEM((tm, d), BF16)],
        compiler_params=_params("parallel", "arbitrary"),
        name="ffn_dense",
    )(x, g, wg, wu, wd)


MLA_SPLIT = 2


def _mla_proj_kernel(x_ref, pos_ref, g_ref, win_ref, qn_ref, wq_ref, kvn_ref, wkv_ref,
                     freq_ref, sign_ref, qnope_ref, qpe_ref, knope_ref, kpe_ref, v_ref):
    tm, d = x_ref.shape
    ts = tm // MLA_SPLIT
    subs = [slice(k * ts, (k + 1) * ts) for k in range(MLA_SPLIT)]

    def rope(xs, cos, sin):
        return xs * cos + pltpu.roll(xs, LANES // 2, axis=1) * sin

    us = [_dot(_rms(x_ref[rows, :], g_ref[...]).astype(BF16), win_ref[...]) for rows in subs]
    trig = []
    for rows in subs:
        ang = pos_ref[rows, :].astype(F32) * freq_ref[...]
        trig.append((jnp.cos(ang), jnp.sin(ang) * sign_ref[...]))
    qs = [_dot(_rms(u[:, :Q_LORA], qn_ref[...]).astype(BF16), wq_ref[...]) for u in us]
    kvs = [_dot(_rms(u[:, Q_LORA:Q_LORA + KV_LORA], kvn_ref[...]).astype(BF16), wkv_ref[...]) for u in us]
    for rows, u, (cos, sin), q, kv in zip(subs, us, trig, qs, kvs):
        kpe_ref[rows, :] = rope(u[:, Q_LORA + KV_LORA:], cos, sin).astype(BF16)
        qnope_ref[rows, :] = q[:, :d].astype(BF16)
        for h in range(MLA_HEADS):
            c0, c1 = h * LANES, (h + 1) * LANES
            qpe_ref[rows, c0:c1] = rope(q[:, d + c0:d + c1], cos, sin).astype(BF16)
        knope_ref[rows, :] = kv[:, :d].astype(BF16)
        v_ref[rows, :] = kv[:, d:].astype(BF16)


def mla_proj(x, pos, g, w_in, q_norm, w_q, kv_norm, w_kv, freq, sign, *, tm):
    n, d = x.shape
    row = lambda w: pl.BlockSpec((tm, w), lambda i: (i, 0))
    full = lambda a: pl.BlockSpec(a.shape, lambda i: (0, 0))
    big = jax.ShapeDtypeStruct((n, d), BF16)
    return pl.pallas_call(
        _mla_proj_kernel,
        grid=(n // tm,),
        in_specs=[row(d), row(1), full(g), full(w_in), full(q_norm), full(w_q), full(kv_norm),
                  full(w_kv), full(freq), full(sign)],
        out_specs=[row(d), row(d), row(d), row(LANES), row(d)],
        out_shape=[big, big, big, jax.ShapeDtypeStruct((n, LANES), BF16), big],
        compiler_params=_params("parallel"),
        name="mla_proj",
    )(x, pos, g, w_in, q_norm, w_q, kv_norm, w_kv, freq, sign)


def _attn_kernel(qn_ref, qp_ref, kn_ref, kp_ref, v_ref, o_ref, m_s, acc_s, bias_s):
    qi = pl.program_id(2)
    tq = qn_ref.shape[0]
    heads = qn_ref.shape[1] // LANES
    c = (QK_DIM ** -0.5) * math.log2(math.e)
    hs = lambda h: slice(h * LANES, (h + 1) * LANES)
    q = [jnp.concatenate([qn_ref[:, hs(h)], qp_ref[:, hs(h)]], axis=1) for h in range(heads)]
    ones = jnp.ones((tq, LANES), BF16)
    m_s[...] = jnp.full_like(m_s, -jnp.inf)
    acc_s[...] = jnp.zeros_like(acc_s)

    @pl.when((pl.program_id(0) == 0) & (pl.program_id(1) == 0) & (qi == 0))
    def _():
        q_idx = lax.broadcasted_iota(jnp.int32, (tq, tq), 0)
        k_idx = lax.broadcasted_iota(jnp.int32, (tq, tq), 1)
        bias_s[...] = jnp.where(k_idx <= q_idx, 0.0, -jnp.inf)

    def chunk(kj, masked):
        k0 = pl.multiple_of(kj * tq, tq)
        kp = kp_ref[pl.ds(k0, tq), :]
        for h in range(heads):
            k = jnp.concatenate([kn_ref[pl.ds(k0, tq), hs(h)], kp], axis=1)
            s = lax.dot_general(q[h], k, (((1,), (1,)), ((), ())), preferred_element_type=F32)
            if masked:
                s = s + bias_s[...]
            slabs = [s[:, hs(j)] for j in range(tq // LANES)]
            part = functools.reduce(jnp.maximum, slabs)
            m_prev = m_s[h]
            m_new = jnp.maximum(m_prev, jnp.max(part, axis=1, keepdims=True))
            p = jnp.concatenate([jnp.exp2((sl - m_new) * c) for sl in slabs], axis=1)
            alpha = jnp.exp2((m_prev - m_new) * c)
            vv = jnp.concatenate([v_ref[pl.ds(k0, tq), hs(h)], ones], axis=1)
            acc = acc_s[h]
            acc = jnp.concatenate([acc[:, :LANES] * alpha, acc[:, LANES:] * alpha], axis=1)
            acc_s[h] = acc + _dot(p.astype(BF16), vv)
            m_s[h] = m_new

    def body(kp2, carry):
        chunk(2 * kp2, False)
        chunk(2 * kp2 + 1, False)
        return carry

    lax.fori_loop(0, qi // 2, body, 0)

    @pl.when(qi % 2 == 1)
    def _():
        chunk(qi - 1, False)

    chunk(qi, True)
    for h in range(heads):
        acc = acc_s[h]
        o_ref[:, hs(h)] = (acc[:, :LANES] / acc[:, LANES:]).astype(o_ref.dtype)


def attention(q_nope, q_pe, k_nope, k_pe, v, *, batch, seq, tq, heads):
    n, d = q_nope.shape
    nq = seq // tq
    w = heads * LANES
    qspec = pl.BlockSpec((tq, w), lambda b, h, i: (b * nq + i, h))
    kspec = pl.BlockSpec((seq, w), lambda b, h, i: (b, h))
    return pl.pallas_call(
        _attn_kernel,
        grid=(batch, MLA_HEADS // heads, nq),
        in_specs=[qspec, qspec, kspec,
                  pl.BlockSpec((seq, LANES), lambda b, h, i: (b, 0)),
                  kspec],
        out_specs=qspec,
        out_shape=jax.ShapeDtypeStruct((n, d), BF16),
        scratch_shapes=[pltpu.VMEM((heads, tq, LANES), F32),
                        pltpu.VMEM((heads, tq, 2 * LANES), F32),
                        pltpu.VMEM((tq, tq), F32)],
        compiler_params=_params("arbitrary", "arbitrary", "arbitrary"),
        name="attention",
    )(q_nope, q_pe, k_nope, k_pe, v)


ROUTER_SPLIT = 4


def _out_router_kernel(o_ref, w_ref, r_ref, g_ref, wr_ref, x_ref, hn_ref, route_ref, cnt_ref, carry):
    i = pl.program_id(0)
    tm = o_ref.shape[0]
    ts = tm // ROUTER_SPLIT
    subs = [slice(k * ts, (k + 1) * ts) for k in range(ROUTER_SPLIT)]

    @pl.when(i == 0)
    def _():
        carry[...] = jnp.zeros_like(carry)

    xs = [r_ref[rows, :] + _dot(o_ref[rows, :], w_ref[...]) for rows in subs]
    hns = []
    for rows, x in zip(subs, xs):
        x_ref[rows, :] = x
        hn = _rms(x, g_ref[...])
        hn_ref[rows, :] = hn
        hns.append(hn)
    logit = []
    for hn in hns:
        hn_hi = hn.astype(BF16)
        hn_lo = (hn - hn_hi.astype(F32)).astype(BF16)
        parts = _dot(hn_hi, wr_ref[...]) + _dot(hn_lo, wr_ref[...])
        logit.append(parts[:, :LANES] + parts[:, LANES:])

    lane = lax.broadcasted_iota(jnp.int32, (ts, LANES), 1)
    lane_f = lane.astype(F32)
    rr = lax.broadcasted_iota(jnp.int32, (ts, ts), 0)
    cc = lax.broadcasted_iota(jnp.int32, (ts, ts), 1)
    tril = jnp.where(cc < rr, 1.0, 0.0).astype(BF16)
    neg = -jnp.inf
    total = carry[...]
    for rows, logits in zip(subs, logit):
        logits = jnp.where(lane < N_EXPERTS, logits, neg)
        m1 = jnp.max(logits, axis=1, keepdims=True)
        e1 = jnp.min(jnp.where(logits == m1, lane_f, float(LANES)), axis=1, keepdims=True)
        rest = jnp.where(lane_f == e1, neg, logits)
        m2 = jnp.max(rest, axis=1, keepdims=True)
        e2 = jnp.min(jnp.where(rest == m2, lane_f, float(LANES)), axis=1, keepdims=True)
        z = jnp.exp(m2 - m1)
        g1 = 1.0 / (1.0 + z)
        g2 = z / (1.0 + z)
        sel1 = lane_f == e1
        sel2 = lane_f == e2
        sel = jnp.where(sel1 | sel2, 1.0, 0.0)
        before = _dot(tril, sel.astype(BF16)) + total
        rank1 = jnp.sum(jnp.where(sel1, before, 0.0), axis=1, keepdims=True)
        rank2 = jnp.sum(jnp.where(sel2, before, 0.0), axis=1, keepdims=True)
        total = total + jnp.sum(sel, axis=0, keepdims=True)
        route = jnp.zeros((ts, LANES), F32)
        for col, val in enumerate((e1, e2, g1, g2, rank1, rank2)):
            route = jnp.where(lane == col, val, route)
        route_ref[rows, :] = route
    carry[...] = total
    cnt_ref[...] = jnp.broadcast_to(total, cnt_ref.shape)


def out_router(o, w_out, res, g, w_router, *, tm):
    n, d = res.shape
    row = lambda w: pl.BlockSpec((tm, w), lambda i: (i, 0))
    full = lambda a: pl.BlockSpec(a.shape, lambda i: (0, 0))
    return pl.pallas_call(
        _out_router_kernel,
        grid=(n // tm,),
        in_specs=[row(d), full(w_out), row(d), full(g), full(w_router)],
        out_specs=[row(d), row(d), row(LANES), pl.BlockSpec((SUBLANES, LANES), lambda i: (0, 0))],
        out_shape=[jax.ShapeDtypeStruct((n, d), F32), jax.ShapeDtypeStruct((n, d), F32),
                   jax.ShapeDtypeStruct((n, LANES), F32),
                   jax.ShapeDtypeStruct((SUBLANES, LANES), F32)],
        scratch_shapes=[pltpu.VMEM((1, LANES), F32)],
        compiler_params=_params("arbitrary"),
        name="out_router",
    )(o, w_out, res, g, w_router)


GATHER_SLOTS = 2
MOE_ROW_PARTS = 2


def _row_copy(src_hbm, row, dst, r, sem):
    return pltpu.make_async_copy(src_hbm.at[pl.ds(row, 1), :], dst.at[pl.ds(r, 1), :], sem)


def _moe_ffn_kernel(src_ref, be_ref, valid_ref, nu_ref, hn_hbm, wg_ref, wu_ref, wd_ref, o_ref, xg, xb, sems,
                    *, per_step):
    i = pl.program_id(0)
    j = pl.program_id(1)
    tm = o_ref.shape[0]
    rows = xg.shape[1]
    n_used = nu_ref[0]
    slot = i % GATHER_SLOTS

    def start_row(blk, sl, r):
        tok = src_ref[blk * tm + jnp.minimum(r, tm - 1)]
        _row_copy(hn_hbm, tok, xg.at[sl], r, sems.at[sl]).start()

    @pl.when((i == 0) & (j == 0))
    def _():
        def first(r, c):
            start_row(0, 0, r)
            return c
        lax.fori_loop(0, rows, first, 0, unroll=8)

    @pl.when((j == 0) & (i <= n_used))
    def _():
        pltpu.make_async_copy(hn_hbm.at[pl.ds(0, rows), :], xg.at[slot], sems.at[slot]).wait()

    @pl.when((j == 0) & (i < n_used))
    def _():
        xb[...] = xg[slot, :tm, :].astype(BF16)

    @pl.when(j == 0)
    def _():
        o_ref[...] = jnp.zeros_like(o_ref)

    def step(m):
        nxt = jnp.minimum(i + 1, n_used - 1)
        for u in range(per_step):
            start_row(nxt, (i + 1) % GATHER_SLOTS, j * per_step + u)
        o_ref[:m, :] += _swiglu_acc(xb[:m, :], wg_ref, wu_ref, wd_ref)

    quarter = tm // MOE_ROW_PARTS
    parts = (valid_ref[i] + quarter - 1) // quarter
    for k in range(1, MOE_ROW_PARTS + 1):
        @pl.when((i < n_used) & (parts == k))
        def _(k=k):
            step(k * quarter)


def moe_ffn(src_tok, block_e, block_valid, n_used, hn, wg, wu, wd, *, tm, tc):
    d = hn.shape[1]
    p = src_tok.shape[0]
    nj = wg.shape[2] // tc
    per_step = -(-tm // nj)
    while (per_step * nj) % SUBLANES:
        per_step += 1

    def blk(i, nu):
        return jnp.minimum(i, nu[0] - 1)

    def col(i, j, nu):
        return jnp.where(i < nu[0], j, nj - 1)

    return pl.pallas_call(
        functools.partial(_moe_ffn_kernel, per_step=per_step),
        grid_spec=pltpu.PrefetchScalarGridSpec(
            num_scalar_prefetch=4,
            grid=(p // tm, nj),
            in_specs=[pl.BlockSpec(memory_space=pl.ANY),
                      pl.BlockSpec((None, d, tc), lambda i, j, s, be, bv, nu: (be[blk(i, nu)], 0, col(i, j, nu))),
                      pl.BlockSpec((None, d, tc), lambda i, j, s, be, bv, nu: (be[blk(i, nu)], 0, col(i, j, nu))),
                      pl.BlockSpec((None, tc, d), lambda i, j, s, be, bv, nu: (be[blk(i, nu)], col(i, j, nu), 0))],
            out_specs=pl.BlockSpec((tm, d), lambda i, j, s, be, bv, nu: (i, 0)),
            scratch_shapes=[pltpu.VMEM((GATHER_SLOTS, per_step * nj, d), F32), pltpu.VMEM((tm, d), BF16),
                            pltpu.SemaphoreType.DMA((GATHER_SLOTS,))]),
        out_shape=jax.ShapeDtypeStruct((p, d), F32),
        compiler_params=_params("arbitrary", "arbitrary"),
        name="moe_ffn",
    )(src_tok, block_e, block_valid, n_used, hn, wg, wu, wd)


def _combine_kernel(dest_ref, x_ref, route_ref, g_ref, y_hbm, o_ref, ya, yb, sems):
    i = pl.program_id(0)
    tm = o_ref.shape[0]
    slot = i % 2

    def copies(blk, sl, r):
        t2 = 2 * (blk * tm + r)
        return (_row_copy(y_hbm, dest_ref[t2], ya.at[sl], r, sems.at[0, sl]),
                _row_copy(y_hbm, dest_ref[t2 + 1], yb.at[sl], r, sems.at[1, sl]))

    def issue(blk, sl):
        def body(r, c):
            ca, cb = copies(blk, sl, r)
            ca.start()
            cb.start()
            return c
        lax.fori_loop(0, tm, body, 0, unroll=8)

    @pl.when(i == 0)
    def _():
        issue(0, 0)

    @pl.when(i + 1 < pl.num_programs(0))
    def _():
        issue(i + 1, 1 - slot)

    pltpu.make_async_copy(y_hbm.at[pl.ds(0, tm), :], ya.at[slot], sems.at[0, slot]).wait()
    pltpu.make_async_copy(y_hbm.at[pl.ds(0, tm), :], yb.at[slot], sems.at[1, slot]).wait()
    route = route_ref[...]
    x = x_ref[...] + (ya[slot] * route[:, 2:3] + yb[slot] * route[:, 3:4])
    o_ref[...] = _rms(x, g_ref[...])


def combine(dest, x, route, g, y_buf, *, tm):
    n, d = x.shape
    return pl.pallas_call(
        _combine_kernel,
        grid_spec=pltpu.PrefetchScalarGridSpec(
            num_scalar_prefetch=1,
            grid=(n // tm,),
            in_specs=[pl.BlockSpec((tm, d), lambda i, dst: (i, 0)),
                      pl.BlockSpec((tm, LANES), lambda i, dst: (i, 0)),
                      pl.BlockSpec((1, d), lambda i, dst: (0, 0)),
                      pl.BlockSpec(memory_space=pl.ANY)],
            out_specs=pl.BlockSpec((tm, d), lambda i, dst: (i, 0)),
            scratch_shapes=[pltpu.VMEM((2, tm, d), F32), pltpu.VMEM((2, tm, d), F32),
                            pltpu.SemaphoreType.DMA((2, 2))]),
        out_shape=jax.ShapeDtypeStruct((n, d), F32),
        compiler_params=_params("arbitrary"),
        name="moe_combine",
    )(dest, x, route, g, y_buf)


def _spread_rope(w):
    half = QK_ROPE // 2
    z = jnp.zeros(w.shape[:-1] + (half,), w.dtype)
    return jnp.concatenate([w[..., :half], z, w[..., half:], z], axis=-1)


def _tile(n, want):
    t = min(n, want)
    assert n % t == 0, (n, t)
    return t


def kernel(x, positions, a_norm_mix, rg_w_in, rg_conv_w, rg_conv_b, rg_w_a, rg_b_a, rg_w_x, rg_b_x, rg_lambda, rg_w_out, a_norm_ffn, ff_w_gate, ff_w_up, ff_w_down, b_norm_mix, mla_w_in, mla_q_norm, mla_w_q_up, mla_kv_norm, mla_w_kv_up, mla_w_out, b_norm_ffn, moe_w_router, moe_w_gate, moe_w_up, moe_w_down, final_norm):
    batch, seq, d = x.shape
    n = batch * seq
    assert a_norm_mix.shape[0] == 1 and b_norm_mix.shape[0] == 1
    xf = x.reshape(n, d)
    vec = lambda a: a.reshape(1, -1).astype(F32)

    wax = jnp.concatenate([rg_w_a[0], rg_w_x[0]], axis=-1).astype(BF16)
    x1 = rglru_layer(xf, vec(a_norm_mix[0]), rg_w_in[0].astype(BF16), rg_conv_w[0], vec(rg_conv_b[0]), wax,
                     vec(rg_b_a[0]), vec(rg_b_x[0]), vec(rg_lambda[0]), rg_w_out[0].astype(BF16),
                     seq=seq, t=_tile(seq, 256))
    x2 = ffn_dense(x1, vec(a_norm_ffn[0]), ff_w_gate[0], ff_w_up[0], ff_w_down[0],
                   tm=_tile(n, FF_TM), tc=FF_TC)

    w_in = mla_w_in[0]
    w_in_p = jnp.concatenate([w_in[:, :Q_LORA + KV_LORA], _spread_rope(w_in[:, Q_LORA + KV_LORA:])],
                             axis=1).astype(BF16)
    wq = mla_w_q_up[0].reshape(Q_LORA, MLA_HEADS, QK_DIM)
    wq_p = jnp.concatenate([wq[..., :QK_NOPE].reshape(Q_LORA, -1),
                            _spread_rope(wq[..., QK_NOPE:]).reshape(Q_LORA, -1)], axis=1).astype(BF16)
    wkv = mla_w_kv_up[0].reshape(KV_LORA, MLA_HEADS, QK_NOPE + V_DIM)
    wkv_p = jnp.concatenate([wkv[..., :QK_NOPE].reshape(KV_LORA, -1),
                             wkv[..., QK_NOPE:].reshape(KV_LORA, -1)], axis=1).astype(BF16)
    inv_freq = 1.0 / (ROPE_THETA ** (jnp.arange(0, QK_ROPE, 2, dtype=F32) / QK_ROPE))
    freq = _spread_rope(jnp.concatenate([inv_freq, inv_freq])[None, :])
    ones = jnp.ones((1, QK_ROPE // 2), F32)
    sign = _spread_rope(jnp.concatenate([-ones, ones], axis=1))
    pos = positions.reshape(n, 1).astype(jnp.int32)
    q_nope, q_pe, k_nope, k_pe, v = mla_proj(
        x2, pos, vec(b_norm_mix[0]), w_in_p, vec(mla_q_norm[0]), wq_p, vec(mla_kv_norm[0]), wkv_p,
        freq, sign, tm=_tile(n, 512))
    o = attention(q_nope, q_pe, k_nope, k_pe, v, batch=batch, seq=seq, tq=_tile(seq, 512), heads=8)
    w_router = jnp.pad(moe_w_router[0].astype(F32), ((0, 0), (0, LANES - N_EXPERTS)))
    wr_hi = w_router.astype(BF16)
    w_router = jnp.concatenate([wr_hi, (w_router - wr_hi.astype(F32)).astype(BF16)], axis=1)
    x3, hn, route, cnt = out_router(o, mla_w_out[0].astype(BF16), x2, vec(b_norm_ffn[0]), w_router,
                                    tm=_tile(n, 512))

    tm = _tile(2 * n, FF_TM)
    counts = cnt[0, :N_EXPERTS].astype(jnp.int32)
    padded = ((counts + tm - 1) // tm) * tm
    pad_end = jnp.cumsum(padded)
    pad_start = pad_end - padded
    experts = route[:, 0:2].astype(jnp.int32)
    ranks = route[:, 4:6].astype(jnp.int32)
    dest = (pad_start[experts] + ranks).reshape(-1)
    p = 2 * n + N_EXPERTS * tm
    n_blocks = p // tm
    src_tok = jnp.zeros((p,), jnp.int32).at[dest].set(jnp.arange(2 * n, dtype=jnp.int32) // 2,
                                                      unique_indices=True, mode='promise_in_bounds')
    block_start = jnp.arange(n_blocks, dtype=jnp.int32) * tm
    block_e = jnp.minimum(jnp.sum(block_start[:, None] >= pad_end[None, :], axis=1),
                          N_EXPERTS - 1).astype(jnp.int32)
    block_valid = jnp.clip((pad_start + counts)[block_e] - block_start, 0, tm).astype(jnp.int32)
    n_used = (pad_end[-1:] // tm).astype(jnp.int32)

    y_buf = moe_ffn(src_tok, block_e, block_valid, n_used, hn, moe_w_gate[0], moe_w_up[0], moe_w_down[0],
                    tm=tm, tc=FF_TC)
    out = combine(dest, x3, route, vec(final_norm), y_buf, tm=_tile(n, 256))
    return out.reshape(batch, seq, d)
```

```python
import functools
import math

import jax
import jax.numpy as jnp
from jax import lax
from jax.experimental import pallas as pl
from jax.experimental.pallas import tpu as pltpu

EPS = 1e-6
RG_HEADS = 16
RG_BLOCK = 128
CONV_W = 4
RG_C = 8.0
MLA_HEADS = 16
Q_LORA = 512
KV_LORA = 512
QK_NOPE = 128
QK_ROPE = 64
V_DIM = 128
QK_DIM = QK_NOPE + QK_ROPE
ROPE_THETA = 10000.0
N_EXPERTS = 8
LANES = 128
SUBLANES = 8
VMEM_LIMIT_BYTES = 56 * 1024 * 1024
FF_TC = 256
FF_TM = 1024

BF16 = jnp.bfloat16
F32 = jnp.float32


def _params(*sem):
    return pltpu.CompilerParams(dimension_semantics=sem, vmem_limit_bytes=VMEM_LIMIT_BYTES)


def _rms(x, g):
    return x * lax.rsqrt(jnp.mean(x * x, axis=-1, keepdims=True) + EPS) * g


def _dot(a, b):
    return jnp.dot(a, b, preferred_element_type=F32)


def _sigmoid(x):
    return 1.0 / (1.0 + jnp.exp(-x))


def _sigmoid_tanh(x):
    return 0.5 * jnp.tanh(0.5 * x) + 0.5


def _gelu_tanh(x):
    c = math.sqrt(2.0 / math.pi)
    return 0.5 * x * (1.0 + jnp.tanh(c * (x + 0.044715 * (x * x * x))))


def _softplus(z):
    return jnp.maximum(z, 0.0) + jnp.log1p(jnp.exp(-jnp.abs(z)))


def _rglru_layer_kernel(xn_ref, xp_ref, g_ref, win_hbm, cw_ref, cb_ref, wax_ref, ba_ref, bx_ref, lam_ref, wout_hbm,
                        o_ref, win_s, wout_s, u_s, y_s, gate_s, hn_s, xbuf, a_s, b_s, h_s, sem, *, ns):
    q = pl.program_id(0)
    s = q % ns
    t, d = xn_ref.shape
    cur = q % 2
    nxt = (q + 1) % 2

    def project(x_ref):
        return _dot(_rms(x_ref[...], g_ref[...]).astype(BF16), win_s[...]).astype(BF16)

    @pl.when(q == 0)
    def _():
        cin = pltpu.make_async_copy(win_hbm, win_s, sem.at[0])
        cout = pltpu.make_async_copy(wout_hbm, wout_s, sem.at[1])
        cin.start()
        cout.start()
        cin.wait()
        cout.wait()
        u_s[0] = project(xp_ref)
        y_s[...] = jnp.zeros_like(y_s)

    @pl.when(s == 0)
    def _():
        xbuf[0:SUBLANES, :] = jnp.zeros((SUBLANES, d), F32)
        h_s[...] = jnp.zeros_like(h_s)

    @pl.when(s > 0)
    def _():
        xbuf[0:SUBLANES, :] = xbuf[t:t + SUBLANES, :]

    xbuf[SUBLANES:t + SUBLANES, :] = u_s[cur, :, d:].astype(F32)
    gate_s[...] = u_s[cur, :, :d]
    hn_s[...] = _rms(xn_ref[...], g_ref[...]).astype(BF16)
    pw = u_s.shape[2] // RG_HEADS
    ow = 2 * d // RG_HEADS

    row = lax.broadcasted_iota(jnp.int32, (t // SUBLANES, SUBLANES, RG_BLOCK), 1)
    for h in range(RG_HEADS):
        c0, c1 = h * RG_BLOCK, (h + 1) * RG_BLOCK
        xc = cb_ref[:, c0:c1] + jnp.zeros((t, RG_BLOCK), F32)
        for k in range(CONV_W):
            sh = CONV_W - 1 - k
            xc = xc + xbuf[SUBLANES - sh:SUBLANES - sh + t, c0:c1] * cw_ref[k:k + 1, c0:c1]
        gg = _dot(xc.astype(BF16), wax_ref[h])
        u_s[nxt, :, h * pw:(h + 1) * pw] = _dot(hn_s[...], win_s[:, h * pw:(h + 1) * pw]).astype(BF16)
        if h % 2 == 0:
            oc = slice((h // 2) * ow, (h // 2 + 1) * ow)
            o_ref[:, oc] = xp_ref[:, oc] + _dot(y_s[nxt], wout_s[:, oc])
        r = _sigmoid_tanh(gg[:, :RG_BLOCK] + ba_ref[:, c0:c1])
        i = _sigmoid_tanh(gg[:, RG_BLOCK:] + bx_ref[:, c0:c1])
        log_a = -RG_C * r * _softplus(-lam_ref[:, c0:c1])
        a = jnp.exp(log_a)
        om = 1.0 - a * a
        b = jnp.where(om > 0.0, om * lax.rsqrt(om), 0.0) * i * xc
        a = a.reshape(t // SUBLANES, SUBLANES, RG_BLOCK)
        b = b.reshape(t // SUBLANES, SUBLANES, RG_BLOCK)
        for sft in (1, 2, 4):
            a_sh = pltpu.roll(a, sft, axis=1)
            b_sh = pltpu.roll(b, sft, axis=1)
            m = row >= sft
            b = jnp.where(m, a * b_sh + b, b)
            a = jnp.where(m, a * a_sh, a)
        a_s[:, c0:c1] = a.reshape(t, RG_BLOCK)
        b_s[:, c0:c1] = b.reshape(t, RG_BLOCK)

    def group(gi, hc):
        r0 = pl.multiple_of(gi * SUBLANES, SUBLANES)
        rows = b_s[pl.ds(r0, SUBLANES), :] + a_s[pl.ds(r0, SUBLANES), :] * hc
        b_s[pl.ds(r0, SUBLANES), :] = rows
        return rows[SUBLANES - 1:SUBLANES, :]

    h_s[...] = lax.fori_loop(0, t // SUBLANES, group, h_s[...])
    y_s[cur] = (b_s[...] * _gelu_tanh(gate_s[...].astype(F32))).astype(BF16)


def rglru_layer(x, g, w_in, conv_w, conv_b, wax, b_a, b_x, lam, w_out, *, seq, t):
    n, d = x.shape
    ns = seq // t
    total = n // t
    vec = pl.BlockSpec((1, d), lambda q: (0, 0))
    hbm = pl.BlockSpec(memory_space=pl.ANY)
    return pl.pallas_call(
        functools.partial(_rglru_layer_kernel, ns=ns),
        grid=(total + 1,),
        in_specs=[pl.BlockSpec((t, d), lambda q: (jnp.minimum(q + 1, total - 1), 0)),
                  pl.BlockSpec((t, d), lambda q: (jnp.maximum(q - 1, 0), 0)),
                  vec, hbm,
                  pl.BlockSpec((CONV_W, d), lambda q: (0, 0)),
                  vec,
                  pl.BlockSpec((RG_HEADS, RG_BLOCK, 2 * RG_BLOCK), lambda q: (0, 0, 0)),
                  vec, vec, vec, hbm],
        out_specs=pl.BlockSpec((t, d), lambda q: (jnp.maximum(q - 1, 0), 0)),
        out_shape=jax.ShapeDtypeStruct((n, d), F32),
        scratch_shapes=[pltpu.VMEM((d, 2 * d), BF16),
                        pltpu.VMEM((d, d), BF16),
                        pltpu.VMEM((2, t, 2 * d), BF16),
                        pltpu.VMEM((2, t, d), BF16),
                        pltpu.VMEM((t, d), BF16),
                        pltpu.VMEM((t, d), BF16),
                        pltpu.VMEM((t + SUBLANES, d), F32),
                        pltpu.VMEM((t, d), F32),
                        pltpu.VMEM((t, d), F32),
                        pltpu.VMEM((1, d), F32),
                        pltpu.SemaphoreType.DMA((2,))],
        compiler_params=_params("arbitrary"),
        name="rglru_layer",
    )(x, x, g, w_in, conv_w, conv_b, wax, b_a, b_x, lam, w_out)


def _swiglu_acc(h, wg_ref, wu_ref, wd_ref):
    g = _dot(h, wg_ref[...].astype(BF16))
    u = _dot(h, wu_ref[...].astype(BF16))
    a = (g * _sigmoid(g) * u).astype(BF16)
    return _dot(a, wd_ref[...].astype(BF16))


def _ffn_kernel(x_ref, g_ref, wg_ref, wu_ref, wd_ref, o_ref, hn_ref):
    @pl.when(pl.program_id(1) == 0)
    def _():
        x = x_ref[...]
        hn_ref[...] = _rms(x, g_ref[...]).astype(BF16)
        o_ref[...] = x

    o_ref[...] += _swiglu_acc(hn_ref[...], wg_ref, wu_ref, wd_ref)


def ffn_dense(x, g, wg, wu, wd, *, tm, tc):
    n, d = x.shape
    f = wg.shape[1]
    return pl.pallas_call(
        _ffn_kernel,
        grid=(n // tm, f // tc),
        in_specs=[pl.BlockSpec((tm, d), lambda i, j: (i, 0)),
                  pl.BlockSpec((1, d), lambda i, j: (0, 0)),
                  pl.BlockSpec((d, tc), lambda i, j: (0, j)),
                  pl.BlockSpec((d, tc), lambda i, j: (0, j)),
                  pl.BlockSpec((tc, d), lambda i, j: (j, 0))],
        out_specs=pl.BlockSpec((tm, d), lambda i, j: (i, 0)),
        out_shape=jax.ShapeDtypeStruct((n, d), F32),
        scratch_shapes=[pltpu.VMEM((tm, d), BF16)],
        compiler_params=_params("parallel", "arbitrary"),
        name="ffn_dense",
    )(x, g, wg, wu, wd)


MLA_SPLIT = 2


def _mla_proj_kernel(x_ref, pos_ref, g_ref, win_ref, qn_ref, wq_ref, kvn_ref, wkv_ref,
                     freq_ref, sign_ref, qnope_ref, qpe_ref, knope_ref, kpe_ref, v_ref):
    tm, d = x_ref.shape
    ts = tm // MLA_SPLIT
    subs = [slice(k * ts, (k + 1) * ts) for k in range(MLA_SPLIT)]

    def rope(xs, cos, sin):
        return xs * cos + pltpu.roll(xs, LANES // 2, axis=1) * sin

    us = [_dot(_rms(x_ref[rows, :], g_ref[...]).astype(BF16), win_ref[...]) for rows in subs]
    trig = []
    for rows in subs:
        ang = pos_ref[rows, :].astype(F32) * freq_ref[...]
        trig.append((jnp.cos(ang), jnp.sin(ang) * sign_ref[...]))
    qs = [_dot(_rms(u[:, :Q_LORA], qn_ref[...]).astype(BF16), wq_ref[...]) for u in us]
    kvs = [_dot(_rms(u[:, Q_LORA:Q_LORA + KV_LORA], kvn_ref[...]).astype(BF16), wkv_ref[...]) for u in us]
    for rows, u, (cos, sin), q, kv in zip(subs, us, trig, qs, kvs):
        kpe_ref[rows, :] = rope(u[:, Q_LORA + KV_LORA:], cos, sin).astype(BF16)
        qnope_ref[rows, :] = q[:, :d].astype(BF16)
        for h in range(MLA_HEADS):
            c0, c1 = h * LANES, (h + 1) * LANES
            qpe_ref[rows, c0:c1] = rope(q[:, d + c0:d + c1], cos, sin).astype(BF16)
        knope_ref[rows, :] = kv[:, :d].astype(BF16)
        v_ref[rows, :] = kv[:, d:].astype(BF16)


def mla_proj(x, pos, g, w_in, q_norm, w_q, kv_norm, w_kv, freq, sign, *, tm):
    n, d = x.shape
    row = lambda w: pl.BlockSpec((tm, w), lambda i: (i, 0))
    full = lambda a: pl.BlockSpec(a.shape, lambda i: (0, 0))
    big = jax.ShapeDtypeStruct((n, d), BF16)
    return pl.pallas_call(
        _mla_proj_kernel,
        grid=(n // tm,),
        in_specs=[row(d), row(1), full(g), full(w_in), full(q_norm), full(w_q), full(kv_norm),
                  full(w_kv), full(freq), full(sign)],
        out_specs=[row(d), row(d), row(d), row(LANES), row(d)],
        out_shape=[big, big, big, jax.ShapeDtypeStruct((n, LANES), BF16), big],
        compiler_params=_params("parallel"),
        name="mla_proj",
    )(x, pos, g, w_in, q_norm, w_q, kv_norm, w_kv, freq, sign)


def _attn_kernel(qn_ref, qp_ref, kn_ref, kp_ref, v_ref, o_ref, m_s, acc_s, bias_s):
    qi = pl.program_id(2)
    tq = qn_ref.shape[0]
    heads = qn_ref.shape[1] // LANES
    c = (QK_DIM ** -0.5) * math.log2(math.e)
    hs = lambda h: slice(h * LANES, (h + 1) * LANES)
    q = [jnp.concatenate([qn_ref[:, hs(h)], qp_ref[:, hs(h)]], axis=1) for h in range(heads)]
    ones = jnp.ones((tq, LANES), BF16)
    m_s[...] = jnp.full_like(m_s, -jnp.inf)
    acc_s[...] = jnp.zeros_like(acc_s)

    @pl.when((pl.program_id(0) == 0) & (pl.program_id(1) == 0) & (qi == 0))
    def _():
        q_idx = lax.broadcasted_iota(jnp.int32, (tq, tq), 0)
        k_idx = lax.broadcasted_iota(jnp.int32, (tq, tq), 1)
        bias_s[...] = jnp.where(k_idx <= q_idx, 0.0, -jnp.inf)

    def chunk(kj, masked):
        k0 = pl.multiple_of(kj * tq, tq)
        kp = kp_ref[pl.ds(k0, tq), :]
        for h in range(heads):
            k = jnp.concatenate([kn_ref[pl.ds(k0, tq), hs(h)], kp], axis=1)
            s = lax.dot_general(q[h], k, (((1,), (1,)), ((), ())), preferred_element_type=F32)
            if masked:
                s = s + bias_s[...]
            slabs = [s[:, hs(j)] for j in range(tq // LANES)]
            part = functools.reduce(jnp.maximum, slabs)
            m_prev = m_s[h]
            m_new = jnp.maximum(m_prev, jnp.max(part, axis=1, keepdims=True))
            p = jnp.concatenate([jnp.exp2((sl - m_new) * c) for sl in slabs], axis=1)
            alpha = jnp.exp2((m_prev - m_new) * c)
            vv = jnp.concatenate([v_ref[pl.ds(k0, tq), hs(h)], ones], axis=1)
            acc = acc_s[h]
            acc = jnp.concatenate([acc[:, :LANES] * alpha, acc[:, LANES:] * alpha], axis=1)
            acc_s[h] = acc + _dot(p.astype(BF16), vv)
            m_s[h] = m_new

    def body(kp2, carry):
        chunk(2 * kp2, False)
        chunk(2 * kp2 + 1, False)
        return carry

    lax.fori_loop(0, qi // 2, body, 0)

    @pl.when(qi % 2 == 1)
    def _():
        chunk(qi - 1, False)

    chunk(qi, True)
    for h in range(heads):
        acc = acc_s[h]
        o_ref[:, hs(h)] = (acc[:, :LANES] / acc[:, LANES:]).astype(o_ref.dtype)


def attention(q_nope, q_pe, k_nope, k_pe, v, *, batch, seq, tq, heads):
    n, d = q_nope.shape
    nq = seq // tq
    w = heads * LANES
    qspec = pl.BlockSpec((tq, w), lambda b, h, i: (b * nq + i, h))
    kspec = pl.BlockSpec((seq, w), lambda b, h, i: (b, h))
    return pl.pallas_call(
        _attn_kernel,
        grid=(batch, MLA_HEADS // heads, nq),
        in_specs=[qspec, qspec, kspec,
                  pl.BlockSpec((seq, LANES), lambda b, h, i: (b, 0)),
                  kspec],
        out_specs=qspec,
        out_shape=jax.ShapeDtypeStruct((n, d), BF16),
        scratch_shapes=[pltpu.VMEM((heads, tq, LANES), F32),
                        pltpu.VMEM((heads, tq, 2 * LANES), F32),
                        pltpu.VMEM((tq, tq), F32)],
        compiler_params=_params("arbitrary", "arbitrary", "arbitrary"),
        name="attention",
    )(q_nope, q_pe, k_nope, k_pe, v)


ROUTER_SPLIT = 4


def _out_router_kernel(o_ref, w_ref, r_ref, g_ref, wr_ref, x_ref, hn_ref, route_ref, cnt_ref, carry):
    i = pl.program_id(0)
    tm = o_ref.shape[0]
    ts = tm // ROUTER_SPLIT
    subs = [slice(k * ts, (k + 1) * ts) for k in range(ROUTER_SPLIT)]

    @pl.when(i == 0)
    def _():
        carry[...] = jnp.zeros_like(carry)

    xs = [r_ref[rows, :] + _dot(o_ref[rows, :], w_ref[...]) for rows in subs]
    hns = []
    for rows, x in zip(subs, xs):
        x_ref[rows, :] = x
        hn = _rms(x, g_ref[...])
        hn_ref[rows, :] = hn
        hns.append(hn)
    logit = []
    for hn in hns:
        hn_hi = hn.astype(BF16)
        hn_lo = (hn - hn_hi.astype(F32)).astype(BF16)
        parts = _dot(hn_hi, wr_ref[...]) + _dot(hn_lo, wr_ref[...])
        logit.append(parts[:, :LANES] + parts[:, LANES:])

    lane = lax.broadcasted_iota(jnp.int32, (ts, LANES), 1)
    lane_f = lane.astype(F32)
    rr = lax.broadcasted_iota(jnp.int32, (ts, ts), 0)
    cc = lax.broadcasted_iota(jnp.int32, (ts, ts), 1)
    tril = jnp.where(cc < rr, 1.0, 0.0).astype(BF16)
    neg = -jnp.inf
    total = carry[...]
    for rows, logits in zip(subs, logit):
        logits = jnp.where(lane < N_EXPERTS, logits, neg)
        m1 = jnp.max(logits, axis=1, keepdims=True)
        e1 = jnp.min(jnp.where(logits == m1, lane_f, float(LANES)), axis=1, keepdims=True)
        rest = jnp.where(lane_f == e1, neg, logits)
        m2 = jnp.max(rest, axis=1, keepdims=True)
        e2 = jnp.min(jnp.where(rest == m2, lane_f, float(LANES)), axis=1, keepdims=True)
        z = jnp.exp(m2 - m1)
        g1 = 1.0 / (1.0 + z)
        g2 = z / (1.0 + z)
        sel1 = lane_f == e1
        sel2 = lane_f == e2
        sel = jnp.where(sel1 | sel2, 1.0, 0.0)
        before = _dot(tril, sel.astype(BF16)) + total
        rank1 = jnp.sum(jnp.where(sel1, before, 0.0), axis=1, keepdims=True)
        rank2 = jnp.sum(jnp.where(sel2, before, 0.0), axis=1, keepdims=True)
        total = total + jnp.sum(sel, axis=0, keepdims=True)
        route = jnp.zeros((ts, LANES), F32)
        for col, val in enumerate((e1, e2, g1, g2, rank1, rank2)):
            route = jnp.where(lane == col, val, route)
        route_ref[rows, :] = route
    carry[...] = total
    cnt_ref[...] = jnp.broadcast_to(total, cnt_ref.shape)


def out_router(o, w_out, res, g, w_router, *, tm):
    n, d = res.shape
    row = lambda w: pl.BlockSpec((tm, w), lambda i: (i, 0))
    full = lambda a: pl.BlockSpec(a.shape, lambda i: (0, 0))
    return pl.pallas_call(
        _out_router_kernel,
        grid=(n // tm,),
        in_specs=[row(d), full(w_out), row(d), full(g), full(w_router)],
        out_specs=[row(d), row(d), row(LANES), pl.BlockSpec((SUBLANES, LANES), lambda i: (0, 0))],
        out_shape=[jax.ShapeDtypeStruct((n, d), F32), jax.ShapeDtypeStruct((n, d), F32),
                   jax.ShapeDtypeStruct((n, LANES), F32),
                   jax.ShapeDtypeStruct((SUBLANES, LANES), F32)],
        scratch_shapes=[pltpu.VMEM((1, LANES), F32)],
        compiler_params=_params("arbitrary"),
        name="out_router",
    )(o, w_out, res, g, w_router)


GATHER_SLOTS = 2
MOE_ROW_PARTS = 4


def _row_copy(src_hbm, row, dst, r, sem):
    return pltpu.make_async_copy(src_hbm.at[pl.ds(row, 1), :], dst.at[pl.ds(r, 1), :], sem)


def _moe_ffn_kernel(src_ref, be_ref, valid_ref, nu_ref, hn_hbm, wg_ref, wu_ref, wd_ref, o_ref, xg, xb, sems,
                    *, per_step):
    i = pl.program_id(0)
    j = pl.program_id(1)
    tm = o_ref.shape[0]
    rows = xg.shape[1]
    n_used = nu_ref[0]
    slot = i % GATHER_SLOTS

    def start_row(blk, sl, r):
        tok = src_ref[blk * tm + jnp.minimum(r, tm - 1)]
        _row_copy(hn_hbm, tok, xg.at[sl], r, sems.at[sl]).start()

    @pl.when((i == 0) & (j == 0))
    def _():
        def first(r, c):
            start_row(0, 0, r)
            return c
        lax.fori_loop(0, rows, first, 0, unroll=8)

    @pl.when((j == 0) & (i <= n_used))
    def _():
        pltpu.make_async_copy(hn_hbm.at[pl.ds(0, rows), :], xg.at[slot], sems.at[slot]).wait()

    @pl.when((j == 0) & (i < n_used))
    def _():
        xb[...] = xg[slot, :tm, :].astype(BF16)

    @pl.when(j == 0)
    def _():
        o_ref[...] = jnp.zeros_like(o_ref)

    def step(m):
        nxt = jnp.minimum(i + 1, n_used - 1)
        for u in range(per_step):
            start_row(nxt, (i + 1) % GATHER_SLOTS, j * per_step + u)
        o_ref[:m, :] += _swiglu_acc(xb[:m, :], wg_ref, wu_ref, wd_ref)

    quarter = tm // MOE_ROW_PARTS
    parts = (valid_ref[i] + quarter - 1) // quarter
    for k in range(1, MOE_ROW_PARTS + 1):
        @pl.when((i < n_used) & (parts == k))
        def _(k=k):
            step(k * quarter)


def moe_ffn(src_tok, block_e, block_valid, n_used, hn, wg, wu, wd, *, tm, tc):
    d = hn.shape[1]
    p = src_tok.shape[0]
    nj = wg.shape[2] // tc
    per_step = -(-tm // nj)
    while (per_step * nj) % SUBLANES:
        per_step += 1

    def blk(i, nu):
        return jnp.minimum(i, nu[0] - 1)

    def col(i, j, nu):
        return jnp.where(i < nu[0], j, nj - 1)

    return pl.pallas_call(
        functools.partial(_moe_ffn_kernel, per_step=per_step),
        grid_spec=pltpu.PrefetchScalarGridSpec(
            num_scalar_prefetch=4,
            grid=(p // tm, nj),
            in_specs=[pl.BlockSpec(memory_space=pl.ANY),
                      pl.BlockSpec((None, d, tc), lambda i, j, s, be, bv, nu: (be[blk(i, nu)], 0, col(i, j, nu))),
                      pl.BlockSpec((None, d, tc), lambda i, j, s, be, bv, nu: (be[blk(i, nu)], 0, col(i, j, nu))),
                      pl.BlockSpec((None, tc, d), lambda i, j, s, be, bv, nu: (be[blk(i, nu)], col(i, j, nu), 0))],
            out_specs=pl.BlockSpec((tm, d), lambda i, j, s, be, bv, nu: (i, 0)),
            scratch_shapes=[pltpu.VMEM((GATHER_SLOTS, per_step * nj, d), F32), pltpu.VMEM((tm, d), BF16),
                            pltpu.SemaphoreType.DMA((GATHER_SLOTS,))]),
        out_shape=jax.ShapeDtypeStruct((p, d), F32),
        compiler_params=_params("arbitrary", "arbitrary"),
        name="moe_ffn",
    )(src_tok, block_e, block_valid, n_used, hn, wg, wu, wd)


def _combine_kernel(dest_ref, x_ref, route_ref, g_ref, y_hbm, o_ref, ya, yb, sems):
    i = pl.program_id(0)
    tm = o_ref.shape[0]
    slot = i % 2

    def copies(blk, sl, r):
        t2 = 2 * (blk * tm + r)
        return (_row_copy(y_hbm, dest_ref[t2], ya.at[sl], r, sems.at[0, sl]),
                _row_copy(y_hbm, dest_ref[t2 + 1], yb.at[sl], r, sems.at[1, sl]))

    def issue(blk, sl):
        def body(r, c):
            ca, cb = copies(blk, sl, r)
            ca.start()
            cb.start()
            return c
        lax.fori_loop(0, tm, body, 0, unroll=8)

    @pl.when(i == 0)
    def _():
        issue(0, 0)

    @pl.when(i + 1 < pl.num_programs(0))
    def _():
        issue(i + 1, 1 - slot)

    pltpu.make_async_copy(y_hbm.at[pl.ds(0, tm), :], ya.at[slot], sems.at[0, slot]).wait()
    pltpu.make_async_copy(y_hbm.at[pl.ds(0, tm), :], yb.at[slot], sems.at[1, slot]).wait()
    route = route_ref[...]
    x = x_ref[...] + (ya[slot] * route[:, 2:3] + yb[slot] * route[:, 3:4])
    o_ref[...] = _rms(x, g_ref[...])


def combine(dest, x, route, g, y_buf, *, tm):
    n, d = x.shape
    return pl.pallas_call(
        _combine_kernel,
        grid_spec=pltpu.PrefetchScalarGridSpec(
            num_scalar_prefetch=1,
            grid=(n // tm,),
            in_specs=[pl.BlockSpec((tm, d), lambda i, dst: (i, 0)),
                      pl.BlockSpec((tm, LANES), lambda i, dst: (i, 0)),
                      pl.BlockSpec((1, d), lambda i, dst: (0, 0)),
                      pl.BlockSpec(memory_space=pl.ANY)],
            out_specs=pl.BlockSpec((tm, d), lambda i, dst: (i, 0)),
            scratch_shapes=[pltpu.VMEM((2, tm, d), F32), pltpu.VMEM((2, tm, d), F32),
                            pltpu.SemaphoreType.DMA((2, 2))]),
        out_shape=jax.ShapeDtypeStruct((n, d), F32),
        compiler_params=_params("arbitrary"),
        name="moe_combine",
    )(dest, x, route, g, y_buf)


def _spread_rope(w):
    half = QK_ROPE // 2
    z = jnp.zeros(w.shape[:-1] + (half,), w.dtype)
    return jnp.concatenate([w[..., :half], z, w[..., half:], z], axis=-1)


def _tile(n, want):
    t = min(n, want)
    assert n % t == 0, (n, t)
    return t


def kernel(x, positions, a_norm_mix, rg_w_in, rg_conv_w, rg_conv_b, rg_w_a, rg_b_a, rg_w_x, rg_b_x, rg_lambda, rg_w_out, a_norm_ffn, ff_w_gate, ff_w_up, ff_w_down, b_norm_mix, mla_w_in, mla_q_norm, mla_w_q_up, mla_kv_norm, mla_w_kv_up, mla_w_out, b_norm_ffn, moe_w_router, moe_w_gate, moe_w_up, moe_w_down, final_norm):
    batch, seq, d = x.shape
    n = batch * seq
    assert a_norm_mix.shape[0] == 1 and b_norm_mix.shape[0] == 1
    xf = x.reshape(n, d)
    vec = lambda a: a.reshape(1, -1).astype(F32)

    wax = jnp.concatenate([rg_w_a[0], rg_w_x[0]], axis=-1).astype(BF16)
    x1 = rglru_layer(xf, vec(a_norm_mix[0]), rg_w_in[0].astype(BF16), rg_conv_w[0], vec(rg_conv_b[0]), wax,
                     vec(rg_b_a[0]), vec(rg_b_x[0]), vec(rg_lambda[0]), rg_w_out[0].astype(BF16),
                     seq=seq, t=_tile(seq, 256))
    x2 = ffn_dense(x1, vec(a_norm_ffn[0]), ff_w_gate[0], ff_w_up[0], ff_w_down[0],
                   tm=_tile(n, FF_TM), tc=FF_TC)

    w_in = mla_w_in[0]
    w_in_p = jnp.concatenate([w_in[:, :Q_LORA + KV_LORA], _spread_rope(w_in[:, Q_LORA + KV_LORA:])],
                             axis=1).astype(BF16)
    wq = mla_w_q_up[0].reshape(Q_LORA, MLA_HEADS, QK_DIM)
    wq_p = jnp.concatenate([wq[..., :QK_NOPE].reshape(Q_LORA, -1),
                            _spread_rope(wq[..., QK_NOPE:]).reshape(Q_LORA, -1)], axis=1).astype(BF16)
    wkv = mla_w_kv_up[0].reshape(KV_LORA, MLA_HEADS, QK_NOPE + V_DIM)
    wkv_p = jnp.concatenate([wkv[..., :QK_NOPE].reshape(KV_LORA, -1),
                             wkv[..., QK_NOPE:].reshape(KV_LORA, -1)], axis=1).astype(BF16)
    inv_freq = 1.0 / (ROPE_THETA ** (jnp.arange(0, QK_ROPE, 2, dtype=F32) / QK_ROPE))
    freq = _spread_rope(jnp.concatenate([inv_freq, inv_freq])[None, :])
    ones = jnp.ones((1, QK_ROPE // 2), F32)
    sign = _spread_rope(jnp.concatenate([-ones, ones], axis=1))
    pos = positions.reshape(n, 1).astype(jnp.int32)
    q_nope, q_pe, k_nope, k_pe, v = mla_proj(
        x2, pos, vec(b_norm_mix[0]), w_in_p, vec(mla_q_norm[0]), wq_p, vec(mla_kv_norm[0]), wkv_p,
        freq, sign, tm=_tile(n, 512))
    o = attention(q_nope, q_pe, k_nope, k_pe, v, batch=batch, seq=seq, tq=_tile(seq, 512), heads=8)
    w_router = jnp.pad(moe_w_router[0].astype(F32), ((0, 0), (0, LANES - N_EXPERTS)))
    wr_hi = w_router.astype(BF16)
    w_router = jnp.concatenate([wr_hi, (w_router - wr_hi.astype(F32)).astype(BF16)], axis=1)
    x3, hn, route, cnt = out_router(o, mla_w_out[0].astype(BF16), x2, vec(b_norm_ffn[0]), w_router,
                                    tm=_tile(n, 512))

    tm = _tile(2 * n, FF_TM)
    counts = cnt[0, :N_EXPERTS].astype(jnp.int32)
    padded = ((counts + tm - 1) // tm) * tm
    pad_end = jnp.cumsum(padded)
    pad_start = pad_end - padded
    experts = route[:, 0:2].astype(jnp.int32)
    ranks = route[:, 4:6].astype(jnp.int32)
    dest = (pad_start[experts] + ranks).reshape(-1)
    p = 2 * n + N_EXPERTS * tm
    n_blocks = p // tm
    src_tok = jnp.zeros((p,), jnp.int32).at[dest].set(jnp.arange(2 * n, dtype=jnp.int32) // 2,
                                                      unique_indices=True, mode='promise_in_bounds')
    block_start = jnp.arange(n_blocks, dtype=jnp.int32) * tm
    block_e = jnp.minimum(jnp.sum(block_start[:, None] >= pad_end[None, :], axis=1),
                          N_EXPERTS - 1).astype(jnp.int32)
    block_valid = jnp.clip((pad_start + counts)[block_e] - block_start, 0, tm).astype(jnp.int32)
    n_used = (pad_end[-1:] // tm).astype(jnp.int32)

    y_buf = moe_ffn(src_tok, block_e, block_valid, n_used, hn, moe_w_gate[0], moe_w_up[0], moe_w_down[0],
                    tm=tm, tc=FF_TC)
    out = combine(dest, x3, route, vec(final_norm), y_buf, tm=_tile(n, 512))
    return out.reshape(batch, seq, d)
```

```python
import functools
import math

import jax
import jax.numpy as jnp
from jax import lax
from jax.experimental import pallas as pl
from jax.experimental.pallas import tpu as pltpu

EPS = 1e-6
RG_HEADS = 16
RG_BLOCK = 128
CONV_W = 4
RG_C = 8.0
MLA_HEADS = 16
Q_LORA = 512
KV_LORA = 512
QK_NOPE = 128
QK_ROPE = 64
V_DIM = 128
QK_DIM = QK_NOPE + QK_ROPE
ROPE_THETA = 10000.0
N_EXPERTS = 8
LANES = 128
SUBLANES = 8
VMEM_LIMIT_BYTES = 56 * 1024 * 1024
FF_TC = 256
FF_TM = 1024

BF16 = jnp.bfloat16
F32 = jnp.float32


def _params(*sem):
    return pltpu.CompilerParams(dimension_semantics=sem, vmem_limit_bytes=VMEM_LIMIT_BYTES)


def _rms(x, g):
    return x * lax.rsqrt(jnp.mean(x * x, axis=-1, keepdims=True) + EPS) * g


def _dot(a, b):
    return jnp.dot(a, b, preferred_element_type=F32)


def _sigmoid(x):
    return 1.0 / (1.0 + jnp.exp(-x))


def _sigmoid_tanh(x):
    return 0.5 * jnp.tanh(0.5 * x) + 0.5


def _gelu_tanh(x):
    c = math.sqrt(2.0 / math.pi)
    return 0.5 * x * (1.0 + jnp.tanh(c * (x + 0.044715 * (x * x * x))))


def _softplus(z):
    return jnp.maximum(z, 0.0) + jnp.log1p(jnp.exp(-jnp.abs(z)))


RG_PARTS = 4


def _rglru_layer_kernel(xn_ref, xp_ref, g_ref, win_hbm, cw_ref, cb_ref, wax_ref, ba_ref, bx_ref, lam_ref, wout_hbm,
                        o_ref, win_s, wout_s, u_s, y_s, gate_s, hn_s, xbuf, a_s, b_s, h_s, sem, *, ns):
    q = pl.program_id(0)
    s = q % ns
    t, d = xn_ref.shape
    cur = q % 2
    nxt = (q + 1) % 2

    def tdot(w_rows, acts):
        return lax.dot_general(w_rows, acts, (((1,), (1,)), ((), ())), preferred_element_type=F32)

    per = RG_HEADS // RG_PARTS
    pw = u_s.shape[2] // RG_PARTS
    ow = d // RG_PARTS

    def project(hn, slot, part):
        pc = slice(part * pw, (part + 1) * pw)
        u_s[slot, :, pc] = tdot(win_s[pc, :], hn).T.astype(BF16)

    @pl.when(q == 0)
    def _():
        cin = pltpu.make_async_copy(win_hbm, win_s, sem.at[0])
        cout = pltpu.make_async_copy(wout_hbm, wout_s, sem.at[1])
        cin.start()
        cout.start()
        cin.wait()
        cout.wait()
        hn0 = _rms(xp_ref[...], g_ref[...]).astype(BF16)
        for part in range(RG_PARTS):
            project(hn0, 0, part)
        y_s[...] = jnp.zeros_like(y_s)

    @pl.when(s == 0)
    def _():
        xbuf[0:SUBLANES, :] = jnp.zeros((SUBLANES, d), F32)
        h_s[...] = jnp.zeros_like(h_s)

    @pl.when(s > 0)
    def _():
        xbuf[0:SUBLANES, :] = xbuf[t:t + SUBLANES, :]

    xbuf[SUBLANES:t + SUBLANES, :] = u_s[cur, :, d:].astype(F32)
    gate_s[...] = u_s[cur, :, :d]
    hn_s[...] = _rms(xn_ref[...], g_ref[...]).astype(BF16)
    row = lax.broadcasted_iota(jnp.int32, (t // SUBLANES, SUBLANES, RG_BLOCK), 1)

    def conv_gates(h):
        c0, c1 = h * RG_BLOCK, (h + 1) * RG_BLOCK
        xc = cb_ref[:, c0:c1] + jnp.zeros((t, RG_BLOCK), F32)
        for k in range(CONV_W):
            sh = CONV_W - 1 - k
            xc = xc + xbuf[SUBLANES - sh:SUBLANES - sh + t, c0:c1] * cw_ref[k:k + 1, c0:c1]
        return xc, _dot(xc.astype(BF16), wax_ref[h])

    def recur(h, xc, gg):
        c0, c1 = h * RG_BLOCK, (h + 1) * RG_BLOCK
        r = _sigmoid_tanh(gg[:, :RG_BLOCK] + ba_ref[:, c0:c1])
        i = _sigmoid_tanh(gg[:, RG_BLOCK:] + bx_ref[:, c0:c1])
        log_a = -RG_C * r * _softplus(-lam_ref[:, c0:c1])
        a = jnp.exp(log_a)
        om = 1.0 - a * a
        b = jnp.where(om > 0.0, om * lax.rsqrt(om), 0.0) * i * xc
        a = a.reshape(t // SUBLANES, SUBLANES, RG_BLOCK)
        b = b.reshape(t // SUBLANES, SUBLANES, RG_BLOCK)
        for sft in (1, 2, 4):
            a_sh = pltpu.roll(a, sft, axis=1)
            b_sh = pltpu.roll(b, sft, axis=1)
            m = row >= sft
            b = jnp.where(m, a * b_sh + b, b)
            a = jnp.where(m, a * a_sh, a)
        a_s[:, c0:c1] = a.reshape(t, RG_BLOCK)
        b_s[:, c0:c1] = b.reshape(t, RG_BLOCK)

    for part in range(RG_PARTS):
        heads = range(part * per, (part + 1) * per)
        staged = [conv_gates(h) for h in heads]
        project(hn_s[...], nxt, part)
        oc = slice(part * ow, (part + 1) * ow)
        o_ref[:, oc] = xp_ref[:, oc] + tdot(wout_s[oc, :], y_s[nxt]).T
        for h, (xc, gg) in zip(heads, staged):
            recur(h, xc, gg)

    def group(gi, hc):
        r0 = pl.multiple_of(gi * SUBLANES, SUBLANES)
        rows = b_s[pl.ds(r0, SUBLANES), :] + a_s[pl.ds(r0, SUBLANES), :] * hc
        b_s[pl.ds(r0, SUBLANES), :] = rows
        return rows[SUBLANES - 1:SUBLANES, :]

    h_s[...] = lax.fori_loop(0, t // SUBLANES, group, h_s[...])
    y_s[cur] = (b_s[...] * _gelu_tanh(gate_s[...].astype(F32))).astype(BF16)


def rglru_layer(x, g, w_in, conv_w, conv_b, wax, b_a, b_x, lam, w_out, *, seq, t):
    n, d = x.shape
    ns = seq // t
    total = n // t
    vec = pl.BlockSpec((1, d), lambda q: (0, 0))
    hbm = pl.BlockSpec(memory_space=pl.ANY)
    return pl.pallas_call(
        functools.partial(_rglru_layer_kernel, ns=ns),
        grid=(total + 1,),
        in_specs=[pl.BlockSpec((t, d), lambda q: (jnp.minimum(q + 1, total - 1), 0)),
                  pl.BlockSpec((t, d), lambda q: (jnp.maximum(q - 1, 0), 0)),
                  vec, hbm,
                  pl.BlockSpec((CONV_W, d), lambda q: (0, 0)),
                  vec,
                  pl.BlockSpec((RG_HEADS, RG_BLOCK, 2 * RG_BLOCK), lambda q: (0, 0, 0)),
                  vec, vec, vec, hbm],
        out_specs=pl.BlockSpec((t, d), lambda q: (jnp.maximum(q - 1, 0), 0)),
        out_shape=jax.ShapeDtypeStruct((n, d), F32),
        scratch_shapes=[pltpu.VMEM((2 * d, d), BF16),
                        pltpu.VMEM((d, d), BF16),
                        pltpu.VMEM((2, t, 2 * d), BF16),
                        pltpu.VMEM((2, t, d), BF16),
                        pltpu.VMEM((t, d), BF16),
                        pltpu.VMEM((t, d), BF16),
                        pltpu.VMEM((t + SUBLANES, d), F32),
                        pltpu.VMEM((t, d), F32),
                        pltpu.VMEM((t, d), F32),
                        pltpu.VMEM((1, d), F32),
                        pltpu.SemaphoreType.DMA((2,))],
        compiler_params=_params("arbitrary"),
        name="rglru_layer",
    )(x, x, g, w_in, conv_w, conv_b, wax, b_a, b_x, lam, w_out)


def _swiglu_acc(h, wg_ref, wu_ref, wd_ref):
    g = _dot(h, wg_ref[...].astype(BF16))
    u = _dot(h, wu_ref[...].astype(BF16))
    a = (g * _sigmoid(g) * u).astype(BF16)
    return _dot(a, wd_ref[...].astype(BF16))


def _ffn_kernel(x_ref, g_ref, wg_ref, wu_ref, wd_ref, o_ref, hn_ref):
    @pl.when(pl.program_id(1) == 0)
    def _():
        x = x_ref[...]
        hn_ref[...] = _rms(x, g_ref[...]).astype(BF16)
        o_ref[...] = x

    o_ref[...] += _swiglu_acc(hn_ref[...], wg_ref, wu_ref, wd_ref)


def ffn_dense(x, g, wg, wu, wd, *, tm, tc):
    n, d = x.shape
    f = wg.shape[1]
    return pl.pallas_call(
        _ffn_kernel,
        grid=(n // tm, f // tc),
        in_specs=[pl.BlockSpec((tm, d), lambda i, j: (i, 0)),
                  pl.BlockSpec((1, d), lambda i, j: (0, 0)),
                  pl.BlockSpec((d, tc), lambda i, j: (0, j)),
                  pl.BlockSpec((d, tc), lambda i, j: (0, j)),
                  pl.BlockSpec((tc, d), lambda i, j: (j, 0))],
        out_specs=pl.BlockSpec((tm, d), lambda i, j: (i, 0)),
        out_shape=jax.ShapeDtypeStruct((n, d), F32),
        scratch_shapes=[pltpu.VMEM((tm, d), BF16)],
        compiler_params=_params("parallel", "arbitrary"),
        name="ffn_dense",
    )(x, g, wg, wu, wd)


MLA_SPLIT = 2


def _mla_proj_kernel(x_ref, pos_ref, g_ref, win_ref, qn_ref, wq_ref, kvn_ref, wkv_ref,
                     freq_ref, sign_ref, qnope_ref, qpe_ref, knope_ref, kpe_ref, v_ref):
    tm, d = x_ref.shape
    ts = tm // MLA_SPLIT
    subs = [slice(k * ts, (k + 1) * ts) for k in range(MLA_SPLIT)]

    def rope(xs, cos, sin):
        return xs * cos + pltpu.roll(xs, LANES // 2, axis=1) * sin

    us = [_dot(_rms(x_ref[rows, :], g_ref[...]).astype(BF16), win_ref[...]) for rows in subs]
    trig = []
    for rows in subs:
        ang = pos_ref[rows, :].astype(F32) * freq_ref[...]
        trig.append((jnp.cos(ang), jnp.sin(ang) * sign_ref[...]))
    qs = [_dot(_rms(u[:, :Q_LORA], qn_ref[...]).astype(BF16), wq_ref[...]) for u in us]
    kvs = [_dot(_rms(u[:, Q_LORA:Q_LORA + KV_LORA], kvn_ref[...]).astype(BF16), wkv_ref[...]) for u in us]
    for rows, u, (cos, sin), q, kv in zip(subs, us, trig, qs, kvs):
        kpe_ref[rows, :] = rope(u[:, Q_LORA + KV_LORA:], cos, sin).astype(BF16)
        qnope_ref[rows, :] = q[:, :d].astype(BF16)
        for h in range(MLA_HEADS):
            c0, c1 = h * LANES, (h + 1) * LANES
            qpe_ref[rows, c0:c1] = rope(q[:, d + c0:d + c1], cos, sin).astype(BF16)
        knope_ref[rows, :] = kv[:, :d].astype(BF16)
        v_ref[rows, :] = kv[:, d:].astype(BF16)


def mla_proj(x, pos, g, w_in, q_norm, w_q, kv_norm, w_kv, freq, sign, *, tm):
    n, d = x.shape
    row = lambda w: pl.BlockSpec((tm, w), lambda i: (i, 0))
    full = lambda a: pl.BlockSpec(a.shape, lambda i: (0, 0))
    big = jax.ShapeDtypeStruct((n, d), BF16)
    return pl.pallas_call(
        _mla_proj_kernel,
        grid=(n // tm,),
        in_specs=[row(d), row(1), full(g), full(w_in), full(q_norm), full(w_q), full(kv_norm),
                  full(w_kv), full(freq), full(sign)],
        out_specs=[row(d), row(d), row(d), row(LANES), row(d)],
        out_shape=[big, big, big, jax.ShapeDtypeStruct((n, LANES), BF16), big],
        compiler_params=_params("parallel"),
        name="mla_proj",
    )(x, pos, g, w_in, q_norm, w_q, kv_norm, w_kv, freq, sign)


def _attn_kernel(qn_ref, qp_ref, kn_ref, kp_ref, v_ref, o_ref, m_s, acc_s, bias_s):
    qi = pl.program_id(2)
    tq = qn_ref.shape[0]
    heads = qn_ref.shape[1] // LANES
    c = (QK_DIM ** -0.5) * math.log2(math.e)
    hs = lambda h: slice(h * LANES, (h + 1) * LANES)
    q = [jnp.concatenate([qn_ref[:, hs(h)], qp_ref[:, hs(h)]], axis=1) for h in range(heads)]
    ones = jnp.ones((tq, LANES), BF16)
    m_s[...] = jnp.full_like(m_s, -jnp.inf)
    acc_s[...] = jnp.zeros_like(acc_s)

    @pl.when((pl.program_id(0) == 0) & (pl.program_id(1) == 0) & (qi == 0))
    def _():
        q_idx = lax.broadcasted_iota(jnp.int32, (tq, tq), 0)
        k_idx = lax.broadcasted_iota(jnp.int32, (tq, tq), 1)
        bias_s[...] = jnp.where(k_idx <= q_idx, 0.0, -jnp.inf)

    def chunk(kj, masked):
        k0 = pl.multiple_of(kj * tq, tq)
        kp = kp_ref[pl.ds(k0, tq), :]
        for h in range(heads):
            k = jnp.concatenate([kn_ref[pl.ds(k0, tq), hs(h)], kp], axis=1)
            s = lax.dot_general(q[h], k, (((1,), (1,)), ((), ())), preferred_element_type=F32)
            if masked:
                s = s + bias_s[...]
            slabs = [s[:, hs(j)] for j in range(tq // LANES)]
            part = functools.reduce(jnp.maximum, slabs)
            m_prev = m_s[h]
            m_new = jnp.maximum(m_prev, jnp.max(part, axis=1, keepdims=True))
            p = jnp.concatenate([jnp.exp2((sl - m_new) * c) for sl in slabs], axis=1)
            alpha = jnp.exp2((m_prev - m_new) * c)
            vv = jnp.concatenate([v_ref[pl.ds(k0, tq), hs(h)], ones], axis=1)
            acc = acc_s[h]
            acc = jnp.concatenate([acc[:, :LANES] * alpha, acc[:, LANES:] * alpha], axis=1)
            acc_s[h] = acc + _dot(p.astype(BF16), vv)
            m_s[h] = m_new

    def body(kp2, carry):
        chunk(2 * kp2, False)
        chunk(2 * kp2 + 1, False)
        return carry

    lax.fori_loop(0, qi // 2, body, 0)

    @pl.when(qi % 2 == 1)
    def _():
        chunk(qi - 1, False)

    chunk(qi, True)
    for h in range(heads):
        acc = acc_s[h]
        o_ref[:, hs(h)] = (acc[:, :LANES] / acc[:, LANES:]).astype(o_ref.dtype)


def attention(q_nope, q_pe, k_nope, k_pe, v, *, batch, seq, tq, heads):
    n, d = q_nope.shape
    nq = seq // tq
    w = heads * LANES
    qspec = pl.BlockSpec((tq, w), lambda b, h, i: (b * nq + i, h))
    kspec = pl.BlockSpec((seq, w), lambda b, h, i: (b, h))
    return pl.pallas_call(
        _attn_kernel,
        grid=(batch, MLA_HEADS // heads, nq),
        in_specs=[qspec, qspec, kspec,
                  pl.BlockSpec((seq, LANES), lambda b, h, i: (b, 0)),
                  kspec],
        out_specs=qspec,
        out_shape=jax.ShapeDtypeStruct((n, d), BF16),
        scratch_shapes=[pltpu.VMEM((heads, tq, LANES), F32),
                        pltpu.VMEM((heads, tq, 2 * LANES), F32),
                        pltpu.VMEM((tq, tq), F32)],
        compiler_params=_params("arbitrary", "arbitrary", "arbitrary"),
        name="attention",
    )(q_nope, q_pe, k_nope, k_pe, v)


ROUTER_SPLIT = 4


def _out_router_kernel(o_ref, w_ref, r_ref, g_ref, wr_ref, x_ref, hn_ref, route_ref, cnt_ref, carry):
    i = pl.program_id(0)
    tm = o_ref.shape[0]
    ts = tm // ROUTER_SPLIT
    subs = [slice(k * ts, (k + 1) * ts) for k in range(ROUTER_SPLIT)]

    @pl.when(i == 0)
    def _():
        carry[...] = jnp.zeros_like(carry)

    xs = [r_ref[rows, :] + _dot(o_ref[rows, :], w_ref[...]) for rows in subs]
    hns = []
    for rows, x in zip(subs, xs):
        x_ref[rows, :] = x
        hn = _rms(x, g_ref[...])
        hn_ref[rows, :] = hn
        hns.append(hn)
    logit = []
    for hn in hns:
        hn_hi = hn.astype(BF16)
        hn_lo = (hn - hn_hi.astype(F32)).astype(BF16)
        parts = _dot(hn_hi, wr_ref[...]) + _dot(hn_lo, wr_ref[...])
        logit.append(parts[:, :LANES] + parts[:, LANES:])

    lane = lax.broadcasted_iota(jnp.int32, (ts, LANES), 1)
    lane_f = lane.astype(F32)
    rr = lax.broadcasted_iota(jnp.int32, (ts, ts), 0)
    cc = lax.broadcasted_iota(jnp.int32, (ts, ts), 1)
    tril = jnp.where(cc < rr, 1.0, 0.0).astype(BF16)
    neg = -jnp.inf
    total = carry[...]
    for rows, logits in zip(subs, logit):
        logits = jnp.where(lane < N_EXPERTS, logits, neg)
        m1 = jnp.max(logits, axis=1, keepdims=True)
        e1 = jnp.min(jnp.where(logits == m1, lane_f, float(LANES)), axis=1, keepdims=True)
        rest = jnp.where(lane_f == e1, neg, logits)
        m2 = jnp.max(rest, axis=1, keepdims=True)
        e2 = jnp.min(jnp.where(rest == m2, lane_f, float(LANES)), axis=1, keepdims=True)
        z = jnp.exp(m2 - m1)
        g1 = 1.0 / (1.0 + z)
        g2 = z / (1.0 + z)
        sel1 = lane_f == e1
        sel2 = lane_f == e2
        sel = jnp.where(sel1 | sel2, 1.0, 0.0)
        before = _dot(tril, sel.astype(BF16)) + total
        rank1 = jnp.sum(jnp.where(sel1, before, 0.0), axis=1, keepdims=True)
        rank2 = jnp.sum(jnp.where(sel2, before, 0.0), axis=1, keepdims=True)
        total = total + jnp.sum(sel, axis=0, keepdims=True)
        route = jnp.zeros((ts, LANES), F32)
        for col, val in enumerate((e1, e2, g1, g2, rank1, rank2)):
            route = jnp.where(lane == col, val, route)
        route_ref[rows, :] = route
    carry[...] = total
    cnt_ref[...] = jnp.broadcast_to(total, cnt_ref.shape)


def out_router(o, w_out, res, g, w_router, *, tm):
    n, d = res.shape
    row = lambda w: pl.BlockSpec((tm, w), lambda i: (i, 0))
    full = lambda a: pl.BlockSpec(a.shape, lambda i: (0, 0))
    return pl.pallas_call(
        _out_router_kernel,
        grid=(n // tm,),
        in_specs=[row(d), full(w_out), row(d), full(g), full(w_router)],
        out_specs=[row(d), row(d), row(LANES), pl.BlockSpec((SUBLANES, LANES), lambda i: (0, 0))],
        out_shape=[jax.ShapeDtypeStruct((n, d), F32), jax.ShapeDtypeStruct((n, d), F32),
                   jax.ShapeDtypeStruct((n, LANES), F32),
                   jax.ShapeDtypeStruct((SUBLANES, LANES), F32)],
        scratch_shapes=[pltpu.VMEM((1, LANES), F32)],
        compiler_params=_params("arbitrary"),
        name="out_router",
    )(o, w_out, res, g, w_router)


GATHER_SLOTS = 2
MOE_ROW_PARTS = 4


def _row_copy(src_hbm, row, dst, r, sem):
    return pltpu.make_async_copy(src_hbm.at[pl.ds(row, 1), :], dst.at[pl.ds(r, 1), :], sem)


def _moe_ffn_kernel(src_ref, be_ref, valid_ref, nu_ref, hn_hbm, wg_ref, wu_ref, wd_ref, o_ref, xg, xb, sems,
                    *, per_step):
    i = pl.program_id(0)
    j = pl.program_id(1)
    tm = o_ref.shape[0]
    rows = xg.shape[1]
    n_used = nu_ref[0]
    slot = i % GATHER_SLOTS

    def start_row(blk, sl, r):
        tok = src_ref[blk * tm + jnp.minimum(r, tm - 1)]
        _row_copy(hn_hbm, tok, xg.at[sl], r, sems.at[sl]).start()

    @pl.when((i == 0) & (j == 0))
    def _():
        def first(r, c):
            start_row(0, 0, r)
            return c
        lax.fori_loop(0, rows, first, 0, unroll=8)

    @pl.when((j == 0) & (i <= n_used))
    def _():
        pltpu.make_async_copy(hn_hbm.at[pl.ds(0, rows), :], xg.at[slot], sems.at[slot]).wait()

    @pl.when((j == 0) & (i < n_used))
    def _():
        xb[...] = xg[slot, :tm, :].astype(BF16)

    @pl.when(j == 0)
    def _():
        o_ref[...] = jnp.zeros_like(o_ref)

    def step(m):
        nxt = jnp.minimum(i + 1, n_used - 1)
        for u in range(per_step):
            start_row(nxt, (i + 1) % GATHER_SLOTS, j * per_step + u)
        o_ref[:m, :] += _swiglu_acc(xb[:m, :], wg_ref, wu_ref, wd_ref)

    quarter = tm // MOE_ROW_PARTS
    parts = (valid_ref[i] + quarter - 1) // quarter
    for k in range(1, MOE_ROW_PARTS + 1):
        @pl.when((i < n_used) & (parts == k))
        def _(k=k):
            step(k * quarter)


def moe_ffn(src_tok, block_e, block_valid, n_used, hn, wg, wu, wd, *, tm, tc):
    d = hn.shape[1]
    p = src_tok.shape[0]
    nj = wg.shape[2] // tc
    per_step = -(-tm // nj)
    while (per_step * nj) % SUBLANES:
        per_step += 1

    def blk(i, nu):
        return jnp.minimum(i, nu[0] - 1)

    def col(i, j, nu):
        return jnp.where(i < nu[0], j, nj - 1)

    return pl.pallas_call(
        functools.partial(_moe_ffn_kernel, per_step=per_step),
        grid_spec=pltpu.PrefetchScalarGridSpec(
            num_scalar_prefetch=4,
            grid=(p // tm, nj),
            in_specs=[pl.BlockSpec(memory_space=pl.ANY),
                      pl.BlockSpec((None, d, tc), lambda i, j, s, be, bv, nu: (be[blk(i, nu)], 0, col(i, j, nu))),
                      pl.BlockSpec((None, d, tc), lambda i, j, s, be, bv, nu: (be[blk(i, nu)], 0, col(i, j, nu))),
                      pl.BlockSpec((None, tc, d), lambda i, j, s, be, bv, nu: (be[blk(i, nu)], col(i, j, nu), 0))],
            out_specs=pl.BlockSpec((tm, d), lambda i, j, s, be, bv, nu: (i, 0)),
            scratch_shapes=[pltpu.VMEM((GATHER_SLOTS, per_step * nj, d), F32), pltpu.VMEM((tm, d), BF16),
                            pltpu.SemaphoreType.DMA((GATHER_SLOTS,))]),
        out_shape=jax.ShapeDtypeStruct((p, d), F32),
        compiler_params=_params("arbitrary", "arbitrary"),
        name="moe_ffn",
    )(src_tok, block_e, block_valid, n_used, hn, wg, wu, wd)


def _combine_kernel(dest_ref, x_ref, route_ref, g_ref, y_hbm, o_ref, ya, yb, sems):
    i = pl.program_id(0)
    tm = o_ref.shape[0]
    slot = i % 2

    def copies(blk, sl, r):
        t2 = 2 * (blk * tm + r)
        return (_row_copy(y_hbm, dest_ref[t2], ya.at[sl], r, sems.at[0, sl]),
                _row_copy(y_hbm, dest_ref[t2 + 1], yb.at[sl], r, sems.at[1, sl]))

    def issue(blk, sl):
        def body(r, c):
            ca, cb = copies(blk, sl, r)
            ca.start()
            cb.start()
            return c
        lax.fori_loop(0, tm, body, 0, unroll=8)

    @pl.when(i == 0)
    def _():
        issue(0, 0)

    @pl.when(i + 1 < pl.num_programs(0))
    def _():
        issue(i + 1, 1 - slot)

    pltpu.make_async_copy(y_hbm.at[pl.ds(0, tm), :], ya.at[slot], sems.at[0, slot]).wait()
    pltpu.make_async_copy(y_hbm.at[pl.ds(0, tm), :], yb.at[slot], sems.at[1, slot]).wait()
    route = route_ref[...]
    x = x_ref[...] + (ya[slot] * route[:, 2:3] + yb[slot] * route[:, 3:4])
    o_ref[...] = _rms(x, g_ref[...])


def combine(dest, x, route, g, y_buf, *, tm):
    n, d = x.shape
    return pl.pallas_call(
        _combine_kernel,
        grid_spec=pltpu.PrefetchScalarGridSpec(
            num_scalar_prefetch=1,
            grid=(n // tm,),
            in_specs=[pl.BlockSpec((tm, d), lambda i, dst: (i, 0)),
                      pl.BlockSpec((tm, LANES), lambda i, dst: (i, 0)),
                      pl.BlockSpec((1, d), lambda i, dst: (0, 0)),
                      pl.BlockSpec(memory_space=pl.ANY)],
            out_specs=pl.BlockSpec((tm, d), lambda i, dst: (i, 0)),
            scratch_shapes=[pltpu.VMEM((2, tm, d), F32), pltpu.VMEM((2, tm, d), F32),
                            pltpu.SemaphoreType.DMA((2, 2))]),
        out_shape=jax.ShapeDtypeStruct((n, d), F32),
        compiler_params=_params("arbitrary"),
        name="moe_combine",
    )(dest, x, route, g, y_buf)


def _spread_rope(w):
    half = QK_ROPE // 2
    z = jnp.zeros(w.shape[:-1] + (half,), w.dtype)
    return jnp.concatenate([w[..., :half], z, w[..., half:], z], axis=-1)


def _tile(n, want):
    t = min(n, want)
    assert n % t == 0, (n, t)
    return t


def kernel(x, positions, a_norm_mix, rg_w_in, rg_conv_w, rg_conv_b, rg_w_a, rg_b_a, rg_w_x, rg_b_x, rg_lambda, rg_w_out, a_norm_ffn, ff_w_gate, ff_w_up, ff_w_down, b_norm_mix, mla_w_in, mla_q_norm, mla_w_q_up, mla_kv_norm, mla_w_kv_up, mla_w_out, b_norm_ffn, moe_w_router, moe_w_gate, moe_w_up, moe_w_down, final_norm):
    batch, seq, d = x.shape
    n = batch * seq
    assert a_norm_mix.shape[0] == 1 and b_norm_mix.shape[0] == 1
    xf = x.reshape(n, d)
    vec = lambda a: a.reshape(1, -1).astype(F32)

    wax = jnp.concatenate([rg_w_a[0], rg_w_x[0]], axis=-1).astype(BF16)
    x1 = rglru_layer(xf, vec(a_norm_mix[0]), rg_w_in[0].T.astype(BF16), rg_conv_w[0], vec(rg_conv_b[0]), wax,
                     vec(rg_b_a[0]), vec(rg_b_x[0]), vec(rg_lambda[0]), rg_w_out[0].T.astype(BF16),
                     seq=seq, t=_tile(seq, 256))
    x2 = ffn_dense(x1, vec(a_norm_ffn[0]), ff_w_gate[0], ff_w_up[0], ff_w_down[0],
                   tm=_tile(n, FF_TM), tc=FF_TC)

    w_in = mla_w_in[0]
    w_in_p = jnp.concatenate([w_in[:, :Q_LORA + KV_LORA], _spread_rope(w_in[:, Q_LORA + KV_LORA:])],
                             axis=1).astype(BF16)
    wq = mla_w_q_up[0].reshape(Q_LORA, MLA_HEADS, QK_DIM)
    wq_p = jnp.concatenate([wq[..., :QK_NOPE].reshape(Q_LORA, -1),
                            _spread_rope(wq[..., QK_NOPE:]).reshape(Q_LORA, -1)], axis=1).astype(BF16)
    wkv = mla_w_kv_up[0].reshape(KV_LORA, MLA_HEADS, QK_NOPE + V_DIM)
    wkv_p = jnp.concatenate([wkv[..., :QK_NOPE].reshape(KV_LORA, -1),
                             wkv[..., QK_NOPE:].reshape(KV_LORA, -1)], axis=1).astype(BF16)
    inv_freq = 1.0 / (ROPE_THETA ** (jnp.arange(0, QK_ROPE, 2, dtype=F32) / QK_ROPE))
    freq = _spread_rope(jnp.concatenate([inv_freq, inv_freq])[None, :])
    ones = jnp.ones((1, QK_ROPE // 2), F32)
    sign = _spread_rope(jnp.concatenate([-ones, ones], axis=1))
    pos = positions.reshape(n, 1).astype(jnp.int32)
    q_nope, q_pe, k_nope, k_pe, v = mla_proj(
        x2, pos, vec(b_norm_mix[0]), w_in_p, vec(mla_q_norm[0]), wq_p, vec(mla_kv_norm[0]), wkv_p,
        freq, sign, tm=_tile(n, 512))
    o = attention(q_nope, q_pe, k_nope, k_pe, v, batch=batch, seq=seq, tq=_tile(seq, 512), heads=8)
    w_router = jnp.pad(moe_w_router[0].astype(F32), ((0, 0), (0, LANES - N_EXPERTS)))
    wr_hi = w_router.astype(BF16)
    w_router = jnp.concatenate([wr_hi, (w_router - wr_hi.astype(F32)).astype(BF16)], axis=1)
    x3, hn, route, cnt = out_router(o, mla_w_out[0].astype(BF16), x2, vec(b_norm_ffn[0]), w_router,
                                    tm=_tile(n, 512))

    tm = _tile(2 * n, FF_TM)
    counts = cnt[0, :N_EXPERTS].astype(jnp.int32)
    padded = ((counts + tm - 1) // tm) * tm
    pad_end = jnp.cumsum(padded)
    pad_start = pad_end - padded
    experts = route[:, 0:2].astype(jnp.int32)
    ranks = route[:, 4:6].astype(jnp.int32)
    dest = (pad_start[experts] + ranks).reshape(-1)
    p = 2 * n + N_EXPERTS * tm
    n_blocks = p // tm
    src_tok = jnp.zeros((p,), jnp.int32).at[dest].set(jnp.arange(2 * n, dtype=jnp.int32) // 2,
                                                      unique_indices=True, mode='promise_in_bounds')
    block_start = jnp.arange(n_blocks, dtype=jnp.int32) * tm
    block_e = jnp.minimum(jnp.sum(block_start[:, None] >= pad_end[None, :], axis=1),
                          N_EXPERTS - 1).astype(jnp.int32)
    block_valid = jnp.clip((pad_start + counts)[block_e] - block_start, 0, tm).astype(jnp.int32)
    n_used = (pad_end[-1:] // tm).astype(jnp.int32)

    y_buf = moe_ffn(src_tok, block_e, block_valid, n_used, hn, moe_w_gate[0], moe_w_up[0], moe_w_down[0],
                    tm=tm, tc=FF_TC)
    out = combine(dest, x3, route, vec(final_norm), y_buf, tm=_tile(n, 256))
    return out.reshape(batch, seq, d)
```

```python
import functools
import math

import jax
import jax.numpy as jnp
from jax import lax
from jax.experimental import pallas as pl
from jax.experimental.pallas import tpu as pltpu

EPS = 1e-6
RG_HEADS = 16
RG_BLOCK = 128
CONV_W = 4
RG_C = 8.0
MLA_HEADS = 16
Q_LORA = 512
KV_LORA = 512
QK_NOPE = 128
QK_ROPE = 64
V_DIM = 128
QK_DIM = QK_NOPE + QK_ROPE
ROPE_THETA = 10000.0
N_EXPERTS = 8
LANES = 128
SUBLANES = 8
VMEM_LIMIT_BYTES = 56 * 1024 * 1024
FF_TC = 256
FF_TM = 1024

BF16 = jnp.bfloat16
F32 = jnp.float32


def _params(*sem):
    return pltpu.CompilerParams(dimension_semantics=sem, vmem_limit_bytes=VMEM_LIMIT_BYTES)


def _rms(x, g):
    return x * lax.rsqrt(jnp.mean(x * x, axis=-1, keepdims=True) + EPS) * g


def _dot(a, b):
    return jnp.dot(a, b, preferred_element_type=F32)


def _sigmoid(x):
    return 1.0 / (1.0 + jnp.exp(-x))


def _sigmoid_tanh(x):
    return 0.5 * jnp.tanh(0.5 * x) + 0.5


def _gelu_tanh(x):
    c = math.sqrt(2.0 / math.pi)
    return 0.5 * x * (1.0 + jnp.tanh(c * (x + 0.044715 * (x * x * x))))


def _softplus(z):
    return jnp.maximum(z, 0.0) + jnp.log1p(jnp.exp(-jnp.abs(z)))


RG_PARTS = 8


def _rglru_layer_kernel(xn_ref, xp_ref, g_ref, win_hbm, cw_ref, cb_ref, wax_ref, ba_ref, bx_ref, lam_ref, wout_hbm,
                        o_ref, win_s, wout_s, u_s, y_s, gate_s, hn_s, xbuf, a_s, b_s, h_s, sem, *, ns):
    q = pl.program_id(0)
    s = q % ns
    t, d = xn_ref.shape
    cur = q % 2
    nxt = (q + 1) % 2

    def tdot(w_rows, acts):
        return lax.dot_general(w_rows, acts, (((1,), (1,)), ((), ())), preferred_element_type=F32)

    per = RG_HEADS // RG_PARTS
    pw = u_s.shape[2] // RG_PARTS
    ow = d // RG_PARTS

    def project(hn, slot, part):
        pc = slice(part * pw, (part + 1) * pw)
        u_s[slot, :, pc] = tdot(win_s[pc, :], hn).T.astype(BF16)

    @pl.when(q == 0)
    def _():
        cin = pltpu.make_async_copy(win_hbm, win_s, sem.at[0])
        cout = pltpu.make_async_copy(wout_hbm, wout_s, sem.at[1])
        cin.start()
        cout.start()
        cin.wait()
        cout.wait()
        hn0 = _rms(xp_ref[...], g_ref[...]).astype(BF16)
        for part in range(RG_PARTS):
            project(hn0, 0, part)
        y_s[...] = jnp.zeros_like(y_s)

    @pl.when(s == 0)
    def _():
        xbuf[0:SUBLANES, :] = jnp.zeros((SUBLANES, d), F32)
        h_s[...] = jnp.zeros_like(h_s)

    @pl.when(s > 0)
    def _():
        xbuf[0:SUBLANES, :] = xbuf[t:t + SUBLANES, :]

    xbuf[SUBLANES:t + SUBLANES, :] = u_s[cur, :, d:].astype(F32)
    gate_s[...] = u_s[cur, :, :d]
    hn_s[...] = _rms(xn_ref[...], g_ref[...]).astype(BF16)
    row = lax.broadcasted_iota(jnp.int32, (t // SUBLANES, SUBLANES, RG_BLOCK), 1)

    def conv_gates(h):
        c0, c1 = h * RG_BLOCK, (h + 1) * RG_BLOCK
        xc = cb_ref[:, c0:c1] + jnp.zeros((t, RG_BLOCK), F32)
        for k in range(CONV_W):
            sh = CONV_W - 1 - k
            xc = xc + xbuf[SUBLANES - sh:SUBLANES - sh + t, c0:c1] * cw_ref[k:k + 1, c0:c1]
        return xc, _dot(xc.astype(BF16), wax_ref[h])

    def recur(h, xc, gg):
        c0, c1 = h * RG_BLOCK, (h + 1) * RG_BLOCK
        r = _sigmoid_tanh(gg[:, :RG_BLOCK] + ba_ref[:, c0:c1])
        i = _sigmoid_tanh(gg[:, RG_BLOCK:] + bx_ref[:, c0:c1])
        log_a = -RG_C * r * _softplus(-lam_ref[:, c0:c1])
        a = jnp.exp(log_a)
        om = 1.0 - a * a
        b = jnp.where(om > 0.0, om * lax.rsqrt(om), 0.0) * i * xc
        a = a.reshape(t // SUBLANES, SUBLANES, RG_BLOCK)
        b = b.reshape(t // SUBLANES, SUBLANES, RG_BLOCK)
        for sft in (1, 2, 4):
            a_sh = pltpu.roll(a, sft, axis=1)
            b_sh = pltpu.roll(b, sft, axis=1)
            m = row >= sft
            b = jnp.where(m, a * b_sh + b, b)
            a = jnp.where(m, a * a_sh, a)
        a_s[:, c0:c1] = a.reshape(t, RG_BLOCK)
        b_s[:, c0:c1] = b.reshape(t, RG_BLOCK)

    for part in range(RG_PARTS):
        heads = range(part * per, (part + 1) * per)
        staged = [conv_gates(h) for h in heads]
        project(hn_s[...], nxt, part)
        oc = slice(part * ow, (part + 1) * ow)
        o_ref[:, oc] = xp_ref[:, oc] + tdot(wout_s[oc, :], y_s[nxt]).T
        for h, (xc, gg) in zip(heads, staged):
            recur(h, xc, gg)

    def group(gi, hc):
        r0 = pl.multiple_of(gi * SUBLANES, SUBLANES)
        rows = b_s[pl.ds(r0, SUBLANES), :] + a_s[pl.ds(r0, SUBLANES), :] * hc
        b_s[pl.ds(r0, SUBLANES), :] = rows
        return rows[SUBLANES - 1:SUBLANES, :]

    h_s[...] = lax.fori_loop(0, t // SUBLANES, group, h_s[...])
    y_s[cur] = (b_s[...] * _gelu_tanh(gate_s[...].astype(F32))).astype(BF16)


def rglru_layer(x, g, w_in, conv_w, conv_b, wax, b_a, b_x, lam, w_out, *, seq, t):
    n, d = x.shape
    ns = seq // t
    total = n // t
    vec = pl.BlockSpec((1, d), lambda q: (0, 0))
    hbm = pl.BlockSpec(memory_space=pl.ANY)
    return pl.pallas_call(
        functools.partial(_rglru_layer_kernel, ns=ns),
        grid=(total + 1,),
        in_specs=[pl.BlockSpec((t, d), lambda q: (jnp.minimum(q + 1, total - 1), 0)),
                  pl.BlockSpec((t, d), lambda q: (jnp.maximum(q - 1, 0), 0)),
                  vec, hbm,
                  pl.BlockSpec((CONV_W, d), lambda q: (0, 0)),
                  vec,
                  pl.BlockSpec((RG_HEADS, RG_BLOCK, 2 * RG_BLOCK), lambda q: (0, 0, 0)),
                  vec, vec, vec, hbm],
        out_specs=pl.BlockSpec((t, d), lambda q: (jnp.maximum(q - 1, 0), 0)),
        out_shape=jax.ShapeDtypeStruct((n, d), F32),
        scratch_shapes=[pltpu.VMEM((2 * d, d), BF16),
                        pltpu.VMEM((d, d), BF16),
                        pltpu.VMEM((2, t, 2 * d), BF16),
                        pltpu.VMEM((2, t, d), BF16),
                        pltpu.VMEM((t, d), BF16),
                        pltpu.VMEM((t, d), BF16),
                        pltpu.VMEM((t + SUBLANES, d), F32),
                        pltpu.VMEM((t, d), F32),
                        pltpu.VMEM((t, d), F32),
                        pltpu.VMEM((1, d), F32),
                        pltpu.SemaphoreType.DMA((2,))],
        compiler_params=_params("arbitrary"),
        name="rglru_layer",
    )(x, x, g, w_in, conv_w, conv_b, wax, b_a, b_x, lam, w_out)


def _swiglu_acc(h, wg_ref, wu_ref, wd_ref):
    g = _dot(h, wg_ref[...].astype(BF16))
    u = _dot(h, wu_ref[...].astype(BF16))
    a = (g * _sigmoid(g) * u).astype(BF16)
    return _dot(a, wd_ref[...].astype(BF16))


def _ffn_kernel(x_ref, g_ref, wg_ref, wu_ref, wd_ref, o_ref, hn_ref):
    @pl.when(pl.program_id(1) == 0)
    def _():
        x = x_ref[...]
        hn_ref[...] = _rms(x, g_ref[...]).astype(BF16)
        o_ref[...] = x

    o_ref[...] += _swiglu_acc(hn_ref[...], wg_ref, wu_ref, wd_ref)


def ffn_dense(x, g, wg, wu, wd, *, tm, tc):
    n, d = x.shape
    f = wg.shape[1]
    return pl.pallas_call(
        _ffn_kernel,
        grid=(n // tm, f // tc),
        in_specs=[pl.BlockSpec((tm, d), lambda i, j: (i, 0)),
                  pl.BlockSpec((1, d), lambda i, j: (0, 0)),
                  pl.BlockSpec((d, tc), lambda i, j: (0, j)),
                  pl.BlockSpec((d, tc), lambda i, j: (0, j)),
                  pl.BlockSpec((tc, d), lambda i, j: (j, 0))],
        out_specs=pl.BlockSpec((tm, d), lambda i, j: (i, 0)),
        out_shape=jax.ShapeDtypeStruct((n, d), F32),
        scratch_shapes=[pltpu.VMEM((tm, d), BF16)],
        compiler_params=_params("parallel", "arbitrary"),
        name="ffn_dense",
    )(x, g, wg, wu, wd)


MLA_SPLIT = 2


def _mla_proj_kernel(x_ref, pos_ref, g_ref, win_ref, qn_ref, wq_ref, kvn_ref, wkv_ref,
                     freq_ref, sign_ref, qnope_ref, qpe_ref, knope_ref, kpe_ref, v_ref):
    tm, d = x_ref.shape
    ts = tm // MLA_SPLIT
    subs = [slice(k * ts, (k + 1) * ts) for k in range(MLA_SPLIT)]

    def rope(xs, cos, sin):
        return xs * cos + pltpu.roll(xs, LANES // 2, axis=1) * sin

    us = [_dot(_rms(x_ref[rows, :], g_ref[...]).astype(BF16), win_ref[...]) for rows in subs]
    trig = []
    for rows in subs:
        ang = pos_ref[rows, :].astype(F32) * freq_ref[...]
        trig.append((jnp.cos(ang), jnp.sin(ang) * sign_ref[...]))
    qs = [_dot(_rms(u[:, :Q_LORA], qn_ref[...]).astype(BF16), wq_ref[...]) for u in us]
    kvs = [_dot(_rms(u[:, Q_LORA:Q_LORA + KV_LORA], kvn_ref[...]).astype(BF16), wkv_ref[...]) for u in us]
    for rows, u, (cos, sin), q, kv in zip(subs, us, trig, qs, kvs):
        kpe_ref[rows, :] = rope(u[:, Q_LORA + KV_LORA:], cos, sin).astype(BF16)
        qnope_ref[rows, :] = q[:, :d].astype(BF16)
        for h in range(MLA_HEADS):
            c0, c1 = h * LANES, (h + 1) * LANES
            qpe_ref[rows, c0:c1] = rope(q[:, d + c0:d + c1], cos, sin).astype(BF16)
        knope_ref[rows, :] = kv[:, :d].astype(BF16)
        v_ref[rows, :] = kv[:, d:].astype(BF16)


def mla_proj(x, pos, g, w_in, q_norm, w_q, kv_norm, w_kv, freq, sign, *, tm):
    n, d = x.shape
    row = lambda w: pl.BlockSpec((tm, w), lambda i: (i, 0))
    full = lambda a: pl.BlockSpec(a.shape, lambda i: (0, 0))
    big = jax.ShapeDtypeStruct((n, d), BF16)
    return pl.pallas_call(
        _mla_proj_kernel,
        grid=(n // tm,),
        in_specs=[row(d), row(1), full(g), full(w_in), full(q_norm), full(w_q), full(kv_norm),
                  full(w_kv), full(freq), full(sign)],
        out_specs=[row(d), row(d), row(d), row(LANES), row(d)],
        out_shape=[big, big, big, jax.ShapeDtypeStruct((n, LANES), BF16), big],
        compiler_params=_params("parallel"),
        name="mla_proj",
    )(x, pos, g, w_in, q_norm, w_q, kv_norm, w_kv, freq, sign)


def _attn_kernel(qn_ref, qp_ref, kn_ref, kp_ref, v_ref, o_ref, m_s, acc_s, bias_s):
    qi = pl.program_id(2)
    tq = qn_ref.shape[0]
    heads = qn_ref.shape[1] // LANES
    c = (QK_DIM ** -0.5) * math.log2(math.e)
    hs = lambda h: slice(h * LANES, (h + 1) * LANES)
    q = [jnp.concatenate([qn_ref[:, hs(h)], qp_ref[:, hs(h)]], axis=1) for h in range(heads)]
    ones = jnp.ones((tq, LANES), BF16)
    m_s[...] = jnp.full_like(m_s, -jnp.inf)
    acc_s[...] = jnp.zeros_like(acc_s)

    @pl.when((pl.program_id(0) == 0) & (pl.program_id(1) == 0) & (qi == 0))
    def _():
        q_idx = lax.broadcasted_iota(jnp.int32, (tq, tq), 0)
        k_idx = lax.broadcasted_iota(jnp.int32, (tq, tq), 1)
        bias_s[...] = jnp.where(k_idx <= q_idx, 0.0, -jnp.inf)

    def chunk(kj, masked):
        k0 = pl.multiple_of(kj * tq, tq)
        kp = kp_ref[pl.ds(k0, tq), :]
        for h in range(heads):
            k = jnp.concatenate([kn_ref[pl.ds(k0, tq), hs(h)], kp], axis=1)
            s = lax.dot_general(q[h], k, (((1,), (1,)), ((), ())), preferred_element_type=F32)
            if masked:
                s = s + bias_s[...]
            slabs = [s[:, hs(j)] for j in range(tq // LANES)]
            part = functools.reduce(jnp.maximum, slabs)
            m_prev = m_s[h]
            m_new = jnp.maximum(m_prev, jnp.max(part, axis=1, keepdims=True))
            p = jnp.concatenate([jnp.exp2((sl - m_new) * c) for sl in slabs], axis=1)
            alpha = jnp.exp2((m_prev - m_new) * c)
            vv = jnp.concatenate([v_ref[pl.ds(k0, tq), hs(h)], ones], axis=1)
            acc = acc_s[h]
            acc = jnp.concatenate([acc[:, :LANES] * alpha, acc[:, LANES:] * alpha], axis=1)
            acc_s[h] = acc + _dot(p.astype(BF16), vv)
            m_s[h] = m_new

    def body(kp2, carry):
        chunk(2 * kp2, False)
        chunk(2 * kp2 + 1, False)
        return carry

    lax.fori_loop(0, qi // 2, body, 0)

    @pl.when(qi % 2 == 1)
    def _():
        chunk(qi - 1, False)

    chunk(qi, True)
    for h in range(heads):
        acc = acc_s[h]
        o_ref[:, hs(h)] = (acc[:, :LANES] / acc[:, LANES:]).astype(o_ref.dtype)


def attention(q_nope, q_pe, k_nope, k_pe, v, *, batch, seq, tq, heads):
    n, d = q_nope.shape
    nq = seq // tq
    w = heads * LANES
    qspec = pl.BlockSpec((tq, w), lambda b, h, i: (b * nq + i, h))
    kspec = pl.BlockSpec((seq, w), lambda b, h, i: (b, h))
    return pl.pallas_call(
        _attn_kernel,
        grid=(batch, MLA_HEADS // heads, nq),
        in_specs=[qspec, qspec, kspec,
                  pl.BlockSpec((seq, LANES), lambda b, h, i: (b, 0)),
                  kspec],
        out_specs=qspec,
        out_shape=jax.ShapeDtypeStruct((n, d), BF16),
        scratch_shapes=[pltpu.VMEM((heads, tq, LANES), F32),
                        pltpu.VMEM((heads, tq, 2 * LANES), F32),
                        pltpu.VMEM((tq, tq), F32)],
        compiler_params=_params("arbitrary", "arbitrary", "arbitrary"),
        name="attention",
    )(q_nope, q_pe, k_nope, k_pe, v)


ROUTER_SPLIT = 4


def _out_router_kernel(o_ref, w_ref, r_ref, g_ref, wr_ref, x_ref, hn_ref, route_ref, cnt_ref, carry):
    i = pl.program_id(0)
    tm = o_ref.shape[0]
    ts = tm // ROUTER_SPLIT
    subs = [slice(k * ts, (k + 1) * ts) for k in range(ROUTER_SPLIT)]

    @pl.when(i == 0)
    def _():
        carry[...] = jnp.zeros_like(carry)

    xs = [r_ref[rows, :] + _dot(o_ref[rows, :], w_ref[...]) for rows in subs]
    hns = []
    for rows, x in zip(subs, xs):
        x_ref[rows, :] = x
        hn = _rms(x, g_ref[...])
        hn_ref[rows, :] = hn
        hns.append(hn)
    logit = []
    for hn in hns:
        hn_hi = hn.astype(BF16)
        hn_lo = (hn - hn_hi.astype(F32)).astype(BF16)
        parts = _dot(hn_hi, wr_ref[...]) + _dot(hn_lo, wr_ref[...])
        logit.append(parts[:, :LANES] + parts[:, LANES:])

    lane = lax.broadcasted_iota(jnp.int32, (ts, LANES), 1)
    lane_f = lane.astype(F32)
    rr = lax.broadcasted_iota(jnp.int32, (ts, ts), 0)
    cc = lax.broadcasted_iota(jnp.int32, (ts, ts), 1)
    tril = jnp.where(cc < rr, 1.0, 0.0).astype(BF16)
    neg = -jnp.inf
    total = carry[...]
    for rows, logits in zip(subs, logit):
        logits = jnp.where(lane < N_EXPERTS, logits, neg)
        m1 = jnp.max(logits, axis=1, keepdims=True)
        e1 = jnp.min(jnp.where(logits == m1, lane_f, float(LANES)), axis=1, keepdims=True)
        rest = jnp.where(lane_f == e1, neg, logits)
        m2 = jnp.max(rest, axis=1, keepdims=True)
        e2 = jnp.min(jnp.where(rest == m2, lane_f, float(LANES)), axis=1, keepdims=True)
        z = jnp.exp(m2 - m1)
        g1 = 1.0 / (1.0 + z)
        g2 = z / (1.0 + z)
        sel1 = lane_f == e1
        sel2 = lane_f == e2
        sel = jnp.where(sel1 | sel2, 1.0, 0.0)
        before = _dot(tril, sel.astype(BF16)) + total
        rank1 = jnp.sum(jnp.where(sel1, before, 0.0), axis=1, keepdims=True)
        rank2 = jnp.sum(jnp.where(sel2, before, 0.0), axis=1, keepdims=True)
        total = total + jnp.sum(sel, axis=0, keepdims=True)
        route = jnp.zeros((ts, LANES), F32)
        for col, val in enumerate((e1, e2, g1, g2, rank1, rank2)):
            route = jnp.where(lane == col, val, route)
        route_ref[rows, :] = route
    carry[...] = total
    cnt_ref[...] = jnp.broadcast_to(total, cnt_ref.shape)


def out_router(o, w_out, res, g, w_router, *, tm):
    n, d = res.shape
    row = lambda w: pl.BlockSpec((tm, w), lambda i: (i, 0))
    full = lambda a: pl.BlockSpec(a.shape, lambda i: (0, 0))
    return pl.pallas_call(
        _out_router_kernel,
        grid=(n // tm,),
        in_specs=[row(d), full(w_out), row(d), full(g), full(w_router)],
        out_specs=[row(d), row(d), row(LANES), pl.BlockSpec((SUBLANES, LANES), lambda i: (0, 0))],
        out_shape=[jax.ShapeDtypeStruct((n, d), F32), jax.ShapeDtypeStruct((n, d), F32),
                   jax.ShapeDtypeStruct((n, LANES), F32),
                   jax.ShapeDtypeStruct((SUBLANES, LANES), F32)],
        scratch_shapes=[pltpu.VMEM((1, LANES), F32)],
        compiler_params=_params("arbitrary"),
        name="out_router",
    )(o, w_out, res, g, w_router)


GATHER_SLOTS = 2
MOE_ROW_PARTS = 4


def _row_copy(src_hbm, row, dst, r, sem):
    return pltpu.make_async_copy(src_hbm.at[pl.ds(row, 1), :], dst.at[pl.ds(r, 1), :], sem)


def _moe_ffn_kernel(src_ref, be_ref, valid_ref, nu_ref, hn_hbm, wg_ref, wu_ref, wd_ref, o_ref, xg, xb, sems,
                    *, per_step):
    i = pl.program_id(0)
    j = pl.program_id(1)
    tm = o_ref.shape[0]
    rows = xg.shape[1]
    n_used = nu_ref[0]
    slot = i % GATHER_SLOTS

    def start_row(blk, sl, r):
        tok = src_ref[blk * tm + jnp.minimum(r, tm - 1)]
        _row_copy(hn_hbm, tok, xg.at[sl], r, sems.at[sl]).start()

    @pl.when((i == 0) & (j == 0))
    def _():
        def first(r, c):
            start_row(0, 0, r)
            return c
        lax.fori_loop(0, rows, first, 0, unroll=8)

    @pl.when((j == 0) & (i <= n_used))
    def _():
        pltpu.make_async_copy(hn_hbm.at[pl.ds(0, rows), :], xg.at[slot], sems.at[slot]).wait()

    @pl.when((j == 0) & (i < n_used))
    def _():
        xb[...] = xg[slot, :tm, :].astype(BF16)

    @pl.when(j == 0)
    def _():
        o_ref[...] = jnp.zeros_like(o_ref)

    def step(m):
        nxt = jnp.minimum(i + 1, n_used - 1)
        for u in range(per_step):
            start_row(nxt, (i + 1) % GATHER_SLOTS, j * per_step + u)
        o_ref[:m, :] += _swiglu_acc(xb[:m, :], wg_ref, wu_ref, wd_ref)

    quarter = tm // MOE_ROW_PARTS
    parts = (valid_ref[i] + quarter - 1) // quarter
    for k in range(1, MOE_ROW_PARTS + 1):
        @pl.when((i < n_used) & (parts == k))
        def _(k=k):
            step(k * quarter)


def moe_ffn(src_tok, block_e, block_valid, n_used, hn, wg, wu, wd, *, tm, tc):
    d = hn.shape[1]
    p = src_tok.shape[0]
    nj = wg.shape[2] // tc
    per_step = -(-tm // nj)
    while (per_step * nj) % SUBLANES:
        per_step += 1

    def blk(i, nu):
        return jnp.minimum(i, nu[0] - 1)

    def col(i, j, nu):
        return jnp.where(i < nu[0], j, nj - 1)

    return pl.pallas_call(
        functools.partial(_moe_ffn_kernel, per_step=per_step),
        grid_spec=pltpu.PrefetchScalarGridSpec(
            num_scalar_prefetch=4,
            grid=(p // tm, nj),
            in_specs=[pl.BlockSpec(memory_space=pl.ANY),
                      pl.BlockSpec((None, d, tc), lambda i, j, s, be, bv, nu: (be[blk(i, nu)], 0, col(i, j, nu))),
                      pl.BlockSpec((None, d, tc), lambda i, j, s, be, bv, nu: (be[blk(i, nu)], 0, col(i, j, nu))),
                      pl.BlockSpec((None, tc, d), lambda i, j, s, be, bv, nu: (be[blk(i, nu)], col(i, j, nu), 0))],
            out_specs=pl.BlockSpec((tm, d), lambda i, j, s, be, bv, nu: (i, 0)),
            scratch_shapes=[pltpu.VMEM((GATHER_SLOTS, per_step * nj, d), F32), pltpu.VMEM((tm, d), BF16),
                            pltpu.SemaphoreType.DMA((GATHER_SLOTS,))]),
        out_shape=jax.ShapeDtypeStruct((p, d), F32),
        compiler_params=_params("arbitrary", "arbitrary"),
        name="moe_ffn",
    )(src_tok, block_e, block_valid, n_used, hn, wg, wu, wd)


def _combine_kernel(dest_ref, x_ref, route_ref, g_ref, y_hbm, o_ref, ya, yb, sems):
    i = pl.program_id(0)
    tm = o_ref.shape[0]
    slot = i % 2

    def copies(blk, sl, r):
        t2 = 2 * (blk * tm + r)
        return (_row_copy(y_hbm, dest_ref[t2], ya.at[sl], r, sems.at[0, sl]),
                _row_copy(y_hbm, dest_ref[t2 + 1], yb.at[sl], r, sems.at[1, sl]))

    def issue(blk, sl):
        def body(r, c):
            ca, cb = copies(blk, sl, r)
            ca.start()
            cb.start()
            return c
        lax.fori_loop(0, tm, body, 0, unroll=8)

    @pl.when(i == 0)
    def _():
        issue(0, 0)

    @pl.when(i + 1 < pl.num_programs(0))
    def _():
        issue(i + 1, 1 - slot)

    pltpu.make_async_copy(y_hbm.at[pl.ds(0, tm), :], ya.at[slot], sems.at[0, slot]).wait()
    pltpu.make_async_copy(y_hbm.at[pl.ds(0, tm), :], yb.at[slot], sems.at[1, slot]).wait()
    route = route_ref[...]
    x = x_ref[...] + (ya[slot] * route[:, 2:3] + yb[slot] * route[:, 3:4])
    o_ref[...] = _rms(x, g_ref[...])


def combine(dest, x, route, g, y_buf, *, tm):
    n, d = x.shape
    return pl.pallas_call(
        _combine_kernel,
        grid_spec=pltpu.PrefetchScalarGridSpec(
            num_scalar_prefetch=1,
            grid=(n // tm,),
            in_specs=[pl.BlockSpec((tm, d), lambda i, dst: (i, 0)),
                      pl.BlockSpec((tm, LANES), lambda i, dst: (i, 0)),
                      pl.BlockSpec((1, d), lambda i, dst: (0, 0)),
                      pl.BlockSpec(memory_space=pl.ANY)],
            out_specs=pl.BlockSpec((tm, d), lambda i, dst: (i, 0)),
            scratch_shapes=[pltpu.VMEM((2, tm, d), F32), pltpu.VMEM((2, tm, d), F32),
                            pltpu.SemaphoreType.DMA((2, 2))]),
        out_shape=jax.ShapeDtypeStruct((n, d), F32),
        compiler_params=_params("arbitrary"),
        name="moe_combine",
    )(dest, x, route, g, y_buf)


def _spread_rope(w):
    half = QK_ROPE // 2
    z = jnp.zeros(w.shape[:-1] + (half,), w.dtype)
    return jnp.concatenate([w[..., :half], z, w[..., half:], z], axis=-1)


def _tile(n, want):
    t = min(n, want)
    assert n % t == 0, (n, t)
    return t


def kernel(x, positions, a_norm_mix, rg_w_in, rg_conv_w, rg_conv_b, rg_w_a, rg_b_a, rg_w_x, rg_b_x, rg_lambda, rg_w_out, a_norm_ffn, ff_w_gate, ff_w_up, ff_w_down, b_norm_mix, mla_w_in, mla_q_norm, mla_w_q_up, mla_kv_norm, mla_w_kv_up, mla_w_out, b_norm_ffn, moe_w_router, moe_w_gate, moe_w_up, moe_w_down, final_norm):
    batch, seq, d = x.shape
    n = batch * seq
    assert a_norm_mix.shape[0] == 1 and b_norm_mix.shape[0] == 1
    xf = x.reshape(n, d)
    vec = lambda a: a.reshape(1, -1).astype(F32)

    wax = jnp.concatenate([rg_w_a[0], rg_w_x[0]], axis=-1).astype(BF16)
    x1 = rglru_layer(xf, vec(a_norm_mix[0]), rg_w_in[0].T.astype(BF16), rg_conv_w[0], vec(rg_conv_b[0]), wax,
                     vec(rg_b_a[0]), vec(rg_b_x[0]), vec(rg_lambda[0]), rg_w_out[0].T.astype(BF16),
                     seq=seq, t=_tile(seq, 256))
    x2 = ffn_dense(x1, vec(a_norm_ffn[0]), ff_w_gate[0], ff_w_up[0], ff_w_down[0],
                   tm=_tile(n, FF_TM), tc=FF_TC)

    w_in = mla_w_in[0]
    w_in_p = jnp.concatenate([w_in[:, :Q_LORA + KV_LORA], _spread_rope(w_in[:, Q_LORA + KV_LORA:])],
                             axis=1).astype(BF16)
    wq = mla_w_q_up[0].reshape(Q_LORA, MLA_HEADS, QK_DIM)
    wq_p = jnp.concatenate([wq[..., :QK_NOPE].reshape(Q_LORA, -1),
                            _spread_rope(wq[..., QK_NOPE:]).reshape(Q_LORA, -1)], axis=1).astype(BF16)
    wkv = mla_w_kv_up[0].reshape(KV_LORA, MLA_HEADS, QK_NOPE + V_DIM)
    wkv_p = jnp.concatenate([wkv[..., :QK_NOPE].reshape(KV_LORA, -1),
                             wkv[..., QK_NOPE:].reshape(KV_LORA, -1)], axis=1).astype(BF16)
    inv_freq = 1.0 / (ROPE_THETA ** (jnp.arange(0, QK_ROPE, 2, dtype=F32) / QK_ROPE))
    freq = _spread_rope(jnp.concatenate([inv_freq, inv_freq])[None, :])
    ones = jnp.ones((1, QK_ROPE // 2), F32)
    sign = _spread_rope(jnp.concatenate([-ones, ones], axis=1))
    pos = positions.reshape(n, 1).astype(jnp.int32)
    q_nope, q_pe, k_nope, k_pe, v = mla_proj(
        x2, pos, vec(b_norm_mix[0]), w_in_p, vec(mla_q_norm[0]), wq_p, vec(mla_kv_norm[0]), wkv_p,
        freq, sign, tm=_tile(n, 512))
    o = attention(q_nope, q_pe, k_nope, k_pe, v, batch=batch, seq=seq, tq=_tile(seq, 512), heads=8)
    w_router = jnp.pad(moe_w_router[0].astype(F32), ((0, 0), (0, LANES - N_EXPERTS)))
    wr_hi = w_router.astype(BF16)
    w_router = jnp.concatenate([wr_hi, (w_router - wr_hi.astype(F32)).astype(BF16)], axis=1)
    x3, hn, route, cnt = out_router(o, mla_w_out[0].astype(BF16), x2, vec(b_norm_ffn[0]), w_router,
                                    tm=_tile(n, 512))

    tm = _tile(2 * n, FF_TM)
    counts = cnt[0, :N_EXPERTS].astype(jnp.int32)
    padded = ((counts + tm - 1) // tm) * tm
    pad_end = jnp.cumsum(padded)
    pad_start = pad_end - padded
    experts = route[:, 0:2].astype(jnp.int32)
    ranks = route[:, 4:6].astype(jnp.int32)
    dest = (pad_start[experts] + ranks).reshape(-1)
    p = 2 * n + N_EXPERTS * tm
    n_blocks = p // tm
    src_tok = jnp.zeros((p,), jnp.int32).at[dest].set(jnp.arange(2 * n, dtype=jnp.int32) // 2,
                                                      unique_indices=True, mode='promise_in_bounds')
    block_start = jnp.arange(n_blocks, dtype=jnp.int32) * tm
    block_e = jnp.minimum(jnp.sum(block_start[:, None] >= pad_end[None, :], axis=1),
                          N_EXPERTS - 1).astype(jnp.int32)
    block_valid = jnp.clip((pad_start + counts)[block_e] - block_start, 0, tm).astype(jnp.int32)
    n_used = (pad_end[-1:] // tm).astype(jnp.int32)

    y_buf = moe_ffn(src_tok, block_e, block_valid, n_used, hn, moe_w_gate[0], moe_w_up[0], moe_w_down[0],
                    tm=tm, tc=FF_TC)
    out = combine(dest, x3, route, vec(final_norm), y_buf, tm=_tile(n, 256))
    return out.reshape(batch, seq, d)
```

```python
import functools
import math

import jax
import jax.numpy as jnp
from jax import lax
from jax.experimental import pallas as pl
from jax.experimental.pallas import tpu as pltpu

EPS = 1e-6
RG_HEADS = 16
RG_BLOCK = 128
CONV_W = 4
RG_C = 8.0
MLA_HEADS = 16
Q_LORA = 512
KV_LORA = 512
QK_NOPE = 128
QK_ROPE = 64
V_DIM = 128
QK_DIM = QK_NOPE + QK_ROPE
ROPE_THETA = 10000.0
N_EXPERTS = 8
LANES = 128
SUBLANES = 8
VMEM_LIMIT_BYTES = 56 * 1024 * 1024
FF_TC = 256
FF_TM = 1024

BF16 = jnp.bfloat16
F32 = jnp.float32


def _params(*sem):
    return pltpu.CompilerParams(dimension_semantics=sem, vmem_limit_bytes=VMEM_LIMIT_BYTES)


def _rms(x, g):
    return x * lax.rsqrt(jnp.mean(x * x, axis=-1, keepdims=True) + EPS) * g


def _dot(a, b):
    return jnp.dot(a, b, preferred_element_type=F32)


def _sigmoid(x):
    return 1.0 / (1.0 + jnp.exp(-x))


def _sigmoid_tanh(x):
    return 0.5 * jnp.tanh(0.5 * x) + 0.5


def _gelu_tanh(x):
    c = math.sqrt(2.0 / math.pi)
    return 0.5 * x * (1.0 + jnp.tanh(c * (x + 0.044715 * (x * x * x))))


def _softplus(z):
    return jnp.maximum(z, 0.0) + jnp.log1p(jnp.exp(-jnp.abs(z)))


def _rglru_layer_kernel(xn_ref, xp_ref, g_ref, win_hbm, cw_ref, cb_ref, wax_ref, ba_ref, bx_ref, lam_ref, wout_hbm,
                        o_ref, win_s, wout_s, u_s, y_s, gate_s, hn_s, xbuf, a_s, b_s, h_s, sem, *, ns):
    q = pl.program_id(0)
    s = q % ns
    t, d = xn_ref.shape
    cur = q % 2
    nxt = (q + 1) % 2

    def project(x_ref):
        return _dot(_rms(x_ref[...], g_ref[...]).astype(BF16), win_s[...]).astype(BF16)

    @pl.when(q == 0)
    def _():
        cin = pltpu.make_async_copy(win_hbm, win_s, sem.at[0])
        cout = pltpu.make_async_copy(wout_hbm, wout_s, sem.at[1])
        cin.start()
        cout.start()
        cin.wait()
        cout.wait()
        u_s[0] = project(xp_ref)
        y_s[...] = jnp.zeros_like(y_s)

    @pl.when(s == 0)
    def _():
        xbuf[0:SUBLANES, :] = jnp.zeros((SUBLANES, d), F32)
        h_s[...] = jnp.zeros_like(h_s)

    @pl.when(s > 0)
    def _():
        xbuf[0:SUBLANES, :] = xbuf[t:t + SUBLANES, :]

    xbuf[SUBLANES:t + SUBLANES, :] = u_s[cur, :, d:].astype(F32)
    gate_s[...] = u_s[cur, :, :d]
    hn_s[...] = _rms(xn_ref[...], g_ref[...]).astype(BF16)
    pw = u_s.shape[2] // RG_HEADS
    ow = 2 * d // RG_HEADS

    row = lax.broadcasted_iota(jnp.int32, (t // SUBLANES, SUBLANES, RG_BLOCK), 1)
    for h in range(RG_HEADS):
        c0, c1 = h * RG_BLOCK, (h + 1) * RG_BLOCK
        xc = cb_ref[:, c0:c1] + jnp.zeros((t, RG_BLOCK), F32)
        for k in range(CONV_W):
            sh = CONV_W - 1 - k
            xc = xc + xbuf[SUBLANES - sh:SUBLANES - sh + t, c0:c1] * cw_ref[k:k + 1, c0:c1]
        gg = _dot(xc.astype(BF16), wax_ref[h])
        u_s[nxt, :, h * pw:(h + 1) * pw] = _dot(hn_s[...], win_s[:, h * pw:(h + 1) * pw]).astype(BF16)
        if h % 2 == 0:
            oc = slice((h // 2) * ow, (h // 2 + 1) * ow)
            o_ref[:, oc] = xp_ref[:, oc] + _dot(y_s[nxt], wout_s[:, oc])
        r = _sigmoid_tanh(gg[:, :RG_BLOCK] + ba_ref[:, c0:c1])
        i = _sigmoid_tanh(gg[:, RG_BLOCK:] + bx_ref[:, c0:c1])
        log_a = -RG_C * r * _softplus(-lam_ref[:, c0:c1])
        a = jnp.exp(log_a)
        om = 1.0 - a * a
        b = jnp.where(om > 0.0, om * lax.rsqrt(om), 0.0) * i * xc
        a = a.reshape(t // SUBLANES, SUBLANES, RG_BLOCK)
        b = b.reshape(t // SUBLANES, SUBLANES, RG_BLOCK)
        for sft in (1, 2, 4):
            a_sh = pltpu.roll(a, sft, axis=1)
            b_sh = pltpu.roll(b, sft, axis=1)
            m = row >= sft
            b = jnp.where(m, a * b_sh + b, b)
            a = jnp.where(m, a * a_sh, a)
        a_s[:, c0:c1] = a.reshape(t, RG_BLOCK)
        b_s[:, c0:c1] = b.reshape(t, RG_BLOCK)

    def group(gi, hc):
        r0 = pl.multiple_of(gi * SUBLANES, SUBLANES)
        rows = b_s[pl.ds(r0, SUBLANES), :] + a_s[pl.ds(r0, SUBLANES), :] * hc
        b_s[pl.ds(r0, SUBLANES), :] = rows
        return rows[SUBLANES - 1:SUBLANES, :]

    h_s[...] = lax.fori_loop(0, t // SUBLANES, group, h_s[...])
    y_s[cur] = (b_s[...] * _gelu_tanh(gate_s[...].astype(F32))).astype(BF16)


def rglru_layer(x, g, w_in, conv_w, conv_b, wax, b_a, b_x, lam, w_out, *, seq, t):
    n, d = x.shape
    ns = seq // t
    total = n // t
    vec = pl.BlockSpec((1, d), lambda q: (0, 0))
    hbm = pl.BlockSpec(memory_space=pl.ANY)
    return pl.pallas_call(
        functools.partial(_rglru_layer_kernel, ns=ns),
        grid=(total + 1,),
        in_specs=[pl.BlockSpec((t, d), lambda q: (jnp.minimum(q + 1, total - 1), 0)),
                  pl.BlockSpec((t, d), lambda q: (jnp.maximum(q - 1, 0), 0)),
                  vec, hbm,
                  pl.BlockSpec((CONV_W, d), lambda q: (0, 0)),
                  vec,
                  pl.BlockSpec((RG_HEADS, RG_BLOCK, 2 * RG_BLOCK), lambda q: (0, 0, 0)),
                  vec, vec, vec, hbm],
        out_specs=pl.BlockSpec((t, d), lambda q: (jnp.maximum(q - 1, 0), 0)),
        out_shape=jax.ShapeDtypeStruct((n, d), F32),
        scratch_shapes=[pltpu.VMEM((d, 2 * d), BF16),
                        pltpu.VMEM((d, d), BF16),
                        pltpu.VMEM((2, t, 2 * d), BF16),
                        pltpu.VMEM((2, t, d), BF16),
                        pltpu.VMEM((t, d), BF16),
                        pltpu.VMEM((t, d), BF16),
                        pltpu.VMEM((t + SUBLANES, d), F32),
                        pltpu.VMEM((t, d), F32),
                        pltpu.VMEM((t, d), F32),
                        pltpu.VMEM((1, d), F32),
                        pltpu.SemaphoreType.DMA((2,))],
        compiler_params=_params("arbitrary"),
        name="rglru_layer",
    )(x, x, g, w_in, conv_w, conv_b, wax, b_a, b_x, lam, w_out)


def _swiglu_acc(h, wg_ref, wu_ref, wd_ref):
    g = _dot(h, wg_ref[...].astype(BF16))
    u = _dot(h, wu_ref[...].astype(BF16))
    a = (g * _sigmoid(g) * u).astype(BF16)
    return _dot(a, wd_ref[...].astype(BF16))


def _ffn_kernel(x_ref, g_ref, wg_ref, wu_ref, wd_ref, o_ref, hn_ref):
    @pl.when(pl.program_id(1) == 0)
    def _():
        x = x_ref[...]
        hn_ref[...] = _rms(x, g_ref[...]).astype(BF16)
        o_ref[...] = x

    o_ref[...] += _swiglu_acc(hn_ref[...], wg_ref, wu_ref, wd_ref)


def ffn_dense(x, g, wg, wu, wd, *, tm, tc):
    n, d = x.shape
    f = wg.shape[1]
    return pl.pallas_call(
        _ffn_kernel,
        grid=(n // tm, f // tc),
        in_specs=[pl.BlockSpec((tm, d), lambda i, j: (i, 0)),
                  pl.BlockSpec((1, d), lambda i, j: (0, 0)),
                  pl.BlockSpec((d, tc), lambda i, j: (0, j)),
                  pl.BlockSpec((d, tc), lambda i, j: (0, j)),
                  pl.BlockSpec((tc, d), lambda i, j: (j, 0))],
        out_specs=pl.BlockSpec((tm, d), lambda i, j: (i, 0)),
        out_shape=jax.ShapeDtypeStruct((n, d), F32),
        scratch_shapes=[pltpu.VMEM((tm, d), BF16)],
        compiler_params=_params("parallel", "arbitrary"),
        name="ffn_dense",
    )(x, g, wg, wu, wd)


MLA_SPLIT = 2


def _mla_proj_kernel(x_ref, pos_ref, g_ref, win_ref, qn_ref, wq_ref, kvn_ref, wkv_ref,
                     freq_ref, sign_ref, qnope_ref, qpe_ref, knope_ref, kpe_ref, v_ref):
    tm, d = x_ref.shape
    ts = tm // MLA_SPLIT
    subs = [slice(k * ts, (k + 1) * ts) for k in range(MLA_SPLIT)]

    def rope(xs, cos, sin):
        return xs * cos + pltpu.roll(xs, LANES // 2, axis=1) * sin

    us = [_dot(_rms(x_ref[rows, :], g_ref[...]).astype(BF16), win_ref[...]) for rows in subs]
    trig = []
    for rows in subs:
        ang = pos_ref[rows, :].astype(F32) * freq_ref[...]
        trig.append((jnp.cos(ang), jnp.sin(ang) * sign_ref[...]))
    qs = [_dot(_rms(u[:, :Q_LORA], qn_ref[...]).astype(BF16), wq_ref[...]) for u in us]
    kvs = [_dot(_rms(u[:, Q_LORA:Q_LORA + KV_LORA], kvn_ref[...]).astype(BF16), wkv_ref[...]) for u in us]
    for rows, u, (cos, sin), q, kv in zip(subs, us, trig, qs, kvs):
        kpe_ref[rows, :] = rope(u[:, Q_LORA + KV_LORA:], cos, sin).astype(BF16)
        qnope_ref[rows, :] = q[:, :d].astype(BF16)
        for h in range(MLA_HEADS):
            c0, c1 = h * LANES, (h + 1) * LANES
            qpe_ref[rows, c0:c1] = rope(q[:, d + c0:d + c1], cos, sin).astype(BF16)
        knope_ref[rows, :] = kv[:, :d].astype(BF16)
        v_ref[rows, :] = kv[:, d:].astype(BF16)


def mla_proj(x, pos, g, w_in, q_norm, w_q, kv_norm, w_kv, freq, sign, *, tm):
    n, d = x.shape
    row = lambda w: pl.BlockSpec((tm, w), lambda i: (i, 0))
    full = lambda a: pl.BlockSpec(a.shape, lambda i: (0, 0))
    big = jax.ShapeDtypeStruct((n, d), BF16)
    return pl.pallas_call(
        _mla_proj_kernel,
        grid=(n // tm,),
        in_specs=[row(d), row(1), full(g), full(w_in), full(q_norm), full(w_q), full(kv_norm),
                  full(w_kv), full(freq), full(sign)],
        out_specs=[row(d), row(d), row(d), row(LANES), row(d)],
        out_shape=[big, big, big, jax.ShapeDtypeStruct((n, LANES), BF16), big],
        compiler_params=_params("parallel"),
        name="mla_proj",
    )(x, pos, g, w_in, q_norm, w_q, kv_norm, w_kv, freq, sign)


def _attn_kernel(qn_ref, qp_ref, kn_ref, kp_ref, v_ref, o_ref, m_s, acc_s, bias_s):
    qi = pl.program_id(2)
    tq = qn_ref.shape[0]
    heads = qn_ref.shape[1] // LANES
    c = (QK_DIM ** -0.5) * math.log2(math.e)
    hs = lambda h: slice(h * LANES, (h + 1) * LANES)
    q = [jnp.concatenate([qn_ref[:, hs(h)], qp_ref[:, hs(h)]], axis=1) for h in range(heads)]
    ones = jnp.ones((tq, LANES), BF16)
    m_s[...] = jnp.full_like(m_s, -jnp.inf)
    acc_s[...] = jnp.zeros_like(acc_s)

    @pl.when((pl.program_id(0) == 0) & (pl.program_id(1) == 0) & (qi == 0))
    def _():
        q_idx = lax.broadcasted_iota(jnp.int32, (tq, tq), 0)
        k_idx = lax.broadcasted_iota(jnp.int32, (tq, tq), 1)
        bias_s[...] = jnp.where(k_idx <= q_idx, 0.0, -jnp.inf)

    def chunk(kj, masked):
        k0 = pl.multiple_of(kj * tq, tq)
        kp = kp_ref[pl.ds(k0, tq), :]
        for h in range(heads):
            k = jnp.concatenate([kn_ref[pl.ds(k0, tq), hs(h)], kp], axis=1)
            s = lax.dot_general(q[h], k, (((1,), (1,)), ((), ())), preferred_element_type=F32)
            if masked:
                s = s + bias_s[...]
            slabs = [s[:, hs(j)] for j in range(tq // LANES)]
            part = functools.reduce(jnp.maximum, slabs)
            m_prev = m_s[h]
            m_new = jnp.maximum(m_prev, jnp.max(part, axis=1, keepdims=True))
            p = jnp.concatenate([jnp.exp2((sl - m_new) * c) for sl in slabs], axis=1)
            alpha = jnp.exp2((m_prev - m_new) * c)
            vv = jnp.concatenate([v_ref[pl.ds(k0, tq), hs(h)], ones], axis=1)
            acc = acc_s[h]
            acc = jnp.concatenate([acc[:, :LANES] * alpha, acc[:, LANES:] * alpha], axis=1)
            acc_s[h] = acc + _dot(p.astype(BF16), vv)
            m_s[h] = m_new

    def body(kp2, carry):
        chunk(2 * kp2, False)
        chunk(2 * kp2 + 1, False)
        return carry

    lax.fori_loop(0, qi // 2, body, 0)

    @pl.when(qi % 2 == 1)
    def _():
        chunk(qi - 1, False)

    chunk(qi, True)
    for h in range(heads):
        acc = acc_s[h]
        o_ref[:, hs(h)] = (acc[:, :LANES] / acc[:, LANES:]).astype(o_ref.dtype)


def attention(q_nope, q_pe, k_nope, k_pe, v, *, batch, seq, tq, heads):
    n, d = q_nope.shape
    nq = seq // tq
    w = heads * LANES
    qspec = pl.BlockSpec((tq, w), lambda b, h, i: (b * nq + i, h))
    kspec = pl.BlockSpec((seq, w), lambda b, h, i: (b, h))
    return pl.pallas_call(
        _attn_kernel,
        grid=(batch, MLA_HEADS // heads, nq),
        in_specs=[qspec, qspec, kspec,
                  pl.BlockSpec((seq, LANES), lambda b, h, i: (b, 0)),
                  kspec],
        out_specs=qspec,
        out_shape=jax.ShapeDtypeStruct((n, d), BF16),
        scratch_shapes=[pltpu.VMEM((heads, tq, LANES), F32),
                        pltpu.VMEM((heads, tq, 2 * LANES), F32),
                        pltpu.VMEM((tq, tq), F32)],
        compiler_params=_params("arbitrary", "arbitrary", "arbitrary"),
        name="attention",
    )(q_nope, q_pe, k_nope, k_pe, v)


ROUTER_SPLIT = 4


def _out_router_kernel(o_ref, w_ref, r_ref, g_ref, wr_ref, x_ref, hn_ref, route_ref, cnt_ref, carry):
    i = pl.program_id(0)
    tm = o_ref.shape[0]
    ts = tm // ROUTER_SPLIT
    subs = [slice(k * ts, (k + 1) * ts) for k in range(ROUTER_SPLIT)]

    @pl.when(i == 0)
    def _():
        carry[...] = jnp.zeros_like(carry)

    xs = [r_ref[rows, :] + _dot(o_ref[rows, :], w_ref[...]) for rows in subs]
    hns = []
    for rows, x in zip(subs, xs):
        x_ref[rows, :] = x
        hn = _rms(x, g_ref[...])
        hn_ref[rows, :] = hn
        hns.append(hn)
    logit = []
    for hn in hns:
        hn_hi = hn.astype(BF16)
        hn_lo = (hn - hn_hi.astype(F32)).astype(BF16)
        parts = _dot(hn_hi, wr_ref[...]) + _dot(hn_lo, wr_ref[...])
        logit.append(parts[:, :LANES] + parts[:, LANES:])

    lane = lax.broadcasted_iota(jnp.int32, (ts, LANES), 1)
    lane_f = lane.astype(F32)
    rr = lax.broadcasted_iota(jnp.int32, (ts, ts), 0)
    cc = lax.broadcasted_iota(jnp.int32, (ts, ts), 1)
    tril = jnp.where(cc < rr, 1.0, 0.0).astype(BF16)
    neg = -jnp.inf
    total = carry[...]
    for rows, logits in zip(subs, logit):
        logits = jnp.where(lane < N_EXPERTS, logits, neg)
        m1 = jnp.max(logits, axis=1, keepdims=True)
        e1 = jnp.min(jnp.where(logits == m1, lane_f, float(LANES)), axis=1, keepdims=True)
        rest = jnp.where(lane_f == e1, neg, logits)
        m2 = jnp.max(rest, axis=1, keepdims=True)
        e2 = jnp.min(jnp.where(rest == m2, lane_f, float(LANES)), axis=1, keepdims=True)
        z = jnp.exp(m2 - m1)
        g1 = 1.0 / (1.0 + z)
        g2 = z / (1.0 + z)
        sel1 = lane_f == e1
        sel2 = lane_f == e2
        sel = jnp.where(sel1 | sel2, 1.0, 0.0)
        before = _dot(tril, sel.astype(BF16)) + total
        rank1 = jnp.sum(jnp.where(sel1, before, 0.0), axis=1, keepdims=True)
        rank2 = jnp.sum(jnp.where(sel2, before, 0.0), axis=1, keepdims=True)
        total = total + jnp.sum(sel, axis=0, keepdims=True)
        route = jnp.zeros((ts, LANES), F32)
        for col, val in enumerate((e1, e2, g1, g2, rank1, rank2)):
            route = jnp.where(lane == col, val, route)
        route_ref[rows, :] = route
    carry[...] = total
    cnt_ref[...] = jnp.broadcast_to(total, cnt_ref.shape)


def out_router(o, w_out, res, g, w_router, *, tm):
    n, d = res.shape
    row = lambda w: pl.BlockSpec((tm, w), lambda i: (i, 0))
    full = lambda a: pl.BlockSpec(a.shape, lambda i: (0, 0))
    return pl.pallas_call(
        _out_router_kernel,
        grid=(n // tm,),
        in_specs=[row(d), full(w_out), row(d), full(g), full(w_router)],
        out_specs=[row(d), row(d), row(LANES), pl.BlockSpec((SUBLANES, LANES), lambda i: (0, 0))],
        out_shape=[jax.ShapeDtypeStruct((n, d), F32), jax.ShapeDtypeStruct((n, d), F32),
                   jax.ShapeDtypeStruct((n, LANES), F32),
                   jax.ShapeDtypeStruct((SUBLANES, LANES), F32)],
        scratch_shapes=[pltpu.VMEM((1, LANES), F32)],
        compiler_params=_params("arbitrary"),
        name="out_router",
    )(o, w_out, res, g, w_router)


GATHER_SLOTS = 2
MOE_ROW_PARTS = 8


def _row_copy(src_hbm, row, dst, r, sem):
    return pltpu.make_async_copy(src_hbm.at[pl.ds(row, 1), :], dst.at[pl.ds(r, 1), :], sem)


def _moe_ffn_kernel(src_ref, be_ref, valid_ref, nu_ref, hn_hbm, wg_ref, wu_ref, wd_ref, o_ref, xg, xb, sems,
                    *, per_step):
    i = pl.program_id(0)
    j = pl.program_id(1)
    tm = o_ref.shape[0]
    rows = xg.shape[1]
    n_used = nu_ref[0]
    slot = i % GATHER_SLOTS

    def start_row(blk, sl, r):
        tok = src_ref[blk * tm + jnp.minimum(r, tm - 1)]
        _row_copy(hn_hbm, tok, xg.at[sl], r, sems.at[sl]).start()

    @pl.when((i == 0) & (j == 0))
    def _():
        def first(r, c):
            start_row(0, 0, r)
            return c
        lax.fori_loop(0, rows, first, 0, unroll=8)

    @pl.when((j == 0) & (i <= n_used))
    def _():
        pltpu.make_async_copy(hn_hbm.at[pl.ds(0, rows), :], xg.at[slot], sems.at[slot]).wait()

    @pl.when((j == 0) & (i < n_used))
    def _():
        xb[...] = xg[slot, :tm, :].astype(BF16)

    @pl.when(j == 0)
    def _():
        o_ref[...] = jnp.zeros_like(o_ref)

    def step(m):
        nxt = jnp.minimum(i + 1, n_used - 1)
        for u in range(per_step):
            start_row(nxt, (i + 1) % GATHER_SLOTS, j * per_step + u)
        o_ref[:m, :] += _swiglu_acc(xb[:m, :], wg_ref, wu_ref, wd_ref)

    quarter = tm // MOE_ROW_PARTS
    parts = (valid_ref[i] + quarter - 1) // quarter
    for k in range(1, MOE_ROW_PARTS + 1):
        @pl.when((i < n_used) & (parts == k))
        def _(k=k):
            step(k * quarter)


def moe_ffn(src_tok, block_e, block_valid, n_used, hn, wg, wu, wd, *, tm, tc):
    d = hn.shape[1]
    p = src_tok.shape[0]
    nj = wg.shape[2] // tc
    per_step = -(-tm // nj)
    while (per_step * nj) % SUBLANES:
        per_step += 1

    def blk(i, nu):
        return jnp.minimum(i, nu[0] - 1)

    def col(i, j, nu):
        return jnp.where(i < nu[0], j, nj - 1)

    return pl.pallas_call(
        functools.partial(_moe_ffn_kernel, per_step=per_step),
        grid_spec=pltpu.PrefetchScalarGridSpec(
            num_scalar_prefetch=4,
            grid=(p // tm, nj),
            in_specs=[pl.BlockSpec(memory_space=pl.ANY),
                      pl.BlockSpec((None, d, tc), lambda i, j, s, be, bv, nu: (be[blk(i, nu)], 0, col(i, j, nu))),
                      pl.BlockSpec((None, d, tc), lambda i, j, s, be, bv, nu: (be[blk(i, nu)], 0, col(i, j, nu))),
                      pl.BlockSpec((None, tc, d), lambda i, j, s, be, bv, nu: (be[blk(i, nu)], col(i, j, nu), 0))],
            out_specs=pl.BlockSpec((tm, d), lambda i, j, s, be, bv, nu: (i, 0)),
            scratch_shapes=[pltpu.VMEM((GATHER_SLOTS, per_step * nj, d), F32), pltpu.VMEM((tm, d), BF16),
                            pltpu.SemaphoreType.DMA((GATHER_SLOTS,))]),
        out_shape=jax.ShapeDtypeStruct((p, d), F32),
        compiler_params=_params("arbitrary", "arbitrary"),
        name="moe_ffn",
    )(src_tok, block_e, block_valid, n_used, hn, wg, wu, wd)


def _combine_kernel(dest_ref, x_ref, route_ref, g_ref, y_hbm, o_ref, ya, yb, sems):
    i = pl.program_id(0)
    tm = o_ref.shape[0]
    slot = i % 2

    def copies(blk, sl, r):
        t2 = 2 * (blk * tm + r)
        return (_row_copy(y_hbm, dest_ref[t2], ya.at[sl], r, sems.at[0, sl]),
                _row_copy(y_hbm, dest_ref[t2 + 1], yb.at[sl], r, sems.at[1, sl]))

    def issue(blk, sl):
        def body(r, c):
            ca, cb = copies(blk, sl, r)
            ca.start()
            cb.start()
            return c
        lax.fori_loop(0, tm, body, 0, unroll=8)

    @pl.when(i == 0)
    def _():
        issue(0, 0)

    @pl.when(i + 1 < pl.num_programs(0))
    def _():
        issue(i + 1, 1 - slot)

    pltpu.make_async_copy(y_hbm.at[pl.ds(0, tm), :], ya.at[slot], sems.at[0, slot]).wait()
    pltpu.make_async_copy(y_hbm.at[pl.ds(0, tm), :], yb.at[slot], sems.at[1, slot]).wait()
    route = route_ref[...]
    x = x_ref[...] + (ya[slot] * route[:, 2:3] + yb[slot] * route[:, 3:4])
    o_ref[...] = _rms(x, g_ref[...])


def combine(dest, x, route, g, y_buf, *, tm):
    n, d = x.shape
    return pl.pallas_call(
        _combine_kernel,
        grid_spec=pltpu.PrefetchScalarGridSpec(
            num_scalar_prefetch=1,
            grid=(n // tm,),
            in_specs=[pl.BlockSpec((tm, d), lambda i, dst: (i, 0)),
                      pl.BlockSpec((tm, LANES), lambda i, dst: (i, 0)),
                      pl.BlockSpec((1, d), lambda i, dst: (0, 0)),
                      pl.BlockSpec(memory_space=pl.ANY)],
            out_specs=pl.BlockSpec((tm, d), lambda i, dst: (i, 0)),
            scratch_shapes=[pltpu.VMEM((2, tm, d), F32), pltpu.VMEM((2, tm, d), F32),
                            pltpu.SemaphoreType.DMA((2, 2))]),
        out_shape=jax.ShapeDtypeStruct((n, d), F32),
        compiler_params=_params("arbitrary"),
        name="moe_combine",
    )(dest, x, route, g, y_buf)


def _spread_rope(w):
    half = QK_ROPE // 2
    z = jnp.zeros(w.shape[:-1] + (half,), w.dtype)
    return jnp.concatenate([w[..., :half], z, w[..., half:], z], axis=-1)


def _tile(n, want):
    t = min(n, want)
    assert n % t == 0, (n, t)
    return t


def kernel(x, positions, a_norm_mix, rg_w_in, rg_conv_w, rg_conv_b, rg_w_a, rg_b_a, rg_w_x, rg_b_x, rg_lambda, rg_w_out, a_norm_ffn, ff_w_gate, ff_w_up, ff_w_down, b_norm_mix, mla_w_in, mla_q_norm, mla_w_q_up, mla_kv_norm, mla_w_kv_up, mla_w_out, b_norm_ffn, moe_w_router, moe_w_gate, moe_w_up, moe_w_down, final_norm):
    batch, seq, d = x.shape
    n = batch * seq
    assert a_norm_mix.shape[0] == 1 and b_norm_mix.shape[0] == 1
    xf = x.reshape(n, d)
    vec = lambda a: a.reshape(1, -1).astype(F32)

    wax = jnp.concatenate([rg_w_a[0], rg_w_x[0]], axis=-1).astype(BF16)
    x1 = rglru_layer(xf, vec(a_norm_mix[0]), rg_w_in[0].astype(BF16), rg_conv_w[0], vec(rg_conv_b[0]), wax,
                     vec(rg_b_a[0]), vec(rg_b_x[0]), vec(rg_lambda[0]), rg_w_out[0].astype(BF16),
                     seq=seq, t=_tile(seq, 256))
    x2 = ffn_dense(x1, vec(a_norm_ffn[0]), ff_w_gate[0], ff_w_up[0], ff_w_down[0],
                   tm=_tile(n, FF_TM), tc=FF_TC)

    w_in = mla_w_in[0]
    w_in_p = jnp.concatenate([w_in[:, :Q_LORA + KV_LORA], _spread_rope(w_in[:, Q_LORA + KV_LORA:])],
                             axis=1).astype(BF16)
    wq = mla_w_q_up[0].reshape(Q_LORA, MLA_HEADS, QK_DIM)
    wq_p = jnp.concatenate([wq[..., :QK_NOPE].reshape(Q_LORA, -1),
                            _spread_rope(wq[..., QK_NOPE:]).reshape(Q_LORA, -1)], axis=1).astype(BF16)
    wkv = mla_w_kv_up[0].reshape(KV_LORA, MLA_HEADS, QK_NOPE + V_DIM)
    wkv_p = jnp.concatenate([wkv[..., :QK_NOPE].reshape(KV_LORA, -1),
                             wkv[..., QK_NOPE:].reshape(KV_LORA, -1)], axis=1).astype(BF16)
    inv_freq = 1.0 / (ROPE_THETA ** (jnp.arange(0, QK_ROPE, 2, dtype=F32) / QK_ROPE))
    freq = _spread_rope(jnp.concatenate([inv_freq, inv_freq])[None, :])
    ones = jnp.ones((1, QK_ROPE // 2), F32)
    sign = _spread_rope(jnp.concatenate([-ones, ones], axis=1))
    pos = positions.reshape(n, 1).astype(jnp.int32)
    q_nope, q_pe, k_nope, k_pe, v = mla_proj(
        x2, pos, vec(b_norm_mix[0]), w_in_p, vec(mla_q_norm[0]), wq_p, vec(mla_kv_norm[0]), wkv_p,
        freq, sign, tm=_tile(n, 512))
    o = attention(q_nope, q_pe, k_nope, k_pe, v, batch=batch, seq=seq, tq=_tile(seq, 512), heads=8)
    w_router = jnp.pad(moe_w_router[0].astype(F32), ((0, 0), (0, LANES - N_EXPERTS)))
    wr_hi = w_router.astype(BF16)
    w_router = jnp.concatenate([wr_hi, (w_router - wr_hi.astype(F32)).astype(BF16)], axis=1)
    x3, hn, route, cnt = out_router(o, mla_w_out[0].astype(BF16), x2, vec(b_norm_ffn[0]), w_router,
                                    tm=_tile(n, 512))

    tm = _tile(2 * n, FF_TM)
    counts = cnt[0, :N_EXPERTS].astype(jnp.int32)
    padded = ((counts + tm - 1) // tm) * tm
    pad_end = jnp.cumsum(padded)
    pad_start = pad_end - padded
    experts = route[:, 0:2].astype(jnp.int32)
    ranks = route[:, 4:6].astype(jnp.int32)
    dest = (pad_start[experts] + ranks).reshape(-1)
    p = 2 * n + N_EXPERTS * tm
    n_blocks = p // tm
    src_tok = jnp.zeros((p,), jnp.int32).at[dest].set(jnp.arange(2 * n, dtype=jnp.int32) // 2,
                                                      unique_indices=True, mode='promise_in_bounds')
    block_start = jnp.arange(n_blocks, dtype=jnp.int32) * tm
    block_e = jnp.minimum(jnp.sum(block_start[:, None] >= pad_end[None, :], axis=1),
                          N_EXPERTS - 1).astype(jnp.int32)
    block_valid = jnp.clip((pad_start + counts)[block_e] - block_start, 0, tm).astype(jnp.int32)
    n_used = (pad_end[-1:] // tm).astype(jnp.int32)

    y_buf = moe_ffn(src_tok, block_e, block_valid, n_used, hn, moe_w_gate[0], moe_w_up[0], moe_w_down[0],
                    tm=tm, tc=FF_TC)
    out = combine(dest, x3, route, vec(final_norm), y_buf, tm=_tile(n, 256))
    return out.reshape(batch, seq, d)
```
